```python
import math
import jax
import jax.numpy as jnp
from jax import lax
import numpy as np

D_MODEL = 1024
BATCH = 4
SEQ = 4096
DEPTH = 2

GRID_W = 64
CTX_LEN = 256
EPS = 1e-6
N_MOD = 6
F32 = jnp.float32

SSD_HEADS = 8
SSD_HEAD_DIM = 64
SSD_INNER = SSD_HEADS * SSD_HEAD_DIM
SSD_GROUPS = 2
SSD_STATE = 64
SSD_CONV_W = 3
SSD_CHUNK = 128
SSD_CONV_DIM = SSD_INNER + 2 * SSD_GROUPS * SSD_STATE

DIFF_HEADS = 4
DIFF_QK_DIM = 32
DIFF_V_DIM = 2 * DIFF_QK_DIM
DIFF_INNER = DIFF_HEADS * DIFF_V_DIM

MLA_HEADS = 4
MLA_Q_LORA = 192
MLA_KV_LORA = 128
MLA_NOPE = 64
MLA_ROPE = 32
MLA_V = 64
MLA_INNER = MLA_HEADS * MLA_V

MIX_WIDTH = SSD_INNER + DIFF_INNER + MLA_INNER

SSD_COLS = 2 * SSD_INNER + 2 * SSD_GROUPS * SSD_STATE + 2 * SSD_HEADS
DIFF_COLS = 3 * DIFF_HEADS * DIFF_V_DIM
MLA_COLS = MLA_Q_LORA + MLA_KV_LORA + MLA_ROPE
IN_COLS = SSD_COLS + DIFF_COLS + MLA_COLS

N_EXPERTS = 16
EC_CAPACITY = 2
EXPERT_FF = 1024

ROPE_BASE = 10000.0
Q_BLOCK = 128

kernel_name = 'hybrid_ssd_diffattn_mla_ecmoe_dit'


def rmsnorm(x, w):
    xf = x.astype(F32)
    y = xf * lax.rsqrt(jnp.mean(xf * xf, axis=-1, keepdims=True) + EPS)
    return (y * w.astype(F32)).astype(x.dtype)


def modulate(x, shift, scale):
    return x * (1 + scale) + shift


def rope_2d(x, rows, cols):
    half = x.shape[-1] // 2
    quarter = half // 2
    inv = ROPE_BASE ** (-jnp.arange(quarter, dtype=F32) / quarter)

    def rot(xh, pos):
        ang = pos[:, None] * inv[None, :]
        cos = jnp.cos(ang)[None, :, None, :].astype(x.dtype)
        sin = jnp.sin(ang)[None, :, None, :].astype(x.dtype)
        x1, x2 = xh[..., :quarter], xh[..., quarter:]
        return jnp.concatenate([x1 * cos - x2 * sin, x1 * sin + x2 * cos], axis=-1)

    return jnp.concatenate([rot(x[..., :half], rows), rot(x[..., half:], cols)], axis=-1)


def over_query_blocks(fn, *qs):
    b, s = qs[0].shape[:2]
    nb = s // Q_BLOCK
    blocks = tuple(jnp.moveaxis(q.reshape((b, nb, Q_BLOCK) + q.shape[2:]), 1, 0) for q in qs)
    out = lax.map(lambda blk: fn(*blk), blocks)
    out = jnp.moveaxis(out, 0, 1)
    return out.reshape((b, s) + out.shape[3:])


def softmax_attend(q, k, v, scale):
    s = jnp.einsum('bqhd,bkhd->bhqk', q, k).astype(F32) * scale
    p = jax.nn.softmax(s, axis=-1)
    return jnp.einsum('bhqk,bkhd->bqhd', p.astype(v.dtype), v)


def centred_dwconv(u, w, b):
    k = w.shape[0]
    out = lax.conv_general_dilated(u, w[:, None, :].astype(u.dtype), window_strides=(1,),
                                   padding=[((k - 1) // 2, (k - 1) // 2)],
                                   dimension_numbers=('NWC', 'WIO', 'NWC'),
                                   feature_group_count=u.shape[-1])
    return out + b


def ssd_scan(xs, dt, a, bm, cm, h0, with_output):
    b, L, H, P = xs.shape
    G, N = bm.shape[2], bm.shape[3]
    rep = H // G
    Q = SSD_CHUNK
    nc = L // Q
    bh = jnp.repeat(bm, rep, axis=2).reshape(b, nc, Q, H, N)
    ch = jnp.repeat(cm, rep, axis=2).reshape(b, nc, Q, H, N)
    xdt = (xs * dt[..., None]).reshape(b, nc, Q, H, P)
    acum = jnp.cumsum((dt * a).reshape(b, nc, Q, H), axis=2)
    a_last = acum[:, :, -1]
    states = jnp.einsum('bcqhn,bcqh,bcqhp->bchpn', bh, jnp.exp(a_last[:, :, None] - acum), xdt)

    def step(h, inp):
        s, al = inp
        return jnp.exp(al)[..., None, None] * h + s, h

    h_final, h_prev = lax.scan(step, h0, (jnp.moveaxis(states, 1, 0), jnp.moveaxis(a_last, 1, 0)))
    if not with_output:
        return None, h_final
    h_prev = jnp.moveaxis(h_prev, 0, 1)
    seg = acum[:, :, :, None, :] - acum[:, :, None, :, :]
    lower = jnp.tril(jnp.ones((Q, Q), bool))[None, None, :, :, None]
    decay = jnp.exp(jnp.where(lower, seg, -jnp.inf))
    cb = jnp.einsum('bcihn,bcjhn->bcijh', ch, bh)
    y_diag = jnp.einsum('bcijh,bcjhp->bcihp', cb * decay, xdt)
    y_off = jnp.einsum('bcihn,bchpn->bcihp', ch * jnp.exp(acum)[..., None], h_prev)
    return (y_diag + y_off).reshape(b, L, H, P), h_final


def ssd_mixer(p_ctx, p_lat, conv_w, conv_b, a_log, dt_bias, d_skip, norm_w, ctx_out):
    def prep(p):
        b, L, _ = p.shape
        z, xbc, dt = jnp.split(p, [SSD_INNER, SSD_INNER + SSD_CONV_DIM], axis=-1)
        xbc = jax.nn.silu(centred_dwconv(xbc, conv_w, conv_b))
        xs, bm, cm = jnp.split(xbc, [SSD_INNER, SSD_INNER + SSD_GROUPS * SSD_STATE], axis=-1)
        xs = xs.reshape(b, L, SSD_HEADS, SSD_HEAD_DIM).astype(F32)
        bm = bm.reshape(b, L, SSD_GROUPS, SSD_STATE).astype(F32)
        cm = cm.reshape(b, L, SSD_GROUPS, SSD_STATE).astype(F32)
        dt = dt.reshape(b, L, 2, SSD_HEADS).astype(F32)
        return z, xs, bm, cm, dt

    zc, xc, bc, cc, dtc = prep(p_ctx)
    zl, xl, bl, cl, dtl = prep(p_lat)
    b = p_lat.shape[0]
    dsk = d_skip.astype(F32)[:, None]
    y_lat = dsk * xl
    y_ctx = dsk * xc if ctx_out else None
    for direction in range(2):
        a = -jnp.exp(a_log[direction].astype(F32))
        bias = dt_bias[direction].astype(F32)
        flip = (lambda t: jnp.flip(t, axis=1)) if direction == 1 else (lambda t: t)
        h0 = jnp.zeros((b, SSD_HEADS, SSD_HEAD_DIM, SSD_STATE), F32)
        yc, hc = ssd_scan(flip(xc), jax.nn.softplus(flip(dtc[:, :, direction]) + bias), a,
                          flip(bc), flip(cc), h0, ctx_out)
        yl, _ = ssd_scan(flip(xl), jax.nn.softplus(flip(dtl[:, :, direction]) + bias), a,
                         flip(bl), flip(cl), hc, True)
        y_lat = y_lat + flip(yl)
        if ctx_out:
            y_ctx = y_ctx + flip(yc)

    def gate_norm(y, z):
        bb, L = y.shape[:2]
        g = y.reshape(bb, L, SSD_INNER).astype(z.dtype) * jax.nn.silu(z)
        g = g.reshape(bb, L, SSD_GROUPS, SSD_INNER // SSD_GROUPS)
        return rmsnorm(g, norm_w.reshape(SSD_GROUPS, SSD_INNER // SSD_GROUPS)).reshape(bb, L, SSD_INNER)

    out_l = gate_norm(y_lat, zl)
    out_c = gate_norm(y_ctx, zc) if ctx_out else None
    return out_l, out_c


def diff_attn_mixer(p_ctx, p_lat, lam_q1, lam_k1, lam_q2, lam_k2, subln_w, lambda_init, rows, cols, ctx_out):
    lam = (jnp.exp(jnp.sum(lam_q1.astype(F32) * lam_k1.astype(F32)))
           - jnp.exp(jnp.sum(lam_q2.astype(F32) * lam_k2.astype(F32))) + lambda_init)
    scale = DIFF_QK_DIM ** -0.5

    def split(p):
        b, L, _ = p.shape
        q, k, v = jnp.split(p, 3, axis=-1)
        q = q.reshape(b, L, DIFF_HEADS, 2, DIFF_QK_DIM)
        k = k.reshape(b, L, DIFF_HEADS, 2, DIFF_QK_DIM)
        v = v.reshape(b, L, DIFF_HEADS, DIFF_V_DIM)
        return q[..., 0, :], q[..., 1, :], k[..., 0, :], k[..., 1, :], v

    def diff_attend(q1, q2, k1, k2, v):
        s1 = jnp.einsum('bqhd,bkhd->bhqk', q1, k1).astype(F32) * scale
        s2 = jnp.einsum('bqhd,bkhd->bhqk', q2, k2).astype(F32) * scale
        amap = jax.nn.softmax(s1, axis=-1) - lam * jax.nn.softmax(s2, axis=-1)
        o = jnp.einsum('bhqk,bkhd->bqhd', amap.astype(v.dtype), v)
        return rmsnorm(o, subln_w) * (1.0 - lambda_init)

    q1c, q2c, k1c, k2c, vc = split(p_ctx)
    q1l, q2l, k1l, k2l, vl = split(p_lat)
    rp = lambda t: rope_2d(t, rows, cols)
    k1 = jnp.concatenate([k1c, rp(k1l)], axis=1)
    k2 = jnp.concatenate([k2c, rp(k2l)], axis=1)
    v = jnp.concatenate([vc, vl], axis=1)
    b, L = p_lat.shape[:2]
    out_l = over_query_blocks(lambda a1, a2: diff_attend(a1, a2, k1, k2, v), rp(q1l), rp(q2l))
    out_l = out_l.reshape(b, L, DIFF_INNER)
    out_c = None
    if ctx_out:
        out_c = diff_attend(q1c, q2c, k1c, k2c, vc).reshape(b, p_ctx.shape[1], DIFF_INNER)
    return out_l, out_c


def mla_mixer(p_ctx, p_lat, q_norm_w, w_q_up, kv_norm_w, w_kv_up, rows, cols, ctx_out):
    scale = (MLA_NOPE + MLA_ROPE) ** -0.5

    def project(p):
        b, L, _ = p.shape
        cq, ckv, k_rope = jnp.split(p, [MLA_Q_LORA, MLA_Q_LORA + MLA_KV_LORA], axis=-1)
        q = (rmsnorm(cq, q_norm_w) @ w_q_up).reshape(b, L, MLA_HEADS, MLA_NOPE + MLA_ROPE)
        kv = (rmsnorm(ckv, kv_norm_w) @ w_kv_up).reshape(b, L, MLA_HEADS, MLA_NOPE + MLA_V)
        return q[..., :MLA_NOPE], q[..., MLA_NOPE:], kv[..., :MLA_NOPE], k_rope[:, :, None, :], kv[..., MLA_NOPE:]

    def join_keys(k_nope, k_rope):
        return jnp.concatenate([k_nope, jnp.broadcast_to(k_rope, k_nope.shape[:-1] + (MLA_ROPE,))], axis=-1)

    qn_c, qr_c, kn_c, kr_c, v_c = project(p_ctx)
    qn_l, qr_l, kn_l, kr_l, v_l = project(p_lat)
    k_c = join_keys(kn_c, kr_c)
    k_all = jnp.concatenate([k_c, join_keys(kn_l, rope_2d(kr_l, rows, cols))], axis=1)
    v_all = jnp.concatenate([v_c, v_l], axis=1)
    q_l = jnp.concatenate([qn_l, rope_2d(qr_l, rows, cols)], axis=-1)
    b, L = p_lat.shape[:2]
    out_l = over_query_blocks(lambda qb: softmax_attend(qb, k_all, v_all, scale), q_l).reshape(b, L, MLA_INNER)
    out_c = None
    if ctx_out:
        q_c = jnp.concatenate([qn_c, qr_c], axis=-1)
        out_c = softmax_attend(q_c, k_c, v_c, scale).reshape(b, p_ctx.shape[1], MLA_INNER)
    return out_l, out_c


def mix_heads(h_ctx, h_lat, w_in, w_out, conv_w, conv_b, a_log, dt_bias, d_skip, ssd_norm_w,
              lam_q1, lam_k1, lam_q2, lam_k2, subln_w, q_norm_w, w_q_up, kv_norm_w, w_kv_up,
              rows, cols, lambda_init, ctx_out):
    cuts = [SSD_COLS, SSD_COLS + DIFF_COLS]
    ssd_c, diff_c, mla_c = jnp.split(h_ctx @ w_in, cuts, axis=-1)
    ssd_l, diff_l, mla_l = jnp.split(h_lat @ w_in, cuts, axis=-1)
    s_l, s_c = ssd_mixer(ssd_c, ssd_l, conv_w, conv_b, a_log, dt_bias, d_skip, ssd_norm_w, ctx_out)
    d_l, d_c = diff_attn_mixer(diff_c, diff_l, lam_q1, lam_k1, lam_q2, lam_k2, subln_w, lambda_init, rows, cols, ctx_out)
    a_l, a_c = mla_mixer(mla_c, mla_l, q_norm_w, w_q_up, kv_norm_w, w_kv_up, rows, cols, ctx_out)
    out_l = jnp.concatenate([s_l, d_l, a_l], axis=-1) @ w_out
    out_c = jnp.concatenate([s_c, d_c, a_c], axis=-1) @ w_out if ctx_out else None
    return out_l, out_c


def expert_choice_ffn(h, router_w, router_b, w_gate, w_up, w_down):
    b, n, d = h.shape
    cap = EC_CAPACITY * n // N_EXPERTS
    affinity = jax.nn.softmax((h @ router_w + router_b).astype(F32), axis=-1)
    gates, idx = lax.top_k(jnp.swapaxes(affinity, 1, 2), cap)
    xg = jax.vmap(lambda hb, ib: hb[ib])(h, idx)
    hid = jax.nn.silu(jnp.einsum('becd,edf->becf', xg, w_gate)) * jnp.einsum('becd,edf->becf', xg, w_up)
    y = jnp.einsum('becf,efd->becd', hid, w_down) * gates[..., None].astype(h.dtype)
    scatter = lambda yb, ib: jnp.zeros((n, d), h.dtype).at[ib.reshape(-1)].add(yb.reshape(-1, d))
    return jax.vmap(scatter)(y, idx)


def setup_inputs(seed: int = 0) -> dict:
    key = jax.random.key(seed)
    ks = iter(jax.random.split(key, 48))
    D = D_MODEL

    def nrm(shape, s):
        return jax.random.normal(next(ks), shape, F32) * s

    def gain(shape):
        return 1.0 + nrm(shape, 0.02)

    dt0 = jnp.exp(jax.random.uniform(next(ks), (DEPTH, 2, SSD_HEADS), F32,
                                     minval=math.log(1e-3), maxval=math.log(1e-1)))
    dt_bias = dt0 + jnp.log(-jnp.expm1(-dt0))
    a_log = jnp.log(jax.random.uniform(next(ks), (DEPTH, 2, SSD_HEADS), F32, minval=1.0, maxval=16.0))
    return {
        'x': nrm((BATCH, SEQ, D), 1.0),
        'c': nrm((BATCH, D), 1.0),
        'ctx': nrm((BATCH, CTX_LEN, D), 1.0),
        'c_ctx': nrm((D,), 1.0),
        'ada_w': nrm((DEPTH, D, N_MOD * D), 0.5 * D ** -0.5),
        'ada_b': nrm((DEPTH, N_MOD * D), 0.02),
        'norm_mix_pre': gain((DEPTH, D)),
        'norm_mix_post': gain((DEPTH, D)),
        'norm_ffn_pre': gain((DEPTH, D)),
        'norm_ffn_post': gain((DEPTH, D)),
        'w_in': nrm((DEPTH, D, IN_COLS), D ** -0.5),
        'ssd_conv_w': nrm((DEPTH, SSD_CONV_W, SSD_CONV_DIM), SSD_CONV_W ** -0.5),
        'ssd_conv_b': nrm((DEPTH, SSD_CONV_DIM), 0.02),
        'ssd_a_log': a_log,
        'ssd_dt_bias': dt_bias,
        'ssd_d': gain((DEPTH, SSD_HEADS)),
        'ssd_norm': gain((DEPTH, SSD_INNER)),
        'diff_lam_q1': nrm((DEPTH, DIFF_QK_DIM), 0.1),
        'diff_lam_k1': nrm((DEPTH, DIFF_QK_DIM), 0.1),
        'diff_lam_q2': nrm((DEPTH, DIFF_QK_DIM), 0.1),
        'diff_lam_k2': nrm((DEPTH, DIFF_QK_DIM), 0.1),
        'diff_subln': gain((DEPTH, DIFF_V_DIM)),
        'mla_q_norm': gain((DEPTH, MLA_Q_LORA)),
        'mla_w_q_up': nrm((DEPTH, MLA_Q_LORA, MLA_HEADS * (MLA_NOPE + MLA_ROPE)), MLA_Q_LORA ** -0.5),
        'mla_kv_norm': gain((DEPTH, MLA_KV_LORA)),
        'mla_w_kv_up': nrm((DEPTH, MLA_KV_LORA, MLA_HEADS * (MLA_NOPE + MLA_V)), MLA_KV_LORA ** -0.5),
        'w_out': nrm((DEPTH, MIX_WIDTH, D), MIX_WIDTH ** -0.5),
        'router_w': nrm((DEPTH, D, N_EXPERTS), D ** -0.5),
        'router_b': nrm((DEPTH, N_EXPERTS), 0.01),
        'w_gate': nrm((DEPTH, N_EXPERTS, D, EXPERT_FF), D ** -0.5),
        'w_up': nrm((DEPTH, N_EXPERTS, D, EXPERT_FF), D ** -0.5),
        'w_down': nrm((DEPTH, N_EXPERTS, EXPERT_FF, D), EXPERT_FF ** -0.5),
    }


def reference(x, c, ctx, c_ctx, ada_w, ada_b, norm_mix_pre, norm_mix_post, norm_ffn_pre, norm_ffn_post,
              w_in, ssd_conv_w, ssd_conv_b, ssd_a_log, ssd_dt_bias, ssd_d, ssd_norm,
              diff_lam_q1, diff_lam_k1, diff_lam_q2, diff_lam_k2, diff_subln,
              mla_q_norm, mla_w_q_up, mla_kv_norm, mla_w_kv_up, w_out,
              router_w, router_b, w_gate, w_up, w_down):
    n_rows = x.shape[1] // GRID_W
    rows = jnp.repeat(jnp.arange(n_rows, dtype=F32), GRID_W)
    cols = jnp.tile(jnp.arange(GRID_W, dtype=F32), n_rows)
    silu_c = jax.nn.silu(c)
    silu_cc = jax.nn.silu(c_ctx)
    for l in range(DEPTH):
        ctx_out = l < DEPTH - 1
        lambda_init = 0.8 - 0.6 * math.exp(-0.3 * l)
        mod_l = jnp.split((silu_c @ ada_w[l] + ada_b[l])[:, None, :], N_MOD, axis=-1)
        mod_c = jnp.split(silu_cc @ ada_w[l] + ada_b[l], N_MOD, axis=-1)
        h_l = modulate(rmsnorm(x, norm_mix_pre[l]), mod_l[0], mod_l[1])
        h_c = modulate(rmsnorm(ctx, norm_mix_pre[l]), mod_c[0], mod_c[1])
        m_l, m_c = mix_heads(h_c, h_l, w_in[l], w_out[l], ssd_conv_w[l], ssd_conv_b[l], ssd_a_log[l],
                             ssd_dt_bias[l], ssd_d[l], ssd_norm[l], diff_lam_q1[l], diff_lam_k1[l],
                             diff_lam_q2[l], diff_lam_k2[l], diff_subln[l], mla_q_norm[l], mla_w_q_up[l],
                             mla_kv_norm[l], mla_w_kv_up[l], rows, cols, lambda_init, ctx_out)
        x = x + mod_l[2] * rmsnorm(m_l, norm_mix_post[l])
        if ctx_out:
            ctx = ctx + mod_c[2] * rmsnorm(m_c, norm_mix_post[l])
        f_l = expert_choice_ffn(modulate(rmsnorm(x, norm_ffn_pre[l]), mod_l[3], mod_l[4]),
                                router_w[l], router_b[l], w_gate[l], w_up[l], w_down[l])
        x = x + mod_l[5] * rmsnorm(f_l, norm_ffn_post[l])
        if ctx_out:
            f_c = expert_choice_ffn(modulate(rmsnorm(ctx, norm_ffn_pre[l]), mod_c[3], mod_c[4]),
                                    router_w[l], router_b[l], w_gate[l], w_up[l], w_down[l])
            ctx = ctx + mod_c[5] * rmsnorm(f_c, norm_ffn_post[l])
    return x
```

```python
import functools
import math

import jax
import jax.numpy as jnp
from jax import lax
from jax.experimental import pallas as pl
from jax.experimental.pallas import tpu as pltpu

F32 = jnp.float32
BF16 = jnp.bfloat16
I32 = jnp.int32
HIGHEST = lax.Precision.HIGHEST

EPS = 1e-6
GRID_W = 64
ROPE_BASE = 10000.0
N_MOD = 6

SSD_HEADS = 8
SSD_HEAD_DIM = 64
SSD_INNER = SSD_HEADS * SSD_HEAD_DIM
SSD_GROUPS = 2
SSD_STATE = 64
SSD_CHUNK = 128
SSD_BC = SSD_GROUPS * SSD_STATE
SSD_CONV_DIM = SSD_INNER + 2 * SSD_BC
SSD_PAIRS = SSD_HEADS // 2

DIFF_HEADS = 4
DIFF_QK = 32
DIFF_V = 64
DIFF_MAPS = 2 * DIFF_HEADS

MLA_HEADS = 4
MLA_Q_LORA = 192
MLA_KV_LORA = 128
MLA_NOPE = 64
MLA_ROPE = 32
MLA_V = 64
MLA_QK_PAD = 128

N_EXPERTS = 16
EC_CAPACITY = 2

TILE = 256
LANES = 128
VMEM_LIMIT = 56 * 1024 * 1024

NT_DIMS = (((1,), (1,)), ((), ()))
TN_DIMS = (((0,), (0,)), ((), ()))


def _cparams(sem, vmem=None):
    return pltpu.CompilerParams(dimension_semantics=sem, vmem_limit_bytes=vmem)


def _rms(x, w):
    return x * lax.rsqrt(jnp.mean(x * x, axis=-1, keepdims=True) + EPS) * w


def _silu(x):
    return x * jax.nn.sigmoid(x)


def _dot(a, b):
    return jnp.dot(a, b, preferred_element_type=F32)


def _adaln_kernel(c_ref, w_ref, b_ref, o_ref):
    s = _silu(c_ref[...])
    o_ref[0] = lax.dot_general(s, w_ref[0], (((1,), (0,)), ((), ())), precision=HIGHEST,
                               preferred_element_type=F32) + b_ref[0]


def _adaln(cvec, ada_w, ada_b):
    depth, d, nd = ada_w.shape
    rows = cvec.shape[0]
    return pl.pallas_call(
        _adaln_kernel,
        grid=(depth, nd // d),
        in_specs=[pl.BlockSpec((rows, d), lambda l, j: (0, 0)),
                  pl.BlockSpec((1, d, d), lambda l, j: (l, 0, j)),
                  pl.BlockSpec((1, 1, d), lambda l, j: (l, 0, j))],
        out_specs=pl.BlockSpec((1, rows, d), lambda l, j: (l, 0, j)),
        out_shape=jax.ShapeDtypeStruct((depth, rows, nd), F32),
        compiler_params=_cparams(("arbitrary", "arbitrary")),
        name="adaln",
    )(cvec, ada_w, ada_b.reshape(depth, 1, nd))


def _inproj_kernel(x_ref, mod_ref, nw_ref, wa_ref, wdk_ref, wm_ref, wqt_ref, wvt_ref,
                   ck_ref, sk_ref, ct_ref, st_ref, qnw_ref, kvnw_ref, wqut_ref, wk2_ref, ek_ref, wvt2_ref,
                   z_ref, xbc_ref, dt_ref, dq_ref, dk_ref, dv_ref, mq_ref, mk_ref, mv_ref):
    x = x_ref[0]
    mod = mod_ref[0, 0]
    h = (_rms(x, nw_ref[...]) * (1.0 + mod[1:2]) + mod[0:1]).astype(BF16)

    ra = _dot(h, wa_ref[...])
    z_ref[0] = ra[:, :SSD_INNER].astype(BF16)
    xbc_ref[0] = ra[:, SSD_INNER:SSD_INNER + SSD_CONV_DIM]
    dt_ref[0] = ra[:, SSD_INNER + SSD_CONV_DIM:]

    nk = DIFF_MAPS * DIFF_QK
    rk = _dot(h, wdk_ref[...])
    k = (rk[:, :nk] * ck_ref[...] + rk[:, nk:] * sk_ref[...]).astype(BF16)
    for m in range(DIFF_MAPS):
        dk_ref[0, m // 2, m % 2, 0] = k[:, DIFF_QK * m:DIFF_QK * (m + 1)]

    ct = ct_ref[...]
    st = st_ref[...]
    rq = lax.dot_general(wqt_ref[...], h, NT_DIMS, preferred_element_type=F32)
    for m in range(DIFF_MAPS):
        lo = DIFF_QK * m
        qm = rq[lo:lo + DIFF_QK] * ct + rq[nk + lo:nk + lo + DIFF_QK] * st
        dq_ref[0, m // 2, m % 2] = (qm * DIFF_C_EXP).astype(BF16)

    rv = lax.dot_general(wvt_ref[...], h, NT_DIMS, preferred_element_type=F32).astype(BF16)
    for hd in range(DIFF_HEADS):
        dv_ref[0, hd, 0] = rv[DIFF_V * hd:DIFF_V * (hd + 1)]

    rm = _dot(h, wm_ref[...])
    cq = _rms(rm[:, :MLA_Q_LORA], qnw_ref[...]).astype(BF16)
    ckv = _rms(rm[:, 256:256 + MLA_KV_LORA], kvnw_ref[...]).astype(BF16)
    kr = rm[:, 384:384 + MLA_ROPE] * ck_ref[:, :MLA_ROPE] + rm[:, 416:416 + MLA_ROPE] * sk_ref[:, :MLA_ROPE]

    rq2 = lax.dot_general(wqut_ref[...], cq, NT_DIMS, preferred_element_type=F32)
    nq = MLA_HEADS * MLA_QK_PAD
    ones = jnp.ones((MLA_NOPE, ct.shape[1]), F32)
    pad1 = jnp.ones((MLA_QK_PAD - MLA_NOPE - MLA_ROPE, ct.shape[1]), F32)
    ct_h = jnp.concatenate([ones, ct, pad1], axis=0)
    st_h = jnp.concatenate([0.0 * ones, st, 0.0 * pad1], axis=0)
    for hd in range(MLA_HEADS):
        lo = MLA_QK_PAD * hd
        qh = rq2[lo:lo + MLA_QK_PAD] * ct_h + rq2[nq + lo:nq + lo + MLA_QK_PAD] * st_h
        mq_ref[0, hd] = (qh * MLA_C_EXP).astype(BF16)

    k2 = (_dot(ckv, wk2_ref[...]) + _dot(kr.astype(BF16), ek_ref[...])).astype(BF16)
    for hd in range(MLA_HEADS):
        mk_ref[0, hd, 0] = k2[:, MLA_QK_PAD * hd:MLA_QK_PAD * (hd + 1)]
    rv2 = lax.dot_general(wvt2_ref[...], ckv, NT_DIMS, preferred_element_type=F32).astype(BF16)
    for hd in range(MLA_HEADS):
        mv_ref[0, hd, 0] = rv2[MLA_V * hd:MLA_V * (hd + 1)]


def _inproj(t, mod, lw, tabs):
    b, nt, d = t.shape
    nti = nt // TILE
    full = lambda a: pl.BlockSpec(a.shape, lambda bi, i: (0,) * a.ndim)
    tok = lambda w: pl.BlockSpec((TILE, w), lambda bi, i: (i, 0))
    tokt = lambda w: pl.BlockSpec((w, TILE), lambda bi, i: (0, i))
    ws = [lw["norm_mix_pre"], lw["wa"], lw["wdk"], lw["wm"], lw["wqt"], lw["wvt"]]
    ws2 = [lw["qnw"], lw["kvnw"], lw["wqut"], lw["wk2"], lw["ek"], lw["wvt2"]]
    out_shape = [
        jax.ShapeDtypeStruct((b, nt, SSD_INNER), BF16),
        jax.ShapeDtypeStruct((b, nt, SSD_CONV_DIM), F32),
        jax.ShapeDtypeStruct((b, nt, LANES), F32),
        jax.ShapeDtypeStruct((b, DIFF_HEADS, 2, DIFF_QK, nt), BF16),
        jax.ShapeDtypeStruct((b, DIFF_HEADS, 2, nti, TILE, DIFF_QK), BF16),
        jax.ShapeDtypeStruct((b, DIFF_HEADS, nti, DIFF_V, TILE), BF16),
        jax.ShapeDtypeStruct((b, MLA_HEADS, MLA_QK_PAD, nt), BF16),
        jax.ShapeDtypeStruct((b, MLA_HEADS, nti, TILE, MLA_QK_PAD), BF16),
        jax.ShapeDtypeStruct((b, MLA_HEADS, nti, MLA_V, TILE), BF16),
    ]
    out_specs = [
        pl.BlockSpec((1, TILE, SSD_INNER), lambda bi, i: (bi, i, 0)),
        pl.BlockSpec((1, TILE, SSD_CONV_DIM), lambda bi, i: (bi, i, 0)),
        pl.BlockSpec((1, TILE, LANES), lambda bi, i: (bi, i, 0)),
        pl.BlockSpec((1, DIFF_HEADS, 2, DIFF_QK, TILE), lambda bi, i: (bi, 0, 0, 0, i)),
        pl.BlockSpec((1, DIFF_HEADS, 2, 1, TILE, DIFF_QK), lambda bi, i: (bi, 0, 0, i, 0, 0)),
        pl.BlockSpec((1, DIFF_HEADS, 1, DIFF_V, TILE), lambda bi, i: (bi, 0, i, 0, 0)),
        pl.BlockSpec((1, MLA_HEADS, MLA_QK_PAD, TILE), lambda bi, i: (bi, 0, 0, i)),
        pl.BlockSpec((1, MLA_HEADS, 1, TILE, MLA_QK_PAD), lambda bi, i: (bi, 0, i, 0, 0)),
        pl.BlockSpec((1, MLA_HEADS, 1, MLA_V, TILE), lambda bi, i: (bi, 0, i, 0, 0)),
    ]
    in_specs = ([pl.BlockSpec((1, TILE, d), lambda bi, i: (bi, i, 0)),
                 pl.BlockSpec((1, 1, N_MOD, d), lambda bi, i: (bi, jnp.minimum(i, 1), 0, 0))]
                + [full(a) for a in ws]
                + [tok(DIFF_MAPS * DIFF_QK), tok(DIFF_MAPS * DIFF_QK), tokt(DIFF_QK), tokt(DIFF_QK)]
                + [full(a) for a in ws2])
    return pl.pallas_call(
        _inproj_kernel,
        grid=(b, nti),
        in_specs=in_specs,
        out_specs=out_specs,
        out_shape=out_shape,
        compiler_params=_cparams(("arbitrary", "arbitrary"), VMEM_LIMIT),
        name="inproj",
    )(t, mod, *ws, tabs["ck"], tabs["sk"], tabs["ct"], tabs["st"], *ws2)


def _ssd_chunk_of(ph, i, nck, nctx):
    back = jnp.where(i < nctx, nctx - 1 - i, nck - 1 + nctx - i)
    return jnp.where(ph == 0, back, i)


def _ssd_kernel(xc_ref, xp_ref, xn_ref, dt_ref, z_ref, cw_ref, cb_ref, alog_ref, dtb_ref, dsk_ref, nw_ref,
                o_ref, s_ref, sb_ref, *, nck, nctx):
    ph = pl.program_id(1)
    i = pl.program_id(2)
    c = _ssd_chunk_of(ph, i, nck, nctx)
    q = SSD_CHUNK

    @pl.when(i == 0)
    def _():
        s_ref[...] = jnp.zeros_like(s_ref)

    x = xc_ref[0]
    has_prev = jnp.logical_and(c != 0, c != nctx)
    has_next = jnp.logical_and(c != nctx - 1, c != nck - 1)
    prev_row = jnp.where(has_prev, xp_ref[0][7:8, :], 0.0)
    next_row = jnp.where(has_next, xn_ref[0][0:1, :], 0.0)
    row = lax.broadcasted_iota(I32, x.shape, 0)
    xm1 = jnp.where(row == 0, prev_row, pltpu.roll(x, 1, 0))
    xp1 = jnp.where(row == q - 1, next_row, pltpu.roll(x, q - 1, 0))
    cw = cw_ref[...]
    u = _silu(xm1 * cw[0:1] + x * cw[1:2] + xp1 * cw[2:3] + cb_ref[...])
    xs = u[:, :SSD_INNER]
    bm = u[:, SSD_INNER:SSD_INNER + SSD_BC]
    cm = u[:, SSD_INNER + SSD_BC:]

    ri = lax.broadcasted_iota(I32, (q, q), 0)
    ci = lax.broadcasted_iota(I32, (q, q), 1)
    lower = ci <= ri
    upper = ci >= ri
    xdt = dt_ref[0] + dtb_ref[...]
    dtc = jnp.maximum(xdt, 0.0) + jnp.log1p(jnp.exp(-jnp.abs(xdt)))
    dac = dtc * (-jnp.exp(alog_ref[...]))
    tri_dims = (((1,), (0,)), ((), ()))
    acf = lax.dot_general(lower.astype(F32), dac, tri_dims, precision=HIGHEST, preferred_element_type=F32)
    acb = lax.dot_general(upper.astype(F32), dac, tri_dims, precision=HIGHEST, preferred_element_type=F32)
    acc = jnp.where(ci < SSD_HEADS, acf, acb)
    act = acc.T
    dtt = dtc.T
    bt = bm.T
    sub = lax.broadcasted_iota(I32, (q, q), 0)
    lane = ci
    first_half_s = sub < SSD_STATE
    first_half_l = lane < SSD_HEAD_DIM
    blockdiag = first_half_s == first_half_l

    def pair_vals(arr_c, arr_t, h0):
        col = jnp.where(first_half_l, arr_c[:, h0:h0 + 1], arr_c[:, h0 + 1:h0 + 2])
        rowv = jnp.where(first_half_s, arr_t[h0:h0 + 1, :], arr_t[h0 + 1:h0 + 2, :])
        return col, rowv

    def state_update(p, d):
        g = (2 * p) // (SSD_HEADS // SSD_GROUPS)
        h0 = d * SSD_HEADS + 2 * p
        edge = q - 1 if d == 0 else 0
        alast_row = jnp.where(first_half_s[:, 0:1], acc[edge:edge + 1, h0:h0 + 1], acc[edge:edge + 1, h0 + 1:h0 + 2])
        _, ar = pair_vals(acc, act, h0)
        _, dr = pair_vals(dtc, dtt, h0)
        w = jnp.exp(alast_row - ar) * dr
        btg = bt[SSD_STATE * g:SSD_STATE * (g + 1)]
        lhs = (jnp.concatenate([btg, btg], axis=0) * w).astype(BF16)
        xs2 = xs[:, 2 * SSD_HEAD_DIM * p:2 * SSD_HEAD_DIM * (p + 1)].astype(BF16)
        upd = jnp.where(blockdiag, _dot(lhs, xs2), 0.0)
        return jnp.exp(alast_row) * s_ref[d, p] + upd

    @pl.when(ph == 0)
    def _():
        for p in range(SSD_PAIRS):
            sb_ref[c, p] = s_ref[1, p].astype(BF16)
            s_ref[1, p] = state_update(p, 1)

    @pl.when(ph == 1)
    def _():
        roll_c = pltpu.roll(cm, SSD_STATE, 1)
        dsk = dsk_ref[...]
        ys = []
        for p in range(SSD_PAIRS):
            g = (2 * p) // (SSD_HEADS // SSD_GROUPS)
            cg_only = jnp.where((lane < SSD_STATE) == (g == 0), cm, 0.0).astype(BF16)
            cb = lax.dot_general(cg_only, bm.astype(BF16), NT_DIMS, preferred_element_type=F32)
            ms = []
            for hh in range(2):
                hf = 2 * p + hh
                hb = SSD_HEADS + hf
                lf = jnp.exp(jnp.where(lower, acc[:, hf:hf + 1] - act[hf:hf + 1, :], -jnp.inf)) * dtt[hf:hf + 1, :]
                lb = jnp.exp(jnp.where(upper, acc[:, hb:hb + 1] - act[hb:hb + 1, :], -jnp.inf)) * dtt[hb:hb + 1, :]
                ms.append((cb * (lf + lb) + jnp.where(ri == ci, dsk[:, hf:hf + 1], 0.0)).astype(BF16))
            xs2 = xs[:, 2 * SSD_HEAD_DIM * p:2 * SSD_HEAD_DIM * (p + 1)]
            rhs = jnp.concatenate([jnp.where(first_half_l, xs2, 0.0), jnp.where(first_half_l, 0.0, xs2)],
                                  axis=0).astype(BF16)
            y = _dot(jnp.concatenate(ms, axis=1), rhs)
            cdup = jnp.where(first_half_l == (g == 0), cm, roll_c)
            ef, _ = pair_vals(acc, act, 2 * p)
            eb, _ = pair_vals(acc, act, SSD_HEADS + 2 * p)
            lhs_off = jnp.concatenate([cdup * jnp.exp(ef), cdup * jnp.exp(eb)], axis=1).astype(BF16)
            rhs_off = jnp.concatenate([s_ref[0, p].astype(BF16), sb_ref[c, p]], axis=0)
            ys.append(y + _dot(lhs_off, rhs_off))
            s_ref[0, p] = state_update(p, 0)
        y = jnp.concatenate(ys, axis=1)
        zf = z_ref[0].astype(F32)
        gt = y * _silu(zf)
        nw = nw_ref[...]
        gw = SSD_INNER // SSD_GROUPS
        outs = [_rms(gt[:, gw * g:gw * (g + 1)], nw[:, gw * g:gw * (g + 1)]) for g in range(SSD_GROUPS)]
        o_ref[0] = jnp.concatenate(outs, axis=1).astype(BF16)


def _ssd(z, xbc, dt, lw):
    b, nt, _ = z.shape
    q = SSD_CHUNK
    nck = nt // q
    nctx = TILE // q
    rows8 = q // 8
    chunk = functools.partial(_ssd_chunk_of, nck=nck, nctx=nctx)
    full = lambda a: pl.BlockSpec(a.shape, lambda bi, ph, i: (0,) * a.ndim)
    ws = [lw["conv_w"], lw["conv_b"], lw["alog"], lw["dtb"], lw["dsk"], lw["ssd_norm"]]
    kern = functools.partial(_ssd_kernel, nck=nck, nctx=nctx)
    return pl.pallas_call(
        kern,
        grid=(b, 2, nck),
        in_specs=[pl.BlockSpec((1, q, SSD_CONV_DIM), lambda bi, ph, i: (bi, chunk(ph, i), 0)),
                  pl.BlockSpec((1, 8, SSD_CONV_DIM),
                               lambda bi, ph, i: (bi, jnp.maximum(chunk(ph, i) * rows8 - 1, 0), 0)),
                  pl.BlockSpec((1, 8, SSD_CONV_DIM),
                               lambda bi, ph, i: (bi, jnp.minimum((chunk(ph, i) + 1) * rows8, nck * rows8 - 1), 0)),
                  pl.BlockSpec((1, q, LANES), lambda bi, ph, i: (bi, chunk(ph, i), 0)),
                  pl.BlockSpec((1, q, SSD_INNER), lambda bi, ph, i: (bi, chunk(ph, i), 0))]
                 + [full(a) for a in ws],
        out_specs=pl.BlockSpec((1, q, SSD_INNER), lambda bi, ph, i: (bi, jnp.where(ph == 0, 0, i), 0)),
        out_shape=jax.ShapeDtypeStruct((b, nt, SSD_INNER), BF16),
        scratch_shapes=[pltpu.VMEM((2, SSD_PAIRS, 2 * SSD_STATE, 2 * SSD_HEAD_DIM), F32),
                        pltpu.VMEM((nck, SSD_PAIRS, 2 * SSD_STATE, 2 * SSD_HEAD_DIM), BF16)],
        compiler_params=_cparams(("arbitrary", "arbitrary", "arbitrary"), VMEM_LIMIT),
        name="ssd",
    )(xbc, xbc, xbc, dt, z, *ws)


ACC_ROWS = 80
DIFF_C_EXP = (DIFF_QK ** -0.5) * math.log2(math.e)
MLA_C_EXP = ((MLA_NOPE + MLA_ROPE) ** -0.5) * math.log2(math.e)


def _ones_rows(tk):
    return (lax.broadcasted_iota(I32, (ACC_ROWS - DIFF_V, tk), 0) == 0).astype(BF16)


def _online_steps(kqs, vaugs, m_ref, acc_ref):
    ss = [_dot(k, q) for k, q in kqs]
    ps, alphas = [], []
    for idx, s in enumerate(ss):
        m = m_ref[idx]
        mn = jnp.maximum(m, jnp.max(s, axis=0, keepdims=True))
        ps.append(jnp.exp2(s - mn).astype(BF16))
        alphas.append(jnp.exp2(m - mn))
        m_ref[idx] = mn
    for idx, (p, alpha) in enumerate(zip(ps, alphas)):
        acc_ref[idx] = acc_ref[idx] * alpha + _dot(vaugs[idx], p)


def _attn_init(m_ref, acc_ref):
    m_ref[...] = jnp.full(m_ref.shape, -jnp.inf, F32)
    acc_ref[...] = jnp.zeros_like(acc_ref)


def _diff_attn_kernel(lq1_ref, lk1_ref, lq2_ref, lk2_ref, subw_ref, q_ref, k_ref, v_ref, o_ref, m_ref, acc_ref, *,
                      nti, lambda_init):
    i = pl.program_id(1)
    nck = jnp.where(i == 0, 1, nti)
    _attn_init(m_ref, acc_ref)
    ones = _ones_rows(TILE)

    def body(c, carry):
        kqs, vaugs = [], []
        for h in range(DIFF_HEADS):
            vaug = jnp.concatenate([v_ref[0, h, c], ones], axis=0)
            for j in range(2):
                kqs.append((k_ref[0, h, j, c], q_ref[0, h, j]))
                vaugs.append(vaug)
        _online_steps(kqs, vaugs, m_ref, acc_ref)
        return carry

    lax.fori_loop(0, nck, body, 0)
    lam = (jnp.exp(jnp.sum(lq1_ref[...] * lk1_ref[...], keepdims=True))
           - jnp.exp(jnp.sum(lq2_ref[...] * lk2_ref[...], keepdims=True)) + lambda_init)
    for h in range(DIFF_HEADS):
        a1 = acc_ref[2 * h]
        a2 = acc_ref[2 * h + 1]
        o = a1[:DIFF_V] / a1[DIFF_V:DIFF_V + 1] - lam * (a2[:DIFF_V] / a2[DIFF_V:DIFF_V + 1])
        o = o * lax.rsqrt(jnp.mean(o * o, axis=0, keepdims=True) + EPS) * subw_ref[...]
        o_ref[0, h] = (o * (1.0 - lambda_init)).astype(BF16)


def _diff_attn(dq, dk, dv, lw, lambda_init):
    b, nh, _, dqk, nt = dq.shape
    nti = nt // TILE
    kern = functools.partial(_diff_attn_kernel, nti=nti, lambda_init=lambda_init)
    vec = pl.BlockSpec((1, DIFF_QK), lambda bi, i: (0, 0))
    return pl.pallas_call(
        kern,
        grid=(b, nti),
        in_specs=[vec, vec, vec, vec,
                  pl.BlockSpec((DIFF_V, 1), lambda bi, i: (0, 0)),
                  pl.BlockSpec((1, nh, 2, dqk, TILE), lambda bi, i: (bi, 0, 0, 0, i)),
                  pl.BlockSpec((1, nh, 2, nti, TILE, dqk), lambda bi, i: (bi, 0, 0, 0, 0, 0)),
                  pl.BlockSpec((1, nh, nti, DIFF_V, TILE), lambda bi, i: (bi, 0, 0, 0, 0))],
        out_specs=pl.BlockSpec((1, nh, DIFF_V, TILE), lambda bi, i: (bi, 0, 0, i)),
        out_shape=jax.ShapeDtypeStruct((b, nh, DIFF_V, nt), BF16),
        scratch_shapes=[pltpu.VMEM((2 * nh, 1, TILE), F32), pltpu.VMEM((2 * nh, ACC_ROWS, TILE), F32)],
        compiler_params=_cparams(("arbitrary", "arbitrary"), VMEM_LIMIT),
        name="diff_attn",
    )(lw["lq1"], lw["lk1"], lw["lq2"], lw["lk2"], lw["subw"], dq, dk, dv)


def _mla_attn_kernel(q_ref, k_ref, v_ref, o_ref, m_ref, acc_ref, *, nti):
    i = pl.program_id(1)
    nck = jnp.where(i == 0, 1, nti)
    _attn_init(m_ref, acc_ref)
    ones = _ones_rows(TILE)

    def body(c, carry):
        kqs = [(k_ref[0, h, c], q_ref[0, h]) for h in range(MLA_HEADS)]
        vaugs = [jnp.concatenate([v_ref[0, h, c], ones], axis=0) for h in range(MLA_HEADS)]
        _online_steps(kqs, vaugs, m_ref, acc_ref)
        return carry

    lax.fori_loop(0, nck, body, 0)
    for h in range(MLA_HEADS):
        a = acc_ref[h]
        o_ref[0, h] = (a[:MLA_V] / a[MLA_V:MLA_V + 1]).astype(BF16)


def _mla_attn(mq, mk, mv):
    b, nh, dpad, nt = mq.shape
    nti = nt // TILE
    kern = functools.partial(_mla_attn_kernel, nti=nti)
    return pl.pallas_call(
        kern,
        grid=(b, nti),
        in_specs=[pl.BlockSpec((1, nh, dpad, TILE), lambda bi, i: (bi, 0, 0, i)),
                  pl.BlockSpec((1, nh, nti, TILE, dpad), lambda bi, i: (bi, 0, 0, 0, 0)),
                  pl.BlockSpec((1, nh, nti, MLA_V, TILE), lambda bi, i: (bi, 0, 0, 0, 0))],
        out_specs=pl.BlockSpec((1, nh, MLA_V, TILE), lambda bi, i: (bi, 0, 0, i)),
        out_shape=jax.ShapeDtypeStruct((b, nh, MLA_V, nt), BF16),
        scratch_shapes=[pltpu.VMEM((nh, 1, TILE), F32), pltpu.VMEM((nh, ACC_ROWS, TILE), F32)],
        compiler_params=_cparams(("arbitrary", "arbitrary"), VMEM_LIMIT),
        name="mla_attn",
    )(mq, mk, mv)


def _outproj_kernel(t_ref, s_ref, d_ref, a_ref, mod_ref, npost_ref, nffn_ref, ws_ref, wd_ref, wa_ref,
                    rwt_ref, rb_ref, tn_ref, hf_ref, aff_ref):
    mod = mod_ref[0, 0]
    m = (_dot(s_ref[0], ws_ref[...])
         + lax.dot_general(d_ref[0], wd_ref[...], TN_DIMS, preferred_element_type=F32)
         + lax.dot_general(a_ref[0], wa_ref[...], TN_DIMS, preferred_element_type=F32))
    tn = t_ref[0] + mod[2:3] * _rms(m, npost_ref[...])
    tn_ref[0] = tn
    hf = _rms(tn, nffn_ref[...]) * (1.0 + mod[4:5]) + mod[3:4]
    hf_ref[0] = hf.astype(BF16)
    logits = lax.dot_general(rwt_ref[...], hf, NT_DIMS, precision=HIGHEST, preferred_element_type=F32) + rb_ref[...]
    e = jnp.exp(logits - jnp.max(logits, axis=0, keepdims=True))
    aff_ref[0] = e / jnp.sum(e, axis=0, keepdims=True)


def _outproj(t, s, dt_, at_, mod, lw):
    b, nt, d = t.shape
    nti = nt // TILE
    full = lambda a: pl.BlockSpec(a.shape, lambda bi, i: (0,) * a.ndim)
    ws = [lw["norm_mix_post"], lw["norm_ffn_pre"], lw["wo_s"], lw["wo_d"], lw["wo_a"], lw["rwt"], lw["rb"]]
    return pl.pallas_call(
        _outproj_kernel,
        grid=(b, nti),
        in_specs=[pl.BlockSpec((1, TILE, d), lambda bi, i: (bi, i, 0)),
                  pl.BlockSpec((1, TILE, SSD_INNER), lambda bi, i: (bi, i, 0)),
                  pl.BlockSpec((1, DIFF_HEADS * DIFF_V, TILE), lambda bi, i: (bi, 0, i)),
                  pl.BlockSpec((1, MLA_HEADS * MLA_V, TILE), lambda bi, i: (bi, 0, i)),
                  pl.BlockSpec((1, 1, N_MOD, d), lambda bi, i: (bi, jnp.minimum(i, 1), 0, 0))]
                 + [full(a) for a in ws],
        out_specs=[pl.BlockSpec((1, TILE, d), lambda bi, i: (bi, i, 0)),
                   pl.BlockSpec((1, TILE, d), lambda bi, i: (bi, i, 0)),
                   pl.BlockSpec((1, N_EXPERTS, TILE), lambda bi, i: (bi, 0, i))],
        out_shape=[jax.ShapeDtypeStruct((b, nt, d), F32),
                   jax.ShapeDtypeStruct((b, nt, d), BF16),
                   jax.ShapeDtypeStruct((b, N_EXPERTS, nt), F32)],
        compiler_params=_cparams(("arbitrary", "arbitrary"), VMEM_LIMIT),
        name="outproj",
    )(t, s, dt_, at_, mod, *ws)


def _route_kernel(aff_ref, pos_ref, gate_ref, cum_ref, *, nti, caps):
    ne = N_EXPERTS
    tri = (lax.broadcasted_iota(I32, (TILE, TILE), 0) < lax.broadcasted_iota(I32, (TILE, TILE), 1)).astype(BF16)
    lane = lax.broadcasted_iota(I32, (ne, LANES), 1)

    def excl_prefix(mask_f):
        return _dot(mask_f.astype(BF16), tri)

    cum_vec = jnp.zeros((ne, LANES), F32)
    total = jnp.zeros((ne, 1), F32)
    seg_bounds = ((0, 1, caps[0]), (1, nti, caps[1]))
    for t0, t1, cap in seg_bounds:
        xi = aff_ref[0, :, t0 * TILE:t1 * TILE]

        def bit_step(j, thr_bits, xi=xi, cap=cap):
            cand = thr_bits | (1 << (29 - j))
            cnt = jnp.sum((xi >= pltpu.bitcast(cand, F32)).astype(F32), axis=1, keepdims=True)
            return jnp.where(cnt >= cap, cand, thr_bits)

        thr = pltpu.bitcast(lax.fori_loop(0, 30, bit_step, jnp.zeros((ne, 1), I32)), F32)
        need = cap - jnp.sum((xi > thr).astype(F32), axis=1, keepdims=True)
        eq_seen = jnp.zeros((ne, 1), F32)
        for t in range(t0, t1):
            lo = (t - t0) * TILE
            xt = xi[:, lo:lo + TILE]
            eq = (xt == thr).astype(F32)
            eq_rank = eq_seen + excl_prefix(eq)
            sel = jnp.where(xt > thr, 1.0, eq * (eq_rank < need).astype(F32))
            eq_seen = eq_seen + jnp.sum(eq, axis=1, keepdims=True)
            rank = total + excl_prefix(sel)
            pos_ref[0, :, t * TILE:(t + 1) * TILE] = jnp.where(sel > 0.0, rank, -1.0).astype(I32)
            gate_ref[0, :, t * TILE:(t + 1) * TILE] = sel * aff_ref[0, :, t * TILE:(t + 1) * TILE]
            cum_vec = jnp.where(lane == t, total, cum_vec)
            total = total + jnp.sum(sel, axis=1, keepdims=True)
    cum_vec = jnp.where(lane == nti, total, cum_vec)
    cum_ref[0] = cum_vec.astype(I32)


def _route(aff, caps):
    b, ne, nt = aff.shape
    nti = nt // TILE
    kern = functools.partial(_route_kernel, nti=nti, caps=caps)
    return pl.pallas_call(
        kern,
        grid=(b,),
        in_specs=[pl.BlockSpec((1, ne, nt), lambda bi: (bi, 0, 0))],
        out_specs=[pl.BlockSpec((1, ne, nt), lambda bi: (bi, 0, 0)),
                   pl.BlockSpec((1, ne, nt), lambda bi: (bi, 0, 0)),
                   pl.BlockSpec((1, ne, LANES), lambda bi: (bi, 0, 0))],
        out_shape=[jax.ShapeDtypeStruct((b, ne, nt), I32),
                   jax.ShapeDtypeStruct((b, ne, nt), F32),
                   jax.ShapeDtypeStruct((b, ne, LANES), I32)],
        compiler_params=_cparams(("arbitrary",)),
        name="route",
    )(aff)


WIN = 64
GROUP = 4


def _tile_windows(cum_ref, b, t, rows):
    los = []
    rounds = jnp.int32(1)
    for e in range(N_EXPERTS):
        base = (b * N_EXPERTS + e) * LANES
        lo = (cum_ref[base + t] // 16) * 16
        los.append(lo)
        rounds = jnp.maximum(rounds, (cum_ref[base + t + 1] - lo + WIN - 1) // WIN)
    return los, rounds


def _window_onehot(pos_row, lo, r, rows):
    want = lo + WIN * r
    w0 = pl.multiple_of(jnp.minimum(want, rows - WIN), 16)
    rowid = w0 + lax.broadcasted_iota(I32, (WIN, TILE), 0)
    return w0, jnp.logical_and(pos_row == rowid, rowid >= want).astype(F32)


def _gather_kernel(cum_ref, hf_ref, pos_ref, gate_ref, xg_ref, gc_ref, *, rows):
    b = pl.program_id(0)
    t = pl.program_id(1)

    @pl.when(t == 0)
    def _():
        xg_ref[...] = jnp.zeros_like(xg_ref)
        gc_ref[...] = jnp.zeros_like(gc_ref)

    los, rounds = _tile_windows(cum_ref, b, t, rows)

    def round_step(r, carry):
        w0s, hots = [], []
        for e in range(N_EXPERTS):
            w0, hot = _window_onehot(pos_ref[0, e:e + 1, :], los[e], r, rows)
            w0s.append(w0)
            hots.append(hot)
            gc_ref[0, e, pl.ds(w0, WIN), :] += jnp.sum(hot * gate_ref[0, e:e + 1, :], axis=1, keepdims=True)
        res = _dot(jnp.concatenate(hots, axis=0).astype(BF16), hf_ref[0])
        for e in range(N_EXPERTS):
            xg_ref[0, e, pl.ds(w0s[e], WIN), :] += res[WIN * e:WIN * (e + 1)].astype(BF16)
        return carry

    lax.fori_loop(0, rounds, round_step, 0)


def _gather(cum_flat, hf, pos, gate, rows):
    b, nt, d = hf.shape
    nti = nt // TILE
    ne = pos.shape[1]
    kern = functools.partial(_gather_kernel, rows=rows)
    grid_spec = pltpu.PrefetchScalarGridSpec(
        num_scalar_prefetch=1,
        grid=(b, nti),
        in_specs=[pl.BlockSpec((1, TILE, d), lambda bi, i, cum: (bi, i, 0)),
                  pl.BlockSpec((1, ne, TILE), lambda bi, i, cum: (bi, 0, i)),
                  pl.BlockSpec((1, ne, TILE), lambda bi, i, cum: (bi, 0, i))],
        out_specs=[pl.BlockSpec((1, ne, rows, d), lambda bi, i, cum: (bi, 0, 0, 0)),
                   pl.BlockSpec((1, ne, rows, 1), lambda bi, i, cum: (bi, 0, 0, 0))],
    )
    return pl.pallas_call(
        kern,
        grid_spec=grid_spec,
        out_shape=[jax.ShapeDtypeStruct((b, ne, rows, d), BF16), jax.ShapeDtypeStruct((b, ne, rows, 1), F32)],
        compiler_params=_cparams(("arbitrary", "arbitrary"), VMEM_LIMIT),
        name="gather",
    )(cum_flat, hf, pos, gate)


def _experts_kernel(xg_ref, gc_ref, wg_ref, wu_ref, wd_ref, y_ref, wgb_ref, wub_ref, wdb_ref):
    @pl.when(pl.program_id(1) == 0)
    def _():
        wgb_ref[...] = wg_ref[0].astype(BF16)
        wub_ref[...] = wu_ref[0].astype(BF16)
        wdb_ref[...] = wd_ref[0].astype(BF16)

    xg = xg_ref[0, 0]
    hid = (_silu(_dot(xg, wgb_ref[...])) * _dot(xg, wub_ref[...])).astype(BF16)
    y_ref[0, 0] = (_dot(hid, wdb_ref[...]) * gc_ref[0, 0]).astype(BF16)


def _experts(xg, gc, lw):
    b, ne, rows, d = xg.shape
    ff = lw["w_gate"].shape[2]
    return pl.pallas_call(
        _experts_kernel,
        grid=(ne, b),
        in_specs=[pl.BlockSpec((1, 1, rows, d), lambda e, bi: (bi, e, 0, 0)),
                  pl.BlockSpec((1, 1, rows, 1), lambda e, bi: (bi, e, 0, 0)),
                  pl.BlockSpec((1, d, ff), lambda e, bi: (e, 0, 0)),
                  pl.BlockSpec((1, d, ff), lambda e, bi: (e, 0, 0)),
                  pl.BlockSpec((1, ff, d), lambda e, bi: (e, 0, 0))],
        out_specs=pl.BlockSpec((1, 1, rows, d), lambda e, bi: (bi, e, 0, 0)),
        out_shape=jax.ShapeDtypeStruct((b, ne, rows, d), BF16),
        scratch_shapes=[pltpu.VMEM((d, ff), BF16), pltpu.VMEM((d, ff), BF16), pltpu.VMEM((ff, d), BF16)],
        compiler_params=_cparams(("arbitrary", "arbitrary"), VMEM_LIMIT),
        name="experts",
    )(xg, gc, lw["w_gate"], lw["w_up"], lw["w_down"])


def _combine_kernel(cum_ref, t_ref, y_ref, pos_ref, mod_ref, npost_ref, o_ref, f_ref, *, rows):
    b = pl.program_id(0)
    t = pl.program_id(1)
    los, rounds = _tile_windows(cum_ref, b, t, rows)
    f_ref[...] = jnp.zeros_like(f_ref)

    def round_step(r, carry):
        for g in range(N_EXPERTS // GROUP):
            hots, wins = [], []
            for e in range(GROUP * g, GROUP * (g + 1)):
                w0, hot = _window_onehot(pos_ref[0, e:e + 1, :], los[e], r, rows)
                hots.append(hot)
                wins.append(y_ref[0, e, pl.ds(w0, WIN), :])
            hot = jnp.concatenate(hots, axis=0).astype(BF16)
            f_ref[...] += lax.dot_general(hot, jnp.concatenate(wins, axis=0), TN_DIMS, preferred_element_type=F32)
        return carry

    lax.fori_loop(0, rounds, round_step, 0)
    mod = mod_ref[0, 0]
    o_ref[0] = t_ref[0] + mod[5:6] * _rms(f_ref[...], npost_ref[...])


def _combine(cum_flat, t, y, pos, mod, lw):
    b, nt, d = t.shape
    nti = nt // TILE
    ne, rows = y.shape[1], y.shape[2]
    kern = functools.partial(_combine_kernel, rows=rows)
    grid_spec = pltpu.PrefetchScalarGridSpec(
        num_scalar_prefetch=1,
        grid=(b, nti),
        in_specs=[pl.BlockSpec((1, TILE, d), lambda bi, i, cum: (bi, i, 0)),
                  pl.BlockSpec((1, ne, rows, d), lambda bi, i, cum: (bi, 0, 0, 0)),
                  pl.BlockSpec((1, ne, TILE), lambda bi, i, cum: (bi, 0, i)),
                  pl.BlockSpec((1, 1, N_MOD, d), lambda bi, i, cum: (bi, jnp.minimum(i, 1), 0, 0)),
                  pl.BlockSpec((1, d), lambda bi, i, cum: (0, 0))],
        out_specs=pl.BlockSpec((1, TILE, d), lambda bi, i, cum: (bi, i, 0)),
        scratch_shapes=[pltpu.VMEM((TILE, d), F32)],
    )
    return pl.pallas_call(
        kern,
        grid_spec=grid_spec,
        out_shape=jax.ShapeDtypeStruct((b, nt, d), F32),
        compiler_params=_cparams(("arbitrary", "arbitrary"), VMEM_LIMIT),
        name="combine",
    )(cum_flat, t, y, pos, mod, lw["norm_ffn_post"])


def _rope_tables(seq, ctx):
    quarter = MLA_ROPE // 4
    inv = ROPE_BASE ** (-jnp.arange(quarter, dtype=F32) / quarter)
    n_rows = seq // GRID_W
    rows = jnp.repeat(jnp.arange(n_rows, dtype=F32), GRID_W)
    cols = jnp.tile(jnp.arange(GRID_W, dtype=F32), n_rows)
    ar = rows[:, None] * inv[None, :]
    ac = cols[:, None] * inv[None, :]
    cos = jnp.concatenate([jnp.cos(ar), jnp.cos(ar), jnp.cos(ac), jnp.cos(ac)], axis=1)
    sin = jnp.concatenate([-jnp.sin(ar), jnp.sin(ar), -jnp.sin(ac), jnp.sin(ac)], axis=1)
    cos = jnp.concatenate([jnp.ones((ctx, MLA_ROPE), F32), cos], axis=0)
    sin = jnp.concatenate([jnp.zeros((ctx, MLA_ROPE), F32), sin], axis=0)
    return {"ck": jnp.tile(cos, (1, DIFF_MAPS)), "sk": jnp.tile(sin, (1, DIFF_MAPS)), "ct": cos.T, "st": sin.T}


def _partner_perm(width):
    idx = jnp.arange(width)
    r = idx % 16
    return jnp.where(r < 8, idx + 8, idx - 8)


def _layer_weights(l, p):
    d = p["w_in"].shape[1]
    w_in = p["w_in"][l]
    o_diff = 2 * SSD_INNER + 2 * SSD_BC + 2 * SSD_HEADS
    o_mla = o_diff + 3 * DIFF_HEADS * DIFF_V
    nk = DIFF_MAPS * DIFF_QK
    w_ssd = w_in[:, :o_diff]
    wa = jnp.concatenate([w_ssd, jnp.zeros((d, LANES - 2 * SSD_HEADS), F32)], axis=1)
    wq = w_in[:, o_diff:o_diff + nk]
    wk = w_in[:, o_diff + nk:o_diff + 2 * nk]
    wv = w_in[:, o_diff + 2 * nk:o_mla]
    perm = _partner_perm(nk)
    wcq = w_in[:, o_mla:o_mla + MLA_Q_LORA]
    wckv = w_in[:, o_mla + MLA_Q_LORA:o_mla + MLA_Q_LORA + MLA_KV_LORA]
    wkr = w_in[:, o_mla + MLA_Q_LORA + MLA_KV_LORA:]
    zeros = lambda n: jnp.zeros((d, n), F32)
    wm = jnp.concatenate([wcq, zeros(256 - MLA_Q_LORA), wckv, wkr, wkr[:, _partner_perm(MLA_ROPE)],
                          zeros(512 - 448)], axis=1)

    wqu = p["mla_w_q_up"][l].reshape(MLA_Q_LORA, MLA_HEADS, MLA_NOPE + MLA_ROPE)
    rope_part = wqu[:, :, MLA_NOPE:]
    pad = jnp.zeros((MLA_Q_LORA, MLA_HEADS, MLA_QK_PAD - MLA_NOPE - MLA_ROPE), F32)
    wqu_plain = jnp.concatenate([wqu, pad], axis=2).reshape(MLA_Q_LORA, -1)
    wqu_rot = jnp.concatenate([jnp.zeros_like(wqu[:, :, :MLA_NOPE]), rope_part[:, :, _partner_perm(MLA_ROPE)], pad],
                              axis=2).reshape(MLA_Q_LORA, -1)
    wkvu = p["mla_w_kv_up"][l].reshape(MLA_KV_LORA, MLA_HEADS, MLA_NOPE + MLA_V)
    wk2 = jnp.concatenate([wkvu[:, :, :MLA_NOPE],
                           jnp.zeros((MLA_KV_LORA, MLA_HEADS, MLA_QK_PAD - MLA_NOPE), F32)],
                          axis=2).reshape(MLA_KV_LORA, -1)
    eye = jnp.eye(MLA_ROPE, dtype=F32)
    ek_h = jnp.concatenate([jnp.zeros((MLA_ROPE, MLA_NOPE), F32), eye,
                            jnp.zeros((MLA_ROPE, MLA_QK_PAD - MLA_NOPE - MLA_ROPE), F32)], axis=1)
    ek = jnp.tile(ek_h, (1, MLA_HEADS))
    wv2 = wkvu[:, :, MLA_NOPE:].reshape(MLA_KV_LORA, -1)

    w_out = p["w_out"][l]
    row = lambda a: a.reshape(1, -1)
    pad16 = lambda a: jnp.concatenate([a.reshape(1, -1), jnp.zeros((1, LANES - 2 * SSD_HEADS), F32)], axis=1)
    return {
        "norm_mix_pre": row(p["norm_mix_pre"][l]), "norm_mix_post": row(p["norm_mix_post"][l]),
        "norm_ffn_pre": row(p["norm_ffn_pre"][l]), "norm_ffn_post": row(p["norm_ffn_post"][l]),
        "wa": wa.astype(BF16),
        "wdk": jnp.concatenate([wk, wk[:, perm]], axis=1).astype(BF16),
        "wm": wm.astype(BF16),
        "wqt": jnp.concatenate([wq, wq[:, perm]], axis=1).T.astype(BF16),
        "wvt": wv.T.astype(BF16),
        "qnw": row(p["mla_q_norm"][l]), "kvnw": row(p["mla_kv_norm"][l]),
        "wqut": jnp.concatenate([wqu_plain, wqu_rot], axis=1).T.astype(BF16),
        "wk2": wk2.astype(BF16), "ek": ek.astype(BF16), "wvt2": wv2.T.astype(BF16),
        "conv_w": p["ssd_conv_w"][l], "conv_b": row(p["ssd_conv_b"][l]),
        "alog": pad16(p["ssd_a_log"][l]), "dtb": pad16(p["ssd_dt_bias"][l]),
        "dsk": row(p["ssd_d"][l]), "ssd_norm": row(p["ssd_norm"][l]),
        "lq1": row(p["diff_lam_q1"][l]), "lk1": row(p["diff_lam_k1"][l]),
        "lq2": row(p["diff_lam_q2"][l]), "lk2": row(p["diff_lam_k2"][l]),
        "subw": p["diff_subln"][l].reshape(-1, 1),
        "wo_s": w_out[:SSD_INNER].astype(BF16),
        "wo_d": w_out[SSD_INNER:SSD_INNER + DIFF_HEADS * DIFF_V].astype(BF16),
        "wo_a": w_out[SSD_INNER + DIFF_HEADS * DIFF_V:].astype(BF16),
        "rwt": p["router_w"][l].T, "rb": p["router_b"][l].reshape(-1, 1),
        "w_gate": p["w_gate"][l], "w_up": p["w_up"][l], "w_down": p["w_down"][l],
    }


def kernel(x, c, ctx, c_ctx, ada_w, ada_b, norm_mix_pre, norm_mix_post, norm_ffn_pre, norm_ffn_post, w_in, ssd_conv_w, ssd_conv_b, ssd_a_log, ssd_dt_bias, ssd_d, ssd_norm, diff_lam_q1, diff_lam_k1, diff_lam_q2, diff_lam_k2, diff_subln, mla_q_norm, mla_w_q_up, mla_kv_norm, mla_w_kv_up, w_out, router_w, router_b, w_gate, w_up, w_down):
    p = dict(norm_mix_pre=norm_mix_pre, norm_mix_post=norm_mix_post, norm_ffn_pre=norm_ffn_pre,
             norm_ffn_post=norm_ffn_post, w_in=w_in, ssd_conv_w=ssd_conv_w, ssd_conv_b=ssd_conv_b,
             ssd_a_log=ssd_a_log, ssd_dt_bias=ssd_dt_bias, ssd_d=ssd_d, ssd_norm=ssd_norm,
             diff_lam_q1=diff_lam_q1, diff_lam_k1=diff_lam_k1, diff_lam_q2=diff_lam_q2, diff_lam_k2=diff_lam_k2,
             diff_subln=diff_subln, mla_q_norm=mla_q_norm, mla_w_q_up=mla_w_q_up, mla_kv_norm=mla_kv_norm,
             mla_w_kv_up=mla_w_kv_up, w_out=w_out, router_w=router_w, router_b=router_b,
             w_gate=w_gate, w_up=w_up, w_down=w_down)
    b, seq, d = x.shape
    nctx = ctx.shape[1]
    depth = ada_w.shape[0]
    assert nctx == TILE and seq % TILE == 0 and seq % GRID_W == 0
    nt = nctx + seq
    caps = (EC_CAPACITY * nctx // N_EXPERTS, EC_CAPACITY * seq // N_EXPERTS)
    assert caps[0] % 16 == 0 and caps[1] % 16 == 0 and caps[0] + caps[1] >= WIN

    cvec = jnp.concatenate([c, c_ctx[None, :], jnp.zeros((8 - b - 1, d), F32)], axis=0)
    mods = _adaln(cvec, ada_w, ada_b).reshape(depth, 8, N_MOD, d)
    tabs = _rope_tables(seq, nctx)
    t = jnp.concatenate([ctx, x], axis=1)
    for l in range(depth):
        lw = _layer_weights(l, p)
        lambda_init = 0.8 - 0.6 * math.exp(-0.3 * l)
        mod = jnp.stack([jnp.broadcast_to(mods[l, b], (b, N_MOD, d)), mods[l, :b]], axis=1)
        z, xbc, dt, dq, dk, dv, mq, mk, mv = _inproj(t, mod, lw, tabs)
        s = _ssd(z, xbc, dt, lw)
        da = _diff_attn(dq, dk, dv, lw, lambda_init).reshape(b, DIFF_HEADS * DIFF_V, nt)
        aa = _mla_attn(mq, mk, mv).reshape(b, MLA_HEADS * MLA_V, nt)
        t, hf, aff = _outproj(t, s, da, aa, mod, lw)
        pos, gate, cum = _route(aff, caps)
        cum_flat = cum.reshape(-1)
        xg, gc = _gather(cum_flat, hf, pos, gate, caps[0] + caps[1])
        y = _experts(xg, gc, lw)
        t = _combine(cum_flat, t, y, pos, mod, lw)
    return t[:, nctx:]
```

```python
import functools
import math

import jax
import jax.numpy as jnp
from jax import lax
from jax.experimental import pallas as pl
from jax.experimental.pallas import tpu as pltpu

F32 = jnp.float32
BF16 = jnp.bfloat16
I32 = jnp.int32
HIGHEST = lax.Precision.HIGHEST

EPS = 1e-6
GRID_W = 64
ROPE_BASE = 10000.0
N_MOD = 6

SSD_HEADS = 8
SSD_HEAD_DIM = 64
SSD_INNER = SSD_HEADS * SSD_HEAD_DIM
SSD_GROUPS = 2
SSD_STATE = 64
SSD_CHUNK = 128
SSD_BC = SSD_GROUPS * SSD_STATE
SSD_CONV_DIM = SSD_INNER + 2 * SSD_BC
SSD_PAIRS = SSD_HEADS // 2

DIFF_HEADS = 4
DIFF_QK = 32
DIFF_V = 64
DIFF_MAPS = 2 * DIFF_HEADS

MLA_HEADS = 4
MLA_Q_LORA = 192
MLA_KV_LORA = 128
MLA_NOPE = 64
MLA_ROPE = 32
MLA_V = 64
MLA_QK_PAD = 128

N_EXPERTS = 16
EC_CAPACITY = 2

TILE = 256
LANES = 128
VMEM_LIMIT = 56 * 1024 * 1024

NT_DIMS = (((1,), (1,)), ((), ()))
TN_DIMS = (((0,), (0,)), ((), ()))


def _cparams(sem, vmem=None):
    return pltpu.CompilerParams(dimension_semantics=sem, vmem_limit_bytes=vmem)


def _rms(x, w):
    return x * lax.rsqrt(jnp.mean(x * x, axis=-1, keepdims=True) + EPS) * w


def _silu(x):
    return x * jax.nn.sigmoid(x)


def _dot(a, b):
    return jnp.dot(a, b, preferred_element_type=F32)


def _adaln_kernel(c_ref, w_ref, b_ref, o_ref):
    s = _silu(c_ref[...])
    o_ref[0] = lax.dot_general(s, w_ref[0], (((1,), (0,)), ((), ())), precision=HIGHEST,
                               preferred_element_type=F32) + b_ref[0]


def _adaln(cvec, ada_w, ada_b):
    depth, d, nd = ada_w.shape
    rows = cvec.shape[0]
    return pl.pallas_call(
        _adaln_kernel,
        grid=(depth, nd // d),
        in_specs=[pl.BlockSpec((rows, d), lambda l, j: (0, 0)),
                  pl.BlockSpec((1, d, d), lambda l, j: (l, 0, j)),
                  pl.BlockSpec((1, 1, d), lambda l, j: (l, 0, j))],
        out_specs=pl.BlockSpec((1, rows, d), lambda l, j: (l, 0, j)),
        out_shape=jax.ShapeDtypeStruct((depth, rows, nd), F32),
        compiler_params=_cparams(("arbitrary", "arbitrary")),
        name="adaln",
    )(cvec, ada_w, ada_b.reshape(depth, 1, nd))


def _token_stream(t):
    if isinstance(t, tuple):
        x, ctx = t
        b, seq, d = x.shape
        specs = [pl.BlockSpec((1, TILE, d), lambda bi, i: (bi, jnp.maximum(i - 1, 0), 0)),
                 pl.BlockSpec((1, TILE, d), lambda bi, i: (bi, 0, 0))]
        return (x, ctx), specs, (b, seq + ctx.shape[1], d)
    b, nt, d = t.shape
    specs = [pl.BlockSpec((1, TILE, d), lambda bi, i: (bi, i, 0)),
             pl.BlockSpec((1, TILE, d), lambda bi, i: (bi, 0, 0))]
    return (t, t), specs, (b, nt, d)


def _stream_tile(x_ref, c_ref):
    return jnp.where(pl.program_id(1) == 0, c_ref[0], x_ref[0])


def _inproj_kernel(x_ref, c_ref, mod_ref, nw_ref, wa_ref, wdk_ref, wm_ref, wqt_ref, wvt_ref,
                   ck_ref, sk_ref, ct_ref, st_ref, qnw_ref, kvnw_ref, wqut_ref, wk2_ref, ek_ref, wvt2_ref,
                   z_ref, xbc_ref, dt_ref, dq_ref, dk_ref, dv_ref, mq_ref, mk_ref, mv_ref):
    x = _stream_tile(x_ref, c_ref)
    mod = mod_ref[0, 0]
    h = (_rms(x, nw_ref[...]) * (1.0 + mod[1:2]) + mod[0:1]).astype(BF16)

    ra = _dot(h, wa_ref[...])
    z_ref[0] = ra[:, :SSD_INNER].astype(BF16)
    xbc_ref[0] = ra[:, SSD_INNER:SSD_INNER + SSD_CONV_DIM]
    dt_ref[0] = ra[:, SSD_INNER + SSD_CONV_DIM:]

    nk = DIFF_MAPS * DIFF_QK
    rk = _dot(h, wdk_ref[...])
    k = (rk[:, :nk] * ck_ref[...] + rk[:, nk:] * sk_ref[...]).astype(BF16)
    for m in range(DIFF_MAPS):
        dk_ref[0, m // 2, m % 2, 0] = k[:, DIFF_QK * m:DIFF_QK * (m + 1)]

    ct = ct_ref[...]
    st = st_ref[...]
    rq = lax.dot_general(wqt_ref[...], h, NT_DIMS, preferred_element_type=F32)
    for m in range(DIFF_MAPS):
        lo = DIFF_QK * m
        qm = rq[lo:lo + DIFF_QK] * ct + rq[nk + lo:nk + lo + DIFF_QK] * st
        dq_ref[0, m // 2, m % 2] = (qm * DIFF_C_EXP).astype(BF16)

    rv = lax.dot_general(wvt_ref[...], h, NT_DIMS, preferred_element_type=F32).astype(BF16)
    for hd in range(DIFF_HEADS):
        dv_ref[0, hd, 0] = rv[DIFF_V * hd:DIFF_V * (hd + 1)]

    rm = _dot(h, wm_ref[...])
    cq = _rms(rm[:, :MLA_Q_LORA], qnw_ref[...]).astype(BF16)
    ckv = _rms(rm[:, 256:256 + MLA_KV_LORA], kvnw_ref[...]).astype(BF16)
    kr = rm[:, 384:384 + MLA_ROPE] * ck_ref[:, :MLA_ROPE] + rm[:, 416:416 + MLA_ROPE] * sk_ref[:, :MLA_ROPE]

    rq2 = lax.dot_general(wqut_ref[...], cq, NT_DIMS, preferred_element_type=F32)
    nq = MLA_HEADS * MLA_QK_PAD
    ones = jnp.ones((MLA_NOPE, ct.shape[1]), F32)
    pad1 = jnp.ones((MLA_QK_PAD - MLA_NOPE - MLA_ROPE, ct.shape[1]), F32)
    ct_h = jnp.concatenate([ones, ct, pad1], axis=0)
    st_h = jnp.concatenate([0.0 * ones, st, 0.0 * pad1], axis=0)
    for hd in range(MLA_HEADS):
        lo = MLA_QK_PAD * hd
        qh = rq2[lo:lo + MLA_QK_PAD] * ct_h + rq2[nq + lo:nq + lo + MLA_QK_PAD] * st_h
        mq_ref[0, hd] = (qh * MLA_C_EXP).astype(BF16)

    k2 = (_dot(ckv, wk2_ref[...]) + _dot(kr.astype(BF16), ek_ref[...])).astype(BF16)
    for hd in range(MLA_HEADS):
        mk_ref[0, hd, 0] = k2[:, MLA_QK_PAD * hd:MLA_QK_PAD * (hd + 1)]
    rv2 = lax.dot_general(wvt2_ref[...], ckv, NT_DIMS, preferred_element_type=F32).astype(BF16)
    for hd in range(MLA_HEADS):
        mv_ref[0, hd, 0] = rv2[MLA_V * hd:MLA_V * (hd + 1)]


def _inproj(t, mod, lw, tabs):
    streams, stream_specs, (b, nt, d) = _token_stream(t)
    nti = nt // TILE
    full = lambda a: pl.BlockSpec(a.shape, lambda bi, i: (0,) * a.ndim)
    tok = lambda w: pl.BlockSpec((TILE, w), lambda bi, i: (i, 0))
    tokt = lambda w: pl.BlockSpec((w, TILE), lambda bi, i: (0, i))
    ws = [lw["norm_mix_pre"], lw["wa"], lw["wdk"], lw["wm"], lw["wqt"], lw["wvt"]]
    ws2 = [lw["qnw"], lw["kvnw"], lw["wqut"], lw["wk2"], lw["ek"], lw["wvt2"]]
    out_shape = [
        jax.ShapeDtypeStruct((b, nt, SSD_INNER), BF16),
        jax.ShapeDtypeStruct((b, nt, SSD_CONV_DIM), F32),
        jax.ShapeDtypeStruct((b, nt, LANES), F32),
        jax.ShapeDtypeStruct((b, DIFF_HEADS, 2, DIFF_QK, nt), BF16),
        jax.ShapeDtypeStruct((b, DIFF_HEADS, 2, nti, TILE, DIFF_QK), BF16),
        jax.ShapeDtypeStruct((b, DIFF_HEADS, nti, DIFF_V, TILE), BF16),
        jax.ShapeDtypeStruct((b, MLA_HEADS, MLA_QK_PAD, nt), BF16),
        jax.ShapeDtypeStruct((b, MLA_HEADS, nti, TILE, MLA_QK_PAD), BF16),
        jax.ShapeDtypeStruct((b, MLA_HEADS, nti, MLA_V, TILE), BF16),
    ]
    out_specs = [
        pl.BlockSpec((1, TILE, SSD_INNER), lambda bi, i: (bi, i, 0)),
        pl.BlockSpec((1, TILE, SSD_CONV_DIM), lambda bi, i: (bi, i, 0)),
        pl.BlockSpec((1, TILE, LANES), lambda bi, i: (bi, i, 0)),
        pl.BlockSpec((1, DIFF_HEADS, 2, DIFF_QK, TILE), lambda bi, i: (bi, 0, 0, 0, i)),
        pl.BlockSpec((1, DIFF_HEADS, 2, 1, TILE, DIFF_QK), lambda bi, i: (bi, 0, 0, i, 0, 0)),
        pl.BlockSpec((1, DIFF_HEADS, 1, DIFF_V, TILE), lambda bi, i: (bi, 0, i, 0, 0)),
        pl.BlockSpec((1, MLA_HEADS, MLA_QK_PAD, TILE), lambda bi, i: (bi, 0, 0, i)),
        pl.BlockSpec((1, MLA_HEADS, 1, TILE, MLA_QK_PAD), lambda bi, i: (bi, 0, i, 0, 0)),
        pl.BlockSpec((1, MLA_HEADS, 1, MLA_V, TILE), lambda bi, i: (bi, 0, i, 0, 0)),
    ]
    in_specs = (stream_specs
                + [pl.BlockSpec((1, 1, N_MOD, d), lambda bi, i: (bi, jnp.minimum(i, 1), 0, 0))]
                + [full(a) for a in ws]
                + [tok(DIFF_MAPS * DIFF_QK), tok(DIFF_MAPS * DIFF_QK), tokt(DIFF_QK), tokt(DIFF_QK)]
                + [full(a) for a in ws2])
    return pl.pallas_call(
        _inproj_kernel,
        grid=(b, nti),
        in_specs=in_specs,
        out_specs=out_specs,
        out_shape=out_shape,
        compiler_params=_cparams(("arbitrary", "arbitrary"), VMEM_LIMIT),
        name="inproj",
    )(*streams, mod, *ws, tabs["ck"], tabs["sk"], tabs["ct"], tabs["st"], *ws2)


def _ssd_chunk_of(ph, i, nck, nctx):
    back = jnp.where(i < nctx, nctx - 1 - i, nck - 1 + nctx - i)
    return jnp.where(ph == 0, back, i)


def _ssd_kernel(xc_ref, xp_ref, xn_ref, dt_ref, z_ref, cw_ref, cb_ref, alog_ref, dtb_ref, dsk_ref, nw_ref,
                o_ref, s_ref, sb_ref, *cache, nck, nctx, nb):
    ph = pl.program_id(1)
    i = pl.program_id(2)
    c = _ssd_chunk_of(ph, i, nck, nctx)

    @pl.when(i == 0)
    def _():
        s_ref[...] = jnp.zeros_like(s_ref)

    one = lambda ref, bb: ref.at[pl.ds(bb, 1)]
    fns = [_ssd_sample(one(xc_ref, bb), one(xp_ref, bb), one(xn_ref, bb), one(dt_ref, bb), one(z_ref, bb),
                       cw_ref, cb_ref, alog_ref, dtb_ref, dsk_ref, nw_ref, one(o_ref, bb),
                       s_ref.at[bb], sb_ref.at[bb], [r.at[bb] for r in cache], c, nck=nck, nctx=nctx)
           for bb in range(nb)]

    @pl.when(ph == 0)
    def _():
        for backward_states, _ in fns:
            backward_states()

    @pl.when(ph == 1)
    def _():
        for _, forward_and_output in fns:
            forward_and_output()


def _ssd_sample(xc_ref, xp_ref, xn_ref, dt_ref, z_ref, cw_ref, cb_ref, alog_ref, dtb_ref, dsk_ref, nw_ref,
                o_ref, s_ref, sb_ref, cache, c, *, nck, nctx):
    q = SSD_CHUNK
    nh2 = 2 * SSD_HEADS
    cx_ref, cbc_ref, crow_ref, ccol_ref, cbt_ref = cache
    ri = lax.broadcasted_iota(I32, (q, q), 0)
    ci = lax.broadcasted_iota(I32, (q, q), 1)
    lower = ci <= ri
    upper = ci >= ri
    lane = ci
    first_half_s = ri < SSD_STATE
    first_half_l = lane < SSD_HEAD_DIM
    blockdiag = first_half_s == first_half_l

    def chunk_values():
        x = xc_ref[0]
        has_prev = jnp.logical_and(c != 0, c != nctx)
        has_next = jnp.logical_and(c != nctx - 1, c != nck - 1)
        prev_row = jnp.where(has_prev, xp_ref[0][7:8, :], 0.0)
        next_row = jnp.where(has_next, xn_ref[0][0:1, :], 0.0)
        row = lax.broadcasted_iota(I32, x.shape, 0)
        xm1 = jnp.where(row == 0, prev_row, pltpu.roll(x, 1, 0))
        xp1 = jnp.where(row == q - 1, next_row, pltpu.roll(x, q - 1, 0))
        cw = cw_ref[...]
        u = _silu(xm1 * cw[0:1] + x * cw[1:2] + xp1 * cw[2:3] + cb_ref[...])
        xs = u[:, :SSD_INNER].astype(BF16)
        bc = u[:, SSD_INNER:]
        xdt = dt_ref[0].T[:nh2] + dtb_ref[...]
        dtt = jnp.maximum(xdt, 0.0) + jnp.log1p(jnp.exp(-jnp.abs(xdt)))
        dat = dtt * (-jnp.exp(alog_ref[...]))
        tri_dims = (((1,), (0,)), ((), ()))
        acf = lax.dot_general(dat, upper.astype(F32), tri_dims, precision=HIGHEST, preferred_element_type=F32)
        acb = lax.dot_general(dat, lower.astype(F32), tri_dims, precision=HIGHEST, preferred_element_type=F32)
        act = jnp.where(ri[:nh2] < SSD_HEADS, acf, acb)
        rows = jnp.concatenate([dtt, act], axis=0)
        cols = jnp.concatenate([rows, jnp.zeros((q - 2 * nh2, q), F32)], axis=0).T
        bt = bc[:, :SSD_BC].T
        return xs, bc, rows, cols, bt

    def bodies(xs, bc, rows, cols, bt):
        bm = bc[:, :SSD_BC]
        cm = bc[:, SSD_BC:]
        dtt = rows[:nh2]
        act = rows[nh2:]
        dtc = cols
        acc = pltpu.roll(cols, LANES - nh2, 1)

        def pair_vals(arr_c, arr_t, h0):
            col = jnp.where(first_half_l, arr_c[:, h0:h0 + 1], arr_c[:, h0 + 1:h0 + 2])
            rowv = jnp.where(first_half_s, arr_t[h0:h0 + 1, :], arr_t[h0 + 1:h0 + 2, :])
            return col, rowv

        def state_update(p, d):
            g = (2 * p) // (SSD_HEADS // SSD_GROUPS)
            h0 = d * SSD_HEADS + 2 * p
            edge = q - 1 if d == 0 else 0
            alast_row = jnp.where(first_half_s[:, 0:1], acc[edge:edge + 1, h0:h0 + 1],
                                  acc[edge:edge + 1, h0 + 1:h0 + 2])
            _, ar = pair_vals(acc, act, h0)
            _, dr = pair_vals(dtc, dtt, h0)
            w = jnp.exp(alast_row - ar) * dr
            btg = bt[SSD_STATE * g:SSD_STATE * (g + 1)]
            lhs = (jnp.concatenate([btg, btg], axis=0) * w).astype(BF16)
            xs2 = xs[:, 2 * SSD_HEAD_DIM * p:2 * SSD_HEAD_DIM * (p + 1)]
            upd = jnp.where(blockdiag, _dot(lhs, xs2), 0.0)
            return jnp.exp(alast_row) * s_ref[d, p] + upd

        def backward():
            for p in range(SSD_PAIRS):
                sb_ref[c, p] = s_ref[1, p].astype(BF16)
                s_ref[1, p] = state_update(p, 1)

        def forward():
            roll_c = pltpu.roll(cm, SSD_STATE, 1)
            dsk = dsk_ref[...]
            ys = []
            for p in range(SSD_PAIRS):
                g = (2 * p) // (SSD_HEADS // SSD_GROUPS)
                cg_only = jnp.where((lane < SSD_STATE) == (g == 0), cm, 0.0).astype(BF16)
                cb = lax.dot_general(cg_only, bm.astype(BF16), NT_DIMS, preferred_element_type=F32)
                ms = []
                for hh in range(2):
                    hf = 2 * p + hh
                    hb = SSD_HEADS + hf
                    lf = (jnp.exp(jnp.where(lower, acc[:, hf:hf + 1] - act[hf:hf + 1, :], -jnp.inf))
                          * dtt[hf:hf + 1, :])
                    lb = (jnp.exp(jnp.where(upper, acc[:, hb:hb + 1] - act[hb:hb + 1, :], -jnp.inf))
                          * dtt[hb:hb + 1, :])
                    ms.append((cb * (lf + lb) + jnp.where(ri == ci, dsk[:, hf:hf + 1], 0.0)).astype(BF16))
                xs2 = xs[:, 2 * SSD_HEAD_DIM * p:2 * SSD_HEAD_DIM * (p + 1)]
                zero = jnp.zeros_like(xs2)
                rhs = jnp.concatenate([jnp.where(first_half_l, xs2, zero), jnp.where(first_half_l, zero, xs2)],
                                      axis=0)
                y = _dot(jnp.concatenate(ms, axis=1), rhs)
                cdup = jnp.where(first_half_l == (g == 0), cm, roll_c)
                ef, _ = pair_vals(acc, act, 2 * p)
                eb, _ = pair_vals(acc, act, SSD_HEADS + 2 * p)
                lhs_off = jnp.concatenate([cdup * jnp.exp(ef), cdup * jnp.exp(eb)], axis=1).astype(BF16)
                rhs_off = jnp.concatenate([s_ref[0, p].astype(BF16), sb_ref[c, p]], axis=0)
                ys.append(y + _dot(lhs_off, rhs_off))
                s_ref[0, p] = state_update(p, 0)
            y = jnp.concatenate(ys, axis=1)
            zf = z_ref[0].astype(F32)
            gt = y * _silu(zf)
            nw = nw_ref[...]
            gw = SSD_INNER // SSD_GROUPS
            outs = [_rms(gt[:, gw * g:gw * (g + 1)], nw[:, gw * g:gw * (g + 1)]) for g in range(SSD_GROUPS)]
            o_ref[0] = jnp.concatenate(outs, axis=1).astype(BF16)

        return backward, forward

    def backward_states():
        xs, bc, rows, cols, bt = chunk_values()
        cx_ref[c] = xs
        cbc_ref[c] = bc
        crow_ref[c] = rows
        ccol_ref[c] = cols
        cbt_ref[c] = bt
        bodies(xs, bc, rows, cols, bt)[0]()

    def forward_and_output():
        bodies(cx_ref[c], cbc_ref[c], crow_ref[c], ccol_ref[c], cbt_ref[c])[1]()

    return backward_states, forward_and_output


def _ssd(z, xbc, dt, lw):
    b, nt, _ = z.shape
    q = SSD_CHUNK
    nck = nt // q
    nctx = TILE // q
    rows8 = q // 8
    nb = 1
    chunk = functools.partial(_ssd_chunk_of, nck=nck, nctx=nctx)
    full = lambda a: pl.BlockSpec(a.shape, lambda bi, ph, i: (0,) * a.ndim)
    ws = [lw["conv_w"], lw["conv_b"], lw["alog"], lw["dtb"], lw["dsk"], lw["ssd_norm"]]
    kern = functools.partial(_ssd_kernel, nck=nck, nctx=nctx, nb=nb)
    return pl.pallas_call(
        kern,
        grid=(b // nb, 2, nck),
        in_specs=[pl.BlockSpec((nb, q, SSD_CONV_DIM), lambda bi, ph, i: (bi, chunk(ph, i), 0)),
                  pl.BlockSpec((nb, 8, SSD_CONV_DIM),
                               lambda bi, ph, i: (bi, jnp.maximum(chunk(ph, i) * rows8 - 1, 0), 0)),
                  pl.BlockSpec((nb, 8, SSD_CONV_DIM),
                               lambda bi, ph, i: (bi, jnp.minimum((chunk(ph, i) + 1) * rows8, nck * rows8 - 1), 0)),
                  pl.BlockSpec((nb, q, LANES), lambda bi, ph, i: (bi, chunk(ph, i), 0)),
                  pl.BlockSpec((nb, q, SSD_INNER), lambda bi, ph, i: (bi, chunk(ph, i), 0))]
                 + [full(a) for a in ws],
        out_specs=pl.BlockSpec((nb, q, SSD_INNER), lambda bi, ph, i: (bi, jnp.where(ph == 0, 0, i), 0)),
        out_shape=jax.ShapeDtypeStruct((b, nt, SSD_INNER), BF16),
        scratch_shapes=[pltpu.VMEM((nb, 2, SSD_PAIRS, 2 * SSD_STATE, 2 * SSD_HEAD_DIM), F32),
                        pltpu.VMEM((nb, nck, SSD_PAIRS, 2 * SSD_STATE, 2 * SSD_HEAD_DIM), BF16),
                        pltpu.VMEM((nb, nck, q, SSD_INNER), BF16),
                        pltpu.VMEM((nb, nck, q, 2 * SSD_BC), F32),
                        pltpu.VMEM((nb, nck, 4 * SSD_HEADS, q), F32),
                        pltpu.VMEM((nb, nck, q, LANES), F32),
                        pltpu.VMEM((nb, nck, SSD_BC, q), F32)],
        compiler_params=_cparams(("arbitrary", "arbitrary", "arbitrary"), VMEM_LIMIT),
        name="ssd",
    )(xbc, xbc, xbc, dt, z, *ws)


ACC_ROWS = 80
DIFF_C_EXP = (DIFF_QK ** -0.5) * math.log2(math.e)
MLA_C_EXP = ((MLA_NOPE + MLA_ROPE) ** -0.5) * math.log2(math.e)


def _ones_rows(tk):
    return (lax.broadcasted_iota(I32, (ACC_ROWS - DIFF_V, tk), 0) == 0).astype(BF16)


def _score_step(kqs, s_ref):
    for idx, (k, q) in enumerate(kqs):
        s_ref[idx, 0:k.shape[0], :] = _dot(k, q)


def _softmax_pv_step(n_keys, vaugs, s_ref, m_ref, acc_ref):
    for idx in range(len(vaugs)):
        s = s_ref[idx, 0:n_keys, :]
        m = m_ref[idx]
        mn = jnp.maximum(m, jnp.max(s, axis=0, keepdims=True))
        p = jnp.exp2(s - mn).astype(BF16)
        acc_ref[idx] = acc_ref[idx] * jnp.exp2(m - mn) + _dot(vaugs[idx], p)
        m_ref[idx] = mn


def _attn_init(m_ref, acc_ref):
    m_ref[...] = jnp.full(m_ref.shape, -jnp.inf, F32)
    acc_ref[...] = jnp.zeros_like(acc_ref)


DIFF_KEY_GROUP = 2
MLA_KEY_GROUP = 4


def _key_group(nti, want):
    n_lat = nti - 1
    assert n_lat % 2 == 0
    while n_lat % (2 * want):
        want //= 2
    return want


def _chunks_k(k_ref, lead, c0, n):
    return jnp.concatenate([k_ref[lead + (c0 + j,)] for j in range(n)], axis=0)


def _chunks_v(v_ref, lead, c0, n):
    v = jnp.concatenate([v_ref[lead + (c0 + j,)] for j in range(n)], axis=1)
    return jnp.concatenate([v, _ones_rows(n * TILE)], axis=0)


def _pipelined_keys(scores, consume, nti, group):
    steps = (nti - 1) // group
    first = lambda k: 1 + (k - 1) * group
    scores(0, 1, 0)
    latent = pl.program_id(1) > 0

    @pl.when(jnp.logical_not(latent))
    def _():
        consume(0, 1, 0)

    @pl.when(latent)
    def _():
        scores(first(1), group, 1)
        consume(0, 1, 0)
        scores(first(2), group, 0)
        consume(first(1), group, 1)

        def body(j, carry):
            k = 2 * j
            scores(first(k + 1), group, 1)
            consume(first(k), group, 0)
            scores(first(k + 2), group, 0)
            consume(first(k + 1), group, 1)
            return carry

        lax.fori_loop(1, steps // 2, body, 0)
        consume(first(steps), group, 0)


def _diff_attn_kernel(lq1_ref, lk1_ref, lq2_ref, lk2_ref, subw_ref, q_ref, k_ref, v_ref, o_ref,
                      m_ref, acc_ref, sa_ref, sb_ref, *, nti, group, lambda_init):
    _attn_init(m_ref, acc_ref)
    slots = (sa_ref, sb_ref)

    def scores(c0, n, slot):
        _score_step([(_chunks_k(k_ref, (0, h, j), c0, n), q_ref[0, h, j])
                     for h in range(DIFF_HEADS) for j in range(2)], slots[slot])

    def consume(c0, n, slot):
        vaugs = []
        for h in range(DIFF_HEADS):
            vaugs += [_chunks_v(v_ref, (0, h), c0, n)] * 2
        _softmax_pv_step(n * TILE, vaugs, slots[slot], m_ref, acc_ref)

    _pipelined_keys(scores, consume, nti, group)
    lam =(jnp.exp(jnp.sum(lq1_ref[...] * lk1_ref[...], keepdims=True))
           - jnp.exp(jnp.sum(lq2_ref[...] * lk2_ref[...], keepdims=True)) + lambda_init)
    for h in range(DIFF_HEADS):
        a1 = acc_ref[2 * h]
        a2 = acc_ref[2 * h + 1]
        o = a1[:DIFF_V] / a1[DIFF_V:DIFF_V + 1] - lam * (a2[:DIFF_V] / a2[DIFF_V:DIFF_V + 1])
        o = o * lax.rsqrt(jnp.mean(o * o, axis=0, keepdims=True) + EPS) * subw_ref[...]
        o_ref[0, h] = (o * (1.0 - lambda_init)).astype(BF16)


def _diff_attn(dq, dk, dv, lw, lambda_init):
    b, nh, _, dqk, nt = dq.shape
    nti = nt // TILE
    group = _key_group(nti, DIFF_KEY_GROUP)
    kern = functools.partial(_diff_attn_kernel, nti=nti, group=group, lambda_init=lambda_init)
    score_slot = pltpu.VMEM((2 * nh, group * TILE, TILE), F32)
    vec = pl.BlockSpec((1, DIFF_QK), lambda bi, i: (0, 0))
    return pl.pallas_call(
        kern,
        grid=(b, nti),
        in_specs=[vec, vec, vec, vec,
                  pl.BlockSpec((DIFF_V, 1), lambda bi, i: (0, 0)),
                  pl.BlockSpec((1, nh, 2, dqk, TILE), lambda bi, i: (bi, 0, 0, 0, i)),
                  pl.BlockSpec((1, nh, 2, nti, TILE, dqk), lambda bi, i: (bi, 0, 0, 0, 0, 0)),
                  pl.BlockSpec((1, nh, nti, DIFF_V, TILE), lambda bi, i: (bi, 0, 0, 0, 0))],
        out_specs=pl.BlockSpec((1, nh, DIFF_V, TILE), lambda bi, i: (bi, 0, 0, i)),
        out_shape=jax.ShapeDtypeStruct((b, nh, DIFF_V, nt), BF16),
        scratch_shapes=[pltpu.VMEM((2 * nh, 1, TILE), F32), pltpu.VMEM((2 * nh, ACC_ROWS, TILE), F32),
                        score_slot, score_slot],
        compiler_params=_cparams(("arbitrary", "arbitrary"), VMEM_LIMIT),
        name="diff_attn",
    )(lw["lq1"], lw["lk1"], lw["lq2"], lw["lk2"], lw["subw"], dq, dk, dv)


def _mla_attn_kernel(q_ref, k_ref, v_ref, o_ref, m_ref, acc_ref, sa_ref, sb_ref, *, nti, group):
    _attn_init(m_ref, acc_ref)
    slots = (sa_ref, sb_ref)

    def scores(c0, n, slot):
        _score_step([(_chunks_k(k_ref, (0, h), c0, n), q_ref[0, h]) for h in range(MLA_HEADS)], slots[slot])

    def consume(c0, n, slot):
        vaugs = [_chunks_v(v_ref, (0, h), c0, n) for h in range(MLA_HEADS)]
        _softmax_pv_step(n * TILE, vaugs, slots[slot], m_ref, acc_ref)

    _pipelined_keys(scores, consume, nti, group)
    for h in range(MLA_HEADS):
        a = acc_ref[h]
        o_ref[0, h] = (a[:MLA_V] / a[MLA_V:MLA_V + 1]).astype(BF16)


def _mla_attn(mq, mk, mv):
    b, nh, dpad, nt = mq.shape
    nti = nt // TILE
    group = _key_group(nti, MLA_KEY_GROUP)
    kern = functools.partial(_mla_attn_kernel, nti=nti, group=group)
    score_slot = pltpu.VMEM((nh, group * TILE, TILE), F32)
    return pl.pallas_call(
        kern,
        grid=(b, nti),
        in_specs=[pl.BlockSpec((1, nh, dpad, TILE), lambda bi, i: (bi, 0, 0, i)),
                  pl.BlockSpec((1, nh, nti, TILE, dpad), lambda bi, i: (bi, 0, 0, 0, 0)),
                  pl.BlockSpec((1, nh, nti, MLA_V, TILE), lambda bi, i: (bi, 0, 0, 0, 0))],
        out_specs=pl.BlockSpec((1, nh, MLA_V, TILE), lambda bi, i: (bi, 0, 0, i)),
        out_shape=jax.ShapeDtypeStruct((b, nh, MLA_V, nt), BF16),
        scratch_shapes=[pltpu.VMEM((nh, 1, TILE), F32), pltpu.VMEM((nh, ACC_ROWS, TILE), F32),
                        score_slot, score_slot],
        compiler_params=_cparams(("arbitrary", "arbitrary"), VMEM_LIMIT),
        name="mla_attn",
    )(mq, mk, mv)


OUTPROJ_SPLIT = 2

def _outproj_kernel(t_ref, c_ref, s_ref, d_ref, a_ref, mod_ref, npost_ref, nffn_ref, ws_ref, wd_ref, wa_ref,
                    rwt_ref, rb_ref, tn_ref, hf_ref, aff_ref):
    mod = mod_ref[0, 0]
    t_in = _stream_tile(t_ref, c_ref)
    rwt = rwt_ref[...]
    rw_hi = rwt.astype(BF16)
    rw_lo = (rwt - rw_hi.astype(F32)).astype(BF16)
    rw_both = jnp.concatenate([rw_hi, rw_lo], axis=0)
    half = TILE // OUTPROJ_SPLIT
    for r in range(OUTPROJ_SPLIT):
        rows = slice(half * r, half * (r + 1))
        m = (_dot(s_ref[0, rows, :], ws_ref[...])
             + lax.dot_general(d_ref[0, :, rows], wd_ref[...], TN_DIMS, preferred_element_type=F32)
             + lax.dot_general(a_ref[0, :, rows], wa_ref[...], TN_DIMS, preferred_element_type=F32))
        tn = t_in[rows] + mod[2:3] * _rms(m, npost_ref[...])
        tn_ref[0, rows, :] = tn
        hf = _rms(tn, nffn_ref[...]) * (1.0 + mod[4:5]) + mod[3:4]
        hf_hi = hf.astype(BF16)
        hf_ref[0, rows, :] = hf_hi
        hf_lo = (hf - hf_hi.astype(F32)).astype(BF16)
        both = lax.dot_general(rw_both, hf_hi, NT_DIMS, preferred_element_type=F32)
        logits = (both[:N_EXPERTS] + both[N_EXPERTS:]
                  + lax.dot_general(rw_hi, hf_lo, NT_DIMS, preferred_element_type=F32) + rb_ref[...])
        e = jnp.exp(logits - jnp.max(logits, axis=0, keepdims=True))
        aff_ref[0, :, rows] = e / jnp.sum(e, axis=0, keepdims=True)


def _outproj(t, s, dt_, at_, mod, lw):
    streams, stream_specs, (b, nt, d) = _token_stream(t)
    nti = nt // TILE
    full = lambda a: pl.BlockSpec(a.shape, lambda bi, i: (0,) * a.ndim)
    ws = [lw["norm_mix_post"], lw["norm_ffn_pre"], lw["wo_s"], lw["wo_d"], lw["wo_a"], lw["rwt"], lw["rb"]]
    return pl.pallas_call(
        _outproj_kernel,
        grid=(b, nti),
        in_specs=stream_specs + [
                  pl.BlockSpec((1, TILE, SSD_INNER), lambda bi, i: (bi, i, 0)),
                  pl.BlockSpec((1, DIFF_HEADS * DIFF_V, TILE), lambda bi, i: (bi, 0, i)),
                  pl.BlockSpec((1, MLA_HEADS * MLA_V, TILE), lambda bi, i: (bi, 0, i)),
                  pl.BlockSpec((1, 1, N_MOD, d), lambda bi, i: (bi, jnp.minimum(i, 1), 0, 0))]
                 + [full(a) for a in ws],
        out_specs=[pl.BlockSpec((1, TILE, d), lambda bi, i: (bi, i, 0)),
                   pl.BlockSpec((1, TILE, d), lambda bi, i: (bi, i, 0)),
                   pl.BlockSpec((1, N_EXPERTS, TILE), lambda bi, i: (bi, 0, i))],
        out_shape=[jax.ShapeDtypeStruct((b, nt, d), F32),
                   jax.ShapeDtypeStruct((b, nt, d), BF16),
                   jax.ShapeDtypeStruct((b, N_EXPERTS, nt), F32)],
        compiler_params=_cparams(("arbitrary", "arbitrary"), VMEM_LIMIT),
        name="outproj",
    )(*streams, s, dt_, at_, mod, *ws)


def _route_kernel(aff_ref, pos_ref, gate_ref, cum_ref, *, nti, caps):
    ne = N_EXPERTS
    tri = (lax.broadcasted_iota(I32, (TILE, TILE), 0) < lax.broadcasted_iota(I32, (TILE, TILE), 1)).astype(BF16)
    lane = lax.broadcasted_iota(I32, (ne, LANES), 1)

    def excl_prefix(mask_f):
        return _dot(mask_f.astype(BF16), tri)

    cum_vec = jnp.zeros((ne, LANES), F32)
    total = jnp.zeros((ne, 1), F32)
    seg_bounds = ((0, 1, caps[0]), (1, nti, caps[1]))
    for t0, t1, cap in seg_bounds:
        xi = aff_ref[0, :, t0 * TILE:t1 * TILE]

        def bit_step(j, thr_bits, xi=xi, cap=cap):
            cand = thr_bits | (1 << (29 - j))
            cnt = jnp.sum((xi >= pltpu.bitcast(cand, F32)).astype(F32), axis=1, keepdims=True)
            return jnp.where(cnt >= cap, cand, thr_bits)

        thr = pltpu.bitcast(lax.fori_loop(0, 30, bit_step, jnp.zeros((ne, 1), I32)), F32)
        need = cap - jnp.sum((xi > thr).astype(F32), axis=1, keepdims=True)
        eq_seen = jnp.zeros((ne, 1), F32)
        for t in range(t0, t1):
            lo = (t - t0) * TILE
            xt = xi[:, lo:lo + TILE]
            eq = (xt == thr).astype(F32)
            eq_rank = eq_seen + excl_prefix(eq)
            sel = jnp.where(xt > thr, 1.0, eq * (eq_rank < need).astype(F32))
            eq_seen = eq_seen + jnp.sum(eq, axis=1, keepdims=True)
            rank = total + excl_prefix(sel)
            pos_ref[0, :, t * TILE:(t + 1) * TILE] = jnp.where(sel > 0.0, rank, -1.0).astype(I32)
            gate_ref[0, :, t * TILE:(t + 1) * TILE] = sel * aff_ref[0, :, t * TILE:(t + 1) * TILE]
            cum_vec = jnp.where(lane == t, total, cum_vec)
            total = total + jnp.sum(sel, axis=1, keepdims=True)
    cum_vec = jnp.where(lane == nti, total, cum_vec)
    cum_ref[0] = cum_vec.astype(I32)


def _route(aff, caps):
    b, ne, nt = aff.shape
    nti = nt // TILE
    kern = functools.partial(_route_kernel, nti=nti, caps=caps)
    return pl.pallas_call(
        kern,
        grid=(b,),
        in_specs=[pl.BlockSpec((1, ne, nt), lambda bi: (bi, 0, 0))],
        out_specs=[pl.BlockSpec((1, ne, nt), lambda bi: (bi, 0, 0)),
                   pl.BlockSpec((1, ne, nt), lambda bi: (bi, 0, 0)),
                   pl.BlockSpec((1, ne, LANES), lambda bi: (bi, 0, 0))],
        out_shape=[jax.ShapeDtypeStruct((b, ne, nt), I32),
                   jax.ShapeDtypeStruct((b, ne, nt), F32),
                   jax.ShapeDtypeStruct((b, ne, LANES), I32)],
        compiler_params=_cparams(("arbitrary",)),
        name="route",
    )(aff)


WIN = 64
GROUP = 4


def _tile_windows(cum_ref, b, t, rows):
    los = []
    rounds = jnp.int32(1)
    for e in range(N_EXPERTS):
        base = (b * N_EXPERTS + e) * LANES
        lo = (cum_ref[base + t] // 16) * 16
        los.append(lo)
        rounds = jnp.maximum(rounds, (cum_ref[base + t + 1] - lo + WIN - 1) // WIN)
    return los, rounds


def _window_onehot(pos_row, lo, r, rows):
    want = lo + WIN * r
    w0 = pl.multiple_of(jnp.minimum(want, rows - WIN), 16)
    rowid = w0 + lax.broadcasted_iota(I32, (WIN, TILE), 0)
    return w0, jnp.logical_and(pos_row == rowid, rowid >= want).astype(F32)


def _gather_kernel(cum_ref, hf_ref, pos_ref, gate_ref, xg_ref, gc_ref, *, rows):
    b = pl.program_id(0)
    t = pl.program_id(1)

    @pl.when(t == 0)
    def _():
        xg_ref[...] = jnp.zeros_like(xg_ref)
        gc_ref[...] = jnp.zeros_like(gc_ref)

    los, rounds = _tile_windows(cum_ref, b, t, rows)

    def round_step(r, carry):
        w0s, hots = [], []
        for e in range(N_EXPERTS):
            w0, hot = _window_onehot(pos_ref[0, e:e + 1, :], los[e], r, rows)
            w0s.append(w0)
            hots.append(hot)
            gc_ref[0, e, pl.ds(w0, WIN), :] += jnp.sum(hot * gate_ref[0, e:e + 1, :], axis=1, keepdims=True)
        res = _dot(jnp.concatenate(hots, axis=0).astype(BF16), hf_ref[0])
        for e in range(N_EXPERTS):
            xg_ref[0, e, pl.ds(w0s[e], WIN), :] += res[WIN * e:WIN * (e + 1)].astype(BF16)
        return carry

    lax.fori_loop(0, rounds, round_step, 0)


def _gather(cum_flat, hf, pos, gate, rows):
    b, nt, d = hf.shape
    nti = nt // TILE
    ne = pos.shape[1]
    kern = functools.partial(_gather_kernel, rows=rows)
    grid_spec = pltpu.PrefetchScalarGridSpec(
        num_scalar_prefetch=1,
        grid=(b, nti),
        in_specs=[pl.BlockSpec((1, TILE, d), lambda bi, i, cum: (bi, i, 0)),
                  pl.BlockSpec((1, ne, TILE), lambda bi, i, cum: (bi, 0, i)),
                  pl.BlockSpec((1, ne, TILE), lambda bi, i, cum: (bi, 0, i))],
        out_specs=[pl.BlockSpec((1, ne, rows, d), lambda bi, i, cum: (bi, 0, 0, 0)),
                   pl.BlockSpec((1, ne, rows, 1), lambda bi, i, cum: (bi, 0, 0, 0))],
    )
    return pl.pallas_call(
        kern,
        grid_spec=grid_spec,
        out_shape=[jax.ShapeDtypeStruct((b, ne, rows, d), BF16), jax.ShapeDtypeStruct((b, ne, rows, 1), F32)],
        compiler_params=_cparams(("arbitrary", "arbitrary"), VMEM_LIMIT),
        name="gather",
    )(cum_flat, hf, pos, gate)


def _experts_kernel(xg_ref, gc_ref, wg_ref, wu_ref, wd_ref, y_ref, wgb_ref, wub_ref, wdb_ref):
    @pl.when(pl.program_id(1) == 0)
    def _():
        wgb_ref[...] = wg_ref[0, 0].astype(BF16)
        wub_ref[...] = wu_ref[0, 0].astype(BF16)
        wdb_ref[...] = wd_ref[0, 0].astype(BF16)

    xg = xg_ref[0, 0]
    hid = (_silu(_dot(xg, wgb_ref[...])) * _dot(xg, wub_ref[...])).astype(BF16)
    y_ref[0, 0] = (_dot(hid, wdb_ref[...]) * gc_ref[0, 0]).astype(BF16)


def _experts(xg, gc, lw, l):
    b, ne, rows, d = xg.shape
    ff = lw["w_gate"].shape[3]
    return pl.pallas_call(
        _experts_kernel,
        grid=(ne, b),
        in_specs=[pl.BlockSpec((1, 1, rows, d), lambda e, bi: (bi, e, 0, 0)),
                  pl.BlockSpec((1, 1, rows, 1), lambda e, bi: (bi, e, 0, 0)),
                  pl.BlockSpec((1, 1, d, ff), lambda e, bi: (l, e, 0, 0)),
                  pl.BlockSpec((1, 1, d, ff), lambda e, bi: (l, e, 0, 0)),
                  pl.BlockSpec((1, 1, ff, d), lambda e, bi: (l, e, 0, 0))],
        out_specs=pl.BlockSpec((1, 1, rows, d), lambda e, bi: (bi, e, 0, 0)),
        out_shape=jax.ShapeDtypeStruct((b, ne, rows, d), BF16),
        scratch_shapes=[pltpu.VMEM((d, ff), BF16), pltpu.VMEM((d, ff), BF16), pltpu.VMEM((ff, d), BF16)],
        compiler_params=_cparams(("arbitrary", "arbitrary"), VMEM_LIMIT),
        name="experts",
    )(xg, gc, lw["w_gate"], lw["w_up"], lw["w_down"])


def _combine_kernel(cum_ref, t_ref, y_ref, pos_ref, mod_ref, npost_ref, o_ref, f_ref, *, rows, latent_only):
    b = pl.program_id(0)
    t = pl.program_id(1)

    def run():
        los, rounds = _tile_windows(cum_ref, b, t, rows)

        def scatter_round(r):
            total = None
            for g in range(N_EXPERTS // GROUP):
                hots, wins = [], []
                for e in range(GROUP * g, GROUP * (g + 1)):
                    w0, hot = _window_onehot(pos_ref[0, e:e + 1, :], los[e], r, rows)
                    hots.append(hot)
                    wins.append(y_ref[0, e, pl.ds(w0, WIN), :])
                hot = jnp.concatenate(hots, axis=0).astype(BF16)
                part = lax.dot_general(hot, jnp.concatenate(wins, axis=0), TN_DIMS, preferred_element_type=F32)
                total = part if total is None else total + part
            return total

        f_ref[...] = scatter_round(0)

        def round_step(r, carry):
            f_ref[...] += scatter_round(r)
            return carry

        lax.fori_loop(1, rounds, round_step, 0)
        mod = mod_ref[0, 0]
        o_ref[0] = t_ref[0] + mod[5:6] * _rms(f_ref[...], npost_ref[...])

    if latent_only:
        pl.when(t > 0)(run)
    else:
        run()


def _combine(cum_flat, t, y, pos, mod, lw, latent_only):
    b, nt, d = t.shape
    nti = nt // TILE
    ne, rows = y.shape[1], y.shape[2]
    kern = functools.partial(_combine_kernel, rows=rows, latent_only=latent_only)
    if latent_only:
        out_rows, out_map = nt - TILE, lambda bi, i, cum: (bi, jnp.maximum(i - 1, 0), 0)
    else:
        out_rows, out_map = nt, lambda bi, i, cum: (bi, i, 0)
    grid_spec = pltpu.PrefetchScalarGridSpec(
        num_scalar_prefetch=1,
        grid=(b, nti),
        in_specs=[pl.BlockSpec((1, TILE, d), lambda bi, i, cum: (bi, i, 0)),
                  pl.BlockSpec((1, ne, rows, d), lambda bi, i, cum: (bi, 0, 0, 0)),
                  pl.BlockSpec((1, ne, TILE), lambda bi, i, cum: (bi, 0, i)),
                  pl.BlockSpec((1, 1, N_MOD, d), lambda bi, i, cum: (bi, jnp.minimum(i, 1), 0, 0)),
                  pl.BlockSpec((1, d), lambda bi, i, cum: (0, 0))],
        out_specs=pl.BlockSpec((1, TILE, d), out_map),
        scratch_shapes=[pltpu.VMEM((TILE, d), F32)],
    )
    return pl.pallas_call(
        kern,
        grid_spec=grid_spec,
        out_shape=jax.ShapeDtypeStruct((b, out_rows, d), F32),
        compiler_params=_cparams(("arbitrary", "arbitrary"), VMEM_LIMIT),
        name="combine",
    )(cum_flat, t, y, pos, mod, lw["norm_ffn_post"])


def _rope_tables(seq, ctx):
    quarter = MLA_ROPE // 4
    inv = ROPE_BASE ** (-jnp.arange(quarter, dtype=F32) / quarter)
    n_rows = seq // GRID_W
    rows = jnp.repeat(jnp.arange(n_rows, dtype=F32), GRID_W)
    cols = jnp.tile(jnp.arange(GRID_W, dtype=F32), n_rows)
    ar = rows[:, None] * inv[None, :]
    ac = cols[:, None] * inv[None, :]
    cos = jnp.concatenate([jnp.cos(ar), jnp.cos(ar), jnp.cos(ac), jnp.cos(ac)], axis=1)
    sin = jnp.concatenate([-jnp.sin(ar), jnp.sin(ar), -jnp.sin(ac), jnp.sin(ac)], axis=1)
    cos = jnp.concatenate([jnp.ones((ctx, MLA_ROPE), F32), cos], axis=0)
    sin = jnp.concatenate([jnp.zeros((ctx, MLA_ROPE), F32), sin], axis=0)
    return {"ck": jnp.tile(cos, (1, DIFF_MAPS)), "sk": jnp.tile(sin, (1, DIFF_MAPS)), "ct": cos.T, "st": sin.T}


def _partner_perm(width):
    idx = jnp.arange(width)
    r = idx % 16
    return jnp.where(r < 8, idx + 8, idx - 8)


def _layer_weights(l, p):
    d = p["w_in"].shape[1]
    w_in = p["w_in"][l]
    o_diff = 2 * SSD_INNER + 2 * SSD_BC + 2 * SSD_HEADS
    o_mla = o_diff + 3 * DIFF_HEADS * DIFF_V
    nk = DIFF_MAPS * DIFF_QK
    w_ssd = w_in[:, :o_diff]
    wa = jnp.concatenate([w_ssd, jnp.zeros((d, LANES - 2 * SSD_HEADS), F32)], axis=1)
    wq = w_in[:, o_diff:o_diff + nk]
    wk = w_in[:, o_diff + nk:o_diff + 2 * nk]
    wv = w_in[:, o_diff + 2 * nk:o_mla]
    perm = _partner_perm(nk)
    wcq = w_in[:, o_mla:o_mla + MLA_Q_LORA]
    wckv = w_in[:, o_mla + MLA_Q_LORA:o_mla + MLA_Q_LORA + MLA_KV_LORA]
    wkr = w_in[:, o_mla + MLA_Q_LORA + MLA_KV_LORA:]
    zeros = lambda n: jnp.zeros((d, n), F32)
    wm = jnp.concatenate([wcq, zeros(256 - MLA_Q_LORA), wckv, wkr, wkr[:, _partner_perm(MLA_ROPE)],
                          zeros(512 - 448)], axis=1)

    wqu = p["mla_w_q_up"][l].reshape(MLA_Q_LORA, MLA_HEADS, MLA_NOPE + MLA_ROPE)
    rope_part = wqu[:, :, MLA_NOPE:]
    pad = jnp.zeros((MLA_Q_LORA, MLA_HEADS, MLA_QK_PAD - MLA_NOPE - MLA_ROPE), F32)
    wqu_plain = jnp.concatenate([wqu, pad], axis=2).reshape(MLA_Q_LORA, -1)
    wqu_rot = jnp.concatenate([jnp.zeros_like(wqu[:, :, :MLA_NOPE]), rope_part[:, :, _partner_perm(MLA_ROPE)], pad],
                              axis=2).reshape(MLA_Q_LORA, -1)
    wkvu = p["mla_w_kv_up"][l].reshape(MLA_KV_LORA, MLA_HEADS, MLA_NOPE + MLA_V)
    wk2 = jnp.concatenate([wkvu[:, :, :MLA_NOPE],
                           jnp.zeros((MLA_KV_LORA, MLA_HEADS, MLA_QK_PAD - MLA_NOPE), F32)],
                          axis=2).reshape(MLA_KV_LORA, -1)
    eye = jnp.eye(MLA_ROPE, dtype=F32)
    ek_h = jnp.concatenate([jnp.zeros((MLA_ROPE, MLA_NOPE), F32), eye,
                            jnp.zeros((MLA_ROPE, MLA_QK_PAD - MLA_NOPE - MLA_ROPE), F32)], axis=1)
    ek = jnp.tile(ek_h, (1, MLA_HEADS))
    wv2 = wkvu[:, :, MLA_NOPE:].reshape(MLA_KV_LORA, -1)

    w_out = p["w_out"][l]
    row = lambda a: a.reshape(1, -1)
    col = lambda a: a.reshape(-1, 1)
    return {
        "norm_mix_pre": row(p["norm_mix_pre"][l]), "norm_mix_post": row(p["norm_mix_post"][l]),
        "norm_ffn_pre": row(p["norm_ffn_pre"][l]), "norm_ffn_post": row(p["norm_ffn_post"][l]),
        "wa": wa.astype(BF16),
        "wdk": jnp.concatenate([wk, wk[:, perm]], axis=1).astype(BF16),
        "wm": wm.astype(BF16),
        "wqt": jnp.concatenate([wq, wq[:, perm]], axis=1).T.astype(BF16),
        "wvt": wv.T.astype(BF16),
        "qnw": row(p["mla_q_norm"][l]), "kvnw": row(p["mla_kv_norm"][l]),
        "wqut": jnp.concatenate([wqu_plain, wqu_rot], axis=1).T.astype(BF16),
        "wk2": wk2.astype(BF16), "ek": ek.astype(BF16), "wvt2": wv2.T.astype(BF16),
        "conv_w": p["ssd_conv_w"][l], "conv_b": row(p["ssd_conv_b"][l]),
        "alog": col(p["ssd_a_log"][l]), "dtb": col(p["ssd_dt_bias"][l]),
        "dsk": row(p["ssd_d"][l]), "ssd_norm": row(p["ssd_norm"][l]),
        "lq1": row(p["diff_lam_q1"][l]), "lk1": row(p["diff_lam_k1"][l]),
        "lq2": row(p["diff_lam_q2"][l]), "lk2": row(p["diff_lam_k2"][l]),
        "subw": p["diff_subln"][l].reshape(-1, 1),
        "wo_s": w_out[:SSD_INNER].astype(BF16),
        "wo_d": w_out[SSD_INNER:SSD_INNER + DIFF_HEADS * DIFF_V].astype(BF16),
        "wo_a": w_out[SSD_INNER + DIFF_HEADS * DIFF_V:].astype(BF16),
        "rwt": p["router_w"][l].T, "rb": p["router_b"][l].reshape(-1, 1),
        "w_gate": p["w_gate"], "w_up": p["w_up"], "w_down": p["w_down"],
    }


def kernel(x, c, ctx, c_ctx, ada_w, ada_b, norm_mix_pre, norm_mix_post, norm_ffn_pre, norm_ffn_post, w_in, ssd_conv_w, ssd_conv_b, ssd_a_log, ssd_dt_bias, ssd_d, ssd_norm, diff_lam_q1, diff_lam_k1, diff_lam_q2, diff_lam_k2, diff_subln, mla_q_norm, mla_w_q_up, mla_kv_norm, mla_w_kv_up, w_out, router_w, router_b, w_gate, w_up, w_down):
    p = dict(norm_mix_pre=norm_mix_pre, norm_mix_post=norm_mix_post, norm_ffn_pre=norm_ffn_pre,
             norm_ffn_post=norm_ffn_post, w_in=w_in, ssd_conv_w=ssd_conv_w, ssd_conv_b=ssd_conv_b,
             ssd_a_log=ssd_a_log, ssd_dt_bias=ssd_dt_bias, ssd_d=ssd_d, ssd_norm=ssd_norm,
             diff_lam_q1=diff_lam_q1, diff_lam_k1=diff_lam_k1, diff_lam_q2=diff_lam_q2, diff_lam_k2=diff_lam_k2,
             diff_subln=diff_subln, mla_q_norm=mla_q_norm, mla_w_q_up=mla_w_q_up, mla_kv_norm=mla_kv_norm,
             mla_w_kv_up=mla_w_kv_up, w_out=w_out, router_w=router_w, router_b=router_b,
             w_gate=w_gate, w_up=w_up, w_down=w_down)
    b, seq, d = x.shape
    nctx = ctx.shape[1]
    depth = ada_w.shape[0]
    assert nctx == TILE and seq % TILE == 0 and seq % GRID_W == 0
    nt = nctx + seq
    caps = (EC_CAPACITY * nctx // N_EXPERTS, EC_CAPACITY * seq // N_EXPERTS)
    assert caps[0] % 16 == 0 and caps[1] % 16 == 0 and caps[0] + caps[1] >= WIN

    cvec = jnp.concatenate([c, c_ctx[None, :], jnp.zeros((8 - b - 1, d), F32)], axis=0)
    mods = _adaln(cvec, ada_w, ada_b).reshape(depth, 8, N_MOD, d)
    tabs = _rope_tables(seq, nctx)
    t = (x, ctx)
    for l in range(depth):
        lw = _layer_weights(l, p)
        lambda_init = 0.8 - 0.6 * math.exp(-0.3 * l)
        mod = jnp.stack([jnp.broadcast_to(mods[l, b], (b, N_MOD, d)), mods[l, :b]], axis=1)
        z, xbc, dt, dq, dk, dv, mq, mk, mv = _inproj(t, mod, lw, tabs)
        s = _ssd(z, xbc, dt, lw)
        da = _diff_attn(dq, dk, dv, lw, lambda_init).reshape(b, DIFF_HEADS * DIFF_V, nt)
        aa = _mla_attn(mq, mk, mv).reshape(b, MLA_HEADS * MLA_V, nt)
        t, hf, aff = _outproj(t, s, da, aa, mod, lw)
        pos, gate, cum = _route(aff, caps)
        cum_flat = cum.reshape(-1)
        xg, gc = _gather(cum_flat, hf, pos, gate, caps[0] + caps[1])
        y = _experts(xg, gc, lw, l)
        t = _combine(cum_flat, t, y, pos, mod, lw, latent_only=(l == depth - 1))
    return t
```

```python
import functools
import math

import jax
import jax.numpy as jnp
from jax import lax
from jax.experimental import pallas as pl
from jax.experimental.pallas import tpu as pltpu

F32 = jnp.float32
BF16 = jnp.bfloat16
I32 = jnp.int32
HIGHEST = lax.Precision.HIGHEST

EPS = 1e-6
GRID_W = 64
ROPE_BASE = 10000.0
N_MOD = 6

SSD_HEADS = 8
SSD_HEAD_DIM = 64
SSD_INNER = SSD_HEADS * SSD_HEAD_DIM
SSD_GROUPS = 2
SSD_STATE = 64
SSD_CHUNK = 128
SSD_BC = SSD_GROUPS * SSD_STATE
SSD_CONV_DIM = SSD_INNER + 2 * SSD_BC
SSD_PAIRS = SSD_HEADS // 2

DIFF_HEADS = 4
DIFF_QK = 32
DIFF_V = 64
DIFF_MAPS = 2 * DIFF_HEADS

MLA_HEADS = 4
MLA_Q_LORA = 192
MLA_KV_LORA = 128
MLA_NOPE = 64
MLA_ROPE = 32
MLA_V = 64
MLA_QK_PAD = 128

N_EXPERTS = 16
EC_CAPACITY = 2

TILE = 256
LANES = 128
MAPS_PER_TILE = LANES // DIFF_QK
VMEM_LIMIT = 56 * 1024 * 1024

NT_DIMS = (((1,), (1,)), ((), ()))
TN_DIMS = (((0,), (0,)), ((), ()))


def _cparams(sem, vmem=None):
    return pltpu.CompilerParams(dimension_semantics=sem, vmem_limit_bytes=vmem)


def _rms(x, w):
    return x * lax.rsqrt(jnp.mean(x * x, axis=-1, keepdims=True) + EPS) * w


def _silu(x):
    return x * jax.nn.sigmoid(x)


def _dot(a, b):
    return jnp.dot(a, b, preferred_element_type=F32)


def _adaln_kernel(c_ref, w_ref, b_ref, o_ref):
    s = _silu(c_ref[...])
    o_ref[0] = lax.dot_general(s, w_ref[0], (((1,), (0,)), ((), ())), precision=HIGHEST,
                               preferred_element_type=F32) + b_ref[0]


def _adaln(cvec, ada_w, ada_b):
    depth, d, nd = ada_w.shape
    rows = cvec.shape[0]
    return pl.pallas_call(
        _adaln_kernel,
        grid=(depth, nd // d),
        in_specs=[pl.BlockSpec((rows, d), lambda l, j: (0, 0)),
                  pl.BlockSpec((1, d, d), lambda l, j: (l, 0, j)),
                  pl.BlockSpec((1, 1, d), lambda l, j: (l, 0, j))],
        out_specs=pl.BlockSpec((1, rows, d), lambda l, j: (l, 0, j)),
        out_shape=jax.ShapeDtypeStruct((depth, rows, nd), F32),
        compiler_params=_cparams(("arbitrary", "arbitrary")),
        name="adaln",
    )(cvec, ada_w, ada_b.reshape(depth, 1, nd))


def _token_stream(t):
    if isinstance(t, tuple):
        x, ctx = t
        b, seq, d = x.shape
        specs = [pl.BlockSpec((1, TILE, d), lambda bi, i: (bi, jnp.maximum(i - 1, 0), 0)),
                 pl.BlockSpec((1, TILE, d), lambda bi, i: (bi, 0, 0))]
        return (x, ctx), specs, (b, seq + ctx.shape[1], d)
    b, nt, d = t.shape
    specs = [pl.BlockSpec((1, TILE, d), lambda bi, i: (bi, i, 0)),
             pl.BlockSpec((1, TILE, d), lambda bi, i: (bi, 0, 0))]
    return (t, t), specs, (b, nt, d)


def _stream_tile(x_ref, c_ref):
    return jnp.where(pl.program_id(1) == 0, c_ref[0], x_ref[0])


def _partner_rows(x):
    parts = []
    for g in range(0, x.shape[0], 16):
        parts += [x[g + 8:g + 16], x[g:g + 8]]
    return jnp.concatenate(parts, axis=0)


def _partner_lanes(x):
    width = x.shape[1]
    lane = lax.broadcasted_iota(I32, x.shape, 1)
    return jnp.where((lane & 8) == 0, pltpu.roll(x, width - 8, 1), pltpu.roll(x, 8, 1))


def _inproj_kernel(x_ref, c_ref, mod_ref, nw_ref, wa_ref, wdk_ref, wm_ref, wqt_ref, wvt_ref,
                   ck_ref, sk_ref, ct_ref, st_ref, qnw_ref, kvnw_ref, wqut_ref, wk2_ref, ek_ref, wvt2_ref,
                   z_ref, xbc_ref, dt_ref, dq_ref, dk_ref, dv_ref, mq_ref, mk_ref, mv_ref):
    x = _stream_tile(x_ref, c_ref)
    mod = mod_ref[0, 0]
    h = (_rms(x, nw_ref[...]) * (1.0 + mod[1:2]) + mod[0:1]).astype(BF16)

    ra = _dot(h, wa_ref[...])
    z_ref[0] = ra[:, :SSD_INNER].astype(BF16)
    xbc_ref[0] = ra[:, SSD_INNER:SSD_INNER + SSD_CONV_DIM]
    dt_ref[0] = ra[:, SSD_INNER + SSD_CONV_DIM:]

    rk = _dot(h, wdk_ref[...])
    k = (rk * ck_ref[...] + _partner_lanes(rk) * sk_ref[...]).astype(BF16)
    for g in range(DIFF_MAPS // MAPS_PER_TILE):
        dk_ref[0, g, 0] = k[:, LANES * g:LANES * (g + 1)]

    ct = ct_ref[...]
    st = st_ref[...]
    rq = lax.dot_general(wqt_ref[...], h, NT_DIMS, preferred_element_type=F32)
    rq_partner = _partner_rows(rq)
    for m in range(DIFF_MAPS):
        lo = DIFF_QK * m
        qm = ((rq[lo:lo + DIFF_QK] * ct + rq_partner[lo:lo + DIFF_QK] * st) * DIFF_C_EXP).astype(BF16)
        above = DIFF_QK * (m % MAPS_PER_TILE)
        below = LANES - above - DIFF_QK
        parts = ([jnp.zeros((above, qm.shape[1]), BF16)] if above else []) + [qm]
        parts += [jnp.zeros((below, qm.shape[1]), BF16)] if below else []
        dq_ref[0, m] = jnp.concatenate(parts, axis=0)

    rv = lax.dot_general(wvt_ref[...], h, NT_DIMS, preferred_element_type=F32).astype(BF16)
    for hd in range(DIFF_HEADS):
        dv_ref[0, hd, 0] = rv[DIFF_V * hd:DIFF_V * (hd + 1)]

    rm = _dot(h, wm_ref[...])
    cq = _rms(rm[:, :MLA_Q_LORA], qnw_ref[...]).astype(BF16)
    ckv = _rms(rm[:, 256:256 + MLA_KV_LORA], kvnw_ref[...]).astype(BF16)
    kr = rm[:, 384:384 + MLA_ROPE] * ck_ref[:, :MLA_ROPE] + rm[:, 416:416 + MLA_ROPE] * sk_ref[:, :MLA_ROPE]

    rq2 = lax.dot_general(wqut_ref[...], cq, NT_DIMS, preferred_element_type=F32)
    rq2_partner = _partner_rows(rq2)
    ones = jnp.ones((MLA_NOPE, ct.shape[1]), F32)
    pad1 = jnp.ones((MLA_QK_PAD - MLA_NOPE - MLA_ROPE, ct.shape[1]), F32)
    ct_h = jnp.concatenate([ones, ct, pad1], axis=0)
    st_h = jnp.concatenate([0.0 * ones, st, 0.0 * pad1], axis=0)
    for hd in range(MLA_HEADS):
        lo = MLA_QK_PAD * hd
        qh = rq2[lo:lo + MLA_QK_PAD] * ct_h + rq2_partner[lo:lo + MLA_QK_PAD] * st_h
        mq_ref[0, hd] = (qh * MLA_C_EXP).astype(BF16)

    k2 = (_dot(ckv, wk2_ref[...]) + _dot(kr.astype(BF16), ek_ref[...])).astype(BF16)
    for hd in range(MLA_HEADS):
        mk_ref[0, hd, 0] = k2[:, MLA_QK_PAD * hd:MLA_QK_PAD * (hd + 1)]
    rv2 = lax.dot_general(wvt2_ref[...], ckv, NT_DIMS, preferred_element_type=F32).astype(BF16)
    for hd in range(MLA_HEADS):
        mv_ref[0, hd, 0] = rv2[MLA_V * hd:MLA_V * (hd + 1)]


def _inproj(t, mod, lw, tabs):
    streams, stream_specs, (b, nt, d) = _token_stream(t)
    nti = nt // TILE
    full = lambda a: pl.BlockSpec(a.shape, lambda bi, i: (0,) * a.ndim)
    tok = lambda w: pl.BlockSpec((TILE, w), lambda bi, i: (i, 0))
    tokt = lambda w: pl.BlockSpec((w, TILE), lambda bi, i: (0, i))
    ws = [lw["norm_mix_pre"], lw["wa"], lw["wdk"], lw["wm"], lw["wqt"], lw["wvt"]]
    ws2 = [lw["qnw"], lw["kvnw"], lw["wqut"], lw["wk2"], lw["ek"], lw["wvt2"]]
    out_shape = [
        jax.ShapeDtypeStruct((b, nt, SSD_INNER), BF16),
        jax.ShapeDtypeStruct((b, nt, SSD_CONV_DIM), F32),
        jax.ShapeDtypeStruct((b, nt, LANES), F32),
        jax.ShapeDtypeStruct((b, DIFF_MAPS, LANES, nt), BF16),
        jax.ShapeDtypeStruct((b, DIFF_MAPS // MAPS_PER_TILE, nti, TILE, LANES), BF16),
        jax.ShapeDtypeStruct((b, DIFF_HEADS, nti, DIFF_V, TILE), BF16),
        jax.ShapeDtypeStruct((b, MLA_HEADS, MLA_QK_PAD, nt), BF16),
        jax.ShapeDtypeStruct((b, MLA_HEADS, nti, TILE, MLA_QK_PAD), BF16),
        jax.ShapeDtypeStruct((b, MLA_HEADS, nti, MLA_V, TILE), BF16),
    ]
    out_specs = [
        pl.BlockSpec((1, TILE, SSD_INNER), lambda bi, i: (bi, i, 0)),
        pl.BlockSpec((1, TILE, SSD_CONV_DIM), lambda bi, i: (bi, i, 0)),
        pl.BlockSpec((1, TILE, LANES), lambda bi, i: (bi, i, 0)),
        pl.BlockSpec((1, DIFF_MAPS, LANES, TILE), lambda bi, i: (bi, 0, 0, i)),
        pl.BlockSpec((1, DIFF_MAPS // MAPS_PER_TILE, 1, TILE, LANES), lambda bi, i: (bi, 0, i, 0, 0)),
        pl.BlockSpec((1, DIFF_HEADS, 1, DIFF_V, TILE), lambda bi, i: (bi, 0, i, 0, 0)),
        pl.BlockSpec((1, MLA_HEADS, MLA_QK_PAD, TILE), lambda bi, i: (bi, 0, 0, i)),
        pl.BlockSpec((1, MLA_HEADS, 1, TILE, MLA_QK_PAD), lambda bi, i: (bi, 0, i, 0, 0)),
        pl.BlockSpec((1, MLA_HEADS, 1, MLA_V, TILE), lambda bi, i: (bi, 0, i, 0, 0)),
    ]
    in_specs = (stream_specs
                + [pl.BlockSpec((1, 1, N_MOD, d), lambda bi, i: (bi, jnp.minimum(i, 1), 0, 0))]
                + [full(a) for a in ws]
                + [tok(DIFF_MAPS * DIFF_QK), tok(DIFF_MAPS * DIFF_QK), tokt(DIFF_QK), tokt(DIFF_QK)]
                + [full(a) for a in ws2])
    return pl.pallas_call(
        _inproj_kernel,
        grid=(b, nti),
        in_specs=in_specs,
        out_specs=out_specs,
        out_shape=out_shape,
        compiler_params=_cparams(("arbitrary", "arbitrary"), VMEM_LIMIT),
        name="inproj",
    )(*streams, mod, *ws, tabs["ck"], tabs["sk"], tabs["ct"], tabs["st"], *ws2)


def _ssd_chunk_of(ph, i, nck, nctx):
    back = jnp.where(i < nctx, nctx - 1 - i, nck - 1 + nctx - i)
    return jnp.where(ph == 0, back, i)


def _ssd_kernel(xc_ref, xp_ref, xn_ref, dt_ref, z_ref, cw_ref, cb_ref, alog_ref, dtb_ref, dsk_ref, nw_ref,
                o_ref, s_ref, sb_ref, *cache, nck, nctx, nb):
    ph = pl.program_id(1)
    i = pl.program_id(2)
    c = _ssd_chunk_of(ph, i, nck, nctx)

    @pl.when(i == 0)
    def _():
        s_ref[...] = jnp.zeros_like(s_ref)

    one = lambda ref, bb: ref.at[pl.ds(bb, 1)]
    fns = [_ssd_sample(one(xc_ref, bb), one(xp_ref, bb), one(xn_ref, bb), one(dt_ref, bb), one(z_ref, bb),
                       cw_ref, cb_ref, alog_ref, dtb_ref, dsk_ref, nw_ref, one(o_ref, bb),
                       s_ref.at[bb], sb_ref.at[bb], [r.at[bb] for r in cache], c, nck=nck, nctx=nctx)
           for bb in range(nb)]

    @pl.when(ph == 0)
    def _():
        for backward_states, _ in fns:
            backward_states()

    @pl.when(ph == 1)
    def _():
        for _, forward_and_output in fns:
            forward_and_output()


def _ssd_sample(xc_ref, xp_ref, xn_ref, dt_ref, z_ref, cw_ref, cb_ref, alog_ref, dtb_ref, dsk_ref, nw_ref,
                o_ref, s_ref, sb_ref, cache, c, *, nck, nctx):
    q = SSD_CHUNK
    nh2 = 2 * SSD_HEADS
    cx_ref, cbc_ref, crow_ref, ccol_ref, cbt_ref = cache
    ri = lax.broadcasted_iota(I32, (q, q), 0)
    ci = lax.broadcasted_iota(I32, (q, q), 1)
    lower = ci <= ri
    upper = ci >= ri
    lane = ci
    first_half_s = ri < SSD_STATE
    first_half_l = lane < SSD_HEAD_DIM
    blockdiag = first_half_s == first_half_l

    def chunk_values():
        x = xc_ref[0]
        has_prev = jnp.logical_and(c != 0, c != nctx)
        has_next = jnp.logical_and(c != nctx - 1, c != nck - 1)
        prev_row = jnp.where(has_prev, xp_ref[0][7:8, :], 0.0)
        next_row = jnp.where(has_next, xn_ref[0][0:1, :], 0.0)
        row = lax.broadcasted_iota(I32, x.shape, 0)
        xm1 = jnp.where(row == 0, prev_row, pltpu.roll(x, 1, 0))
        xp1 = jnp.where(row == q - 1, next_row, pltpu.roll(x, q - 1, 0))
        cw = cw_ref[...]
        u = _silu(xm1 * cw[0:1] + x * cw[1:2] + xp1 * cw[2:3] + cb_ref[...])
        xs = u[:, :SSD_INNER].astype(BF16)
        bc = u[:, SSD_INNER:]
        xdt = dt_ref[0].T[:nh2] + dtb_ref[...]
        dtt = jnp.maximum(xdt, 0.0) + jnp.log1p(jnp.exp(-jnp.abs(xdt)))
        dat = dtt * (-jnp.exp(alog_ref[...]))
        tri_dims = (((1,), (0,)), ((), ()))
        acf = lax.dot_general(dat, upper.astype(F32), tri_dims, precision=HIGHEST, preferred_element_type=F32)
        acb = lax.dot_general(dat, lower.astype(F32), tri_dims, precision=HIGHEST, preferred_element_type=F32)
        act = jnp.where(ri[:nh2] < SSD_HEADS, acf, acb)
        rows = jnp.concatenate([dtt, act], axis=0)
        cols = jnp.concatenate([rows, jnp.zeros((q - 2 * nh2, q), F32)], axis=0).T
        bt = bc[:, :SSD_BC].T
        return xs, bc, rows, cols, bt

    def bodies(xs, bc, rows, cols, bt):
        bm = bc[:, :SSD_BC]
        cm = bc[:, SSD_BC:]
        dtt = rows[:nh2]
        act = rows[nh2:]
        dtc = cols
        acc = pltpu.roll(cols, LANES - nh2, 1)

        def pair_vals(arr_c, arr_t, h0):
            col = jnp.where(first_half_l, arr_c[:, h0:h0 + 1], arr_c[:, h0 + 1:h0 + 2])
            rowv = jnp.where(first_half_s, arr_t[h0:h0 + 1, :], arr_t[h0 + 1:h0 + 2, :])
            return col, rowv

        def state_update(p, d):
            g = (2 * p) // (SSD_HEADS // SSD_GROUPS)
            h0 = d * SSD_HEADS + 2 * p
            edge = q - 1 if d == 0 else 0
            alast_row = jnp.where(first_half_s[:, 0:1], acc[edge:edge + 1, h0:h0 + 1],
                                  acc[edge:edge + 1, h0 + 1:h0 + 2])
            _, ar = pair_vals(acc, act, h0)
            _, dr = pair_vals(dtc, dtt, h0)
            w = jnp.exp(alast_row - ar) * dr
            btg = bt[SSD_STATE * g:SSD_STATE * (g + 1)]
            lhs = (jnp.concatenate([btg, btg], axis=0) * w).astype(BF16)
            xs2 = xs[:, 2 * SSD_HEAD_DIM * p:2 * SSD_HEAD_DIM * (p + 1)]
            upd = jnp.where(blockdiag, _dot(lhs, xs2), 0.0)
            return jnp.exp(alast_row) * s_ref[d, p] + upd

        def backward():
            for p in range(SSD_PAIRS):
                sb_ref[c, p] = s_ref[1, p].astype(BF16)
                s_ref[1, p] = state_update(p, 1)

        def forward():
            roll_c = pltpu.roll(cm, SSD_STATE, 1)
            dsk = dsk_ref[...]
            ys = []
            for p in range(SSD_PAIRS):
                g = (2 * p) // (SSD_HEADS // SSD_GROUPS)
                cg_only = jnp.where((lane < SSD_STATE) == (g == 0), cm, 0.0).astype(BF16)
                cb = lax.dot_general(cg_only, bm.astype(BF16), NT_DIMS, preferred_element_type=F32)
                ms = []
                for hh in range(2):
                    hf = 2 * p + hh
                    hb = SSD_HEADS + hf
                    lf = (jnp.exp(jnp.where(lower, acc[:, hf:hf + 1] - act[hf:hf + 1, :], -jnp.inf))
                          * dtt[hf:hf + 1, :])
                    lb = (jnp.exp(jnp.where(upper, acc[:, hb:hb + 1] - act[hb:hb + 1, :], -jnp.inf))
                          * dtt[hb:hb + 1, :])
                    ms.append((cb * (lf + lb) + jnp.where(ri == ci, dsk[:, hf:hf + 1], 0.0)).astype(BF16))
                xs2 = xs[:, 2 * SSD_HEAD_DIM * p:2 * SSD_HEAD_DIM * (p + 1)]
                zero = jnp.zeros_like(xs2)
                rhs = jnp.concatenate([jnp.where(first_half_l, xs2, zero), jnp.where(first_half_l, zero, xs2)],
                                      axis=0)
                y = _dot(jnp.concatenate(ms, axis=1), rhs)
                cdup = jnp.where(first_half_l == (g == 0), cm, roll_c)
                ef, _ = pair_vals(acc, act, 2 * p)
                eb, _ = pair_vals(acc, act, SSD_HEADS + 2 * p)
                lhs_off = jnp.concatenate([cdup * jnp.exp(ef), cdup * jnp.exp(eb)], axis=1).astype(BF16)
                rhs_off = jnp.concatenate([s_ref[0, p].astype(BF16), sb_ref[c, p]], axis=0)
                ys.append(y + _dot(lhs_off, rhs_off))
                s_ref[0, p] = state_update(p, 0)
            y = jnp.concatenate(ys, axis=1)
            zf = z_ref[0].astype(F32)
            gt = y * _silu(zf)
            nw = nw_ref[...]
            gw = SSD_INNER // SSD_GROUPS
            outs = [_rms(gt[:, gw * g:gw * (g + 1)], nw[:, gw * g:gw * (g + 1)]) for g in range(SSD_GROUPS)]
            o_ref[0] = jnp.concatenate(outs, axis=1).astype(BF16)

        return backward, forward

    def backward_states():
        xs, bc, rows, cols, bt = chunk_values()
        cx_ref[c] = xs
        cbc_ref[c] = bc
        crow_ref[c] = rows
        ccol_ref[c] = cols
        cbt_ref[c] = bt
        bodies(xs, bc, rows, cols, bt)[0]()

    def forward_and_output():
        bodies(cx_ref[c], cbc_ref[c], crow_ref[c], ccol_ref[c], cbt_ref[c])[1]()

    return backward_states, forward_and_output


def _ssd(z, xbc, dt, lw):
    b, nt, _ = z.shape
    q = SSD_CHUNK
    nck = nt // q
    nctx = TILE // q
    rows8 = q // 8
    nb = 1
    chunk = functools.partial(_ssd_chunk_of, nck=nck, nctx=nctx)
    full = lambda a: pl.BlockSpec(a.shape, lambda bi, ph, i: (0,) * a.ndim)
    ws = [lw["conv_w"], lw["conv_b"], lw["alog"], lw["dtb"], lw["dsk"], lw["ssd_norm"]]
    kern = functools.partial(_ssd_kernel, nck=nck, nctx=nctx, nb=nb)
    return pl.pallas_call(
        kern,
        grid=(b // nb, 2, nck),
        in_specs=[pl.BlockSpec((nb, q, SSD_CONV_DIM), lambda bi, ph, i: (bi, chunk(ph, i), 0)),
                  pl.BlockSpec((nb, 8, SSD_CONV_DIM),
                               lambda bi, ph, i: (bi, jnp.maximum(chunk(ph, i) * rows8 - 1, 0), 0)),
                  pl.BlockSpec((nb, 8, SSD_CONV_DIM),
                               lambda bi, ph, i: (bi, jnp.minimum((chunk(ph, i) + 1) * rows8, nck * rows8 - 1), 0)),
                  pl.BlockSpec((nb, q, LANES), lambda bi, ph, i: (bi, chunk(ph, i), 0)),
                  pl.BlockSpec((nb, q, SSD_INNER), lambda bi, ph, i: (bi, chunk(ph, i), 0))]
                 + [full(a) for a in ws],
        out_specs=pl.BlockSpec((nb, q, SSD_INNER), lambda bi, ph, i: (bi, jnp.where(ph == 0, 0, i), 0)),
        out_shape=jax.ShapeDtypeStruct((b, nt, SSD_INNER), BF16),
        scratch_shapes=[pltpu.VMEM((nb, 2, SSD_PAIRS, 2 * SSD_STATE, 2 * SSD_HEAD_DIM), F32),
                        pltpu.VMEM((nb, nck, SSD_PAIRS, 2 * SSD_STATE, 2 * SSD_HEAD_DIM), BF16),
                        pltpu.VMEM((nb, nck, q, SSD_INNER), BF16),
                        pltpu.VMEM((nb, nck, q, 2 * SSD_BC), F32),
                        pltpu.VMEM((nb, nck, 4 * SSD_HEADS, q), F32),
                        pltpu.VMEM((nb, nck, q, LANES), F32),
                        pltpu.VMEM((nb, nck, SSD_BC, q), F32)],
        compiler_params=_cparams(("arbitrary", "arbitrary", "arbitrary"), VMEM_LIMIT),
        name="ssd",
    )(xbc, xbc, xbc, dt, z, *ws)


ACC_ROWS = 80
DIFF_C_EXP = (DIFF_QK ** -0.5) * math.log2(math.e)
MLA_C_EXP = ((MLA_NOPE + MLA_ROPE) ** -0.5) * math.log2(math.e)


def _ones_rows(tk):
    return (lax.broadcasted_iota(I32, (ACC_ROWS - DIFF_V, tk), 0) == 0).astype(BF16)


def _score_step(kqs, s_ref):
    for idx, (k, q) in enumerate(kqs):
        s_ref[idx, 0:k.shape[0], :] = _dot(k, q)


def _softmax_pv_step(n_keys, vaugs, s_ref, m_ref, acc_ref):
    for idx in range(len(vaugs)):
        s = s_ref[idx, 0:n_keys, :]
        m = m_ref[idx]
        mn = jnp.maximum(m, jnp.max(s, axis=0, keepdims=True))
        p = jnp.exp2(s - mn).astype(BF16)
        acc_ref[idx] = acc_ref[idx] * jnp.exp2(m - mn) + _dot(vaugs[idx], p)
        m_ref[idx] = mn


def _attn_init(m_ref, acc_ref):
    m_ref[...] = jnp.full(m_ref.shape, -jnp.inf, F32)
    acc_ref[...] = jnp.zeros_like(acc_ref)


DIFF_KEY_GROUP = 2
MLA_KEY_GROUP = 4


def _key_group(nti, want):
    n_lat = nti - 1
    assert n_lat % 2 == 0
    while n_lat % (2 * want):
        want //= 2
    return want


def _chunks_k(k_ref, lead, c0, n):
    return jnp.concatenate([k_ref[lead + (c0 + j,)] for j in range(n)], axis=0)


def _chunks_v(v_ref, lead, c0, n):
    v = jnp.concatenate([v_ref[lead + (c0 + j,)] for j in range(n)], axis=1)
    return jnp.concatenate([v, _ones_rows(n * TILE)], axis=0)


def _pipelined_keys(scores, consume, nti, group):
    steps = (nti - 1) // group
    first = lambda k: 1 + (k - 1) * group
    scores(0, 1, 0)
    latent = pl.program_id(1) > 0

    @pl.when(jnp.logical_not(latent))
    def _():
        consume(0, 1, 0)

    @pl.when(latent)
    def _():
        scores(first(1), group, 1)
        consume(0, 1, 0)
        scores(first(2), group, 0)
        consume(first(1), group, 1)

        def body(j, carry):
            k = 2 * j
            scores(first(k + 1), group, 1)
            consume(first(k), group, 0)
            scores(first(k + 2), group, 0)
            consume(first(k + 1), group, 1)
            return carry

        lax.fori_loop(1, steps // 2, body, 0)
        consume(first(steps), group, 0)


def _diff_attn_kernel(lq1_ref, lk1_ref, lq2_ref, lk2_ref, subw_ref, q_ref, k_ref, v_ref, o_ref,
                      m_ref, acc_ref, sa_ref, sb_ref, *, nti, group, lambda_init):
    _attn_init(m_ref, acc_ref)
    slots = (sa_ref, sb_ref)

    def scores(c0, n, slot):
        ks = [_chunks_k(k_ref, (0, g), c0, n) for g in range(DIFF_MAPS // MAPS_PER_TILE)]
        _score_step([(ks[m // MAPS_PER_TILE], q_ref[0, m]) for m in range(DIFF_MAPS)], slots[slot])

    def consume(c0, n, slot):
        vaugs = []
        for h in range(DIFF_HEADS):
            vaugs += [_chunks_v(v_ref, (0, h), c0, n)] * 2
        _softmax_pv_step(n * TILE, vaugs, slots[slot], m_ref, acc_ref)

    _pipelined_keys(scores, consume, nti, group)
    lam =(jnp.exp(jnp.sum(lq1_ref[...] * lk1_ref[...], keepdims=True))
           - jnp.exp(jnp.sum(lq2_ref[...] * lk2_ref[...], keepdims=True)) + lambda_init)
    for h in range(DIFF_HEADS):
        a1 = acc_ref[2 * h]
        a2 = acc_ref[2 * h + 1]
        o = a1[:DIFF_V] / a1[DIFF_V:DIFF_V + 1] - lam * (a2[:DIFF_V] / a2[DIFF_V:DIFF_V + 1])
        o = o * lax.rsqrt(jnp.mean(o * o, axis=0, keepdims=True) + EPS) * subw_ref[...]
        o_ref[0, h] = (o * (1.0 - lambda_init)).astype(BF16)


def _diff_attn(dq, dk, dv, lw, lambda_init):
    b, nmaps, _, nt = dq.shape
    nh = nmaps // 2
    nti = nt // TILE
    group = _key_group(nti, DIFF_KEY_GROUP)
    kern = functools.partial(_diff_attn_kernel, nti=nti, group=group, lambda_init=lambda_init)
    score_slot = pltpu.VMEM((2 * nh, group * TILE, TILE), F32)
    vec = pl.BlockSpec((1, DIFF_QK), lambda bi, i: (0, 0))
    return pl.pallas_call(
        kern,
        grid=(b, nti),
        in_specs=[vec, vec, vec, vec,
                  pl.BlockSpec((DIFF_V, 1), lambda bi, i: (0, 0)),
                  pl.BlockSpec((1, nmaps, LANES, TILE), lambda bi, i: (bi, 0, 0, i)),
                  pl.BlockSpec((1, nmaps // MAPS_PER_TILE, nti, TILE, LANES), lambda bi, i: (bi, 0, 0, 0, 0)),
                  pl.BlockSpec((1, nh, nti, DIFF_V, TILE), lambda bi, i: (bi, 0, 0, 0, 0))],
        out_specs=pl.BlockSpec((1, nh, DIFF_V, TILE), lambda bi, i: (bi, 0, 0, i)),
        out_shape=jax.ShapeDtypeStruct((b, nh, DIFF_V, nt), BF16),
        scratch_shapes=[pltpu.VMEM((2 * nh, 1, TILE), F32), pltpu.VMEM((2 * nh, ACC_ROWS, TILE), F32),
                        score_slot, score_slot],
        compiler_params=_cparams(("arbitrary", "arbitrary"), VMEM_LIMIT),
        name="diff_attn",
    )(lw["lq1"], lw["lk1"], lw["lq2"], lw["lk2"], lw["subw"], dq, dk, dv)


def _mla_attn_kernel(q_ref, k_ref, v_ref, o_ref, m_ref, acc_ref, sa_ref, sb_ref, *, nti, group):
    _attn_init(m_ref, acc_ref)
    slots = (sa_ref, sb_ref)

    def scores(c0, n, slot):
        _score_step([(_chunks_k(k_ref, (0, h), c0, n), q_ref[0, h]) for h in range(MLA_HEADS)], slots[slot])

    def consume(c0, n, slot):
        vaugs = [_chunks_v(v_ref, (0, h), c0, n) for h in range(MLA_HEADS)]
        _softmax_pv_step(n * TILE, vaugs, slots[slot], m_ref, acc_ref)

    _pipelined_keys(scores, consume, nti, group)
    for h in range(MLA_HEADS):
        a = acc_ref[h]
        o_ref[0, h] = (a[:MLA_V] / a[MLA_V:MLA_V + 1]).astype(BF16)


def _mla_attn(mq, mk, mv):
    b, nh, dpad, nt = mq.shape
    nti = nt // TILE
    group = _key_group(nti, MLA_KEY_GROUP)
    kern = functools.partial(_mla_attn_kernel, nti=nti, group=group)
    score_slot = pltpu.VMEM((nh, group * TILE, TILE), F32)
    return pl.pallas_call(
        kern,
        grid=(b, nti),
        in_specs=[pl.BlockSpec((1, nh, dpad, TILE), lambda bi, i: (bi, 0, 0, i)),
                  pl.BlockSpec((1, nh, nti, TILE, dpad), lambda bi, i: (bi, 0, 0, 0, 0)),
                  pl.BlockSpec((1, nh, nti, MLA_V, TILE), lambda bi, i: (bi, 0, 0, 0, 0))],
        out_specs=pl.BlockSpec((1, nh, MLA_V, TILE), lambda bi, i: (bi, 0, 0, i)),
        out_shape=jax.ShapeDtypeStruct((b, nh, MLA_V, nt), BF16),
        scratch_shapes=[pltpu.VMEM((nh, 1, TILE), F32), pltpu.VMEM((nh, ACC_ROWS, TILE), F32),
                        score_slot, score_slot],
        compiler_params=_cparams(("arbitrary", "arbitrary"), VMEM_LIMIT),
        name="mla_attn",
    )(mq, mk, mv)


OUTPROJ_SPLIT = 2

def _outproj_kernel(t_ref, c_ref, s_ref, d_ref, a_ref, mod_ref, npost_ref, nffn_ref, ws_ref, wd_ref, wa_ref,
                    rwt_ref, rb_ref, tn_ref, hf_ref, aff_ref):
    mod = mod_ref[0, 0]
    t_in = _stream_tile(t_ref, c_ref)
    rwt = rwt_ref[...]
    rw_hi = rwt.astype(BF16)
    rw_lo = (rwt - rw_hi.astype(F32)).astype(BF16)
    rw_both = jnp.concatenate([rw_hi, rw_lo], axis=0)
    half = TILE // OUTPROJ_SPLIT
    for r in range(OUTPROJ_SPLIT):
        rows = slice(half * r, half * (r + 1))
        m = (_dot(s_ref[0, rows, :], ws_ref[...])
             + lax.dot_general(d_ref[0, :, rows], wd_ref[...], TN_DIMS, preferred_element_type=F32)
             + lax.dot_general(a_ref[0, :, rows], wa_ref[...], TN_DIMS, preferred_element_type=F32))
        tn = t_in[rows] + mod[2:3] * _rms(m, npost_ref[...])
        tn_ref[0, rows, :] = tn
        hf = _rms(tn, nffn_ref[...]) * (1.0 + mod[4:5]) + mod[3:4]
        hf_hi = hf.astype(BF16)
        hf_ref[0, rows, :] = hf_hi
        hf_lo = (hf - hf_hi.astype(F32)).astype(BF16)
        both = lax.dot_general(rw_both, hf_hi, NT_DIMS, preferred_element_type=F32)
        logits = (both[:N_EXPERTS] + both[N_EXPERTS:]
                  + lax.dot_general(rw_hi, hf_lo, NT_DIMS, preferred_element_type=F32) + rb_ref[...])
        e = jnp.exp(logits - jnp.max(logits, axis=0, keepdims=True))
        aff_ref[0, :, rows] = e / jnp.sum(e, axis=0, keepdims=True)


def _outproj(t, s, dt_, at_, mod, lw):
    streams, stream_specs, (b, nt, d) = _token_stream(t)
    nti = nt // TILE
    full = lambda a: pl.BlockSpec(a.shape, lambda bi, i: (0,) * a.ndim)
    ws = [lw["norm_mix_post"], lw["norm_ffn_pre"], lw["wo_s"], lw["wo_d"], lw["wo_a"], lw["rwt"], lw["rb"]]
    return pl.pallas_call(
        _outproj_kernel,
        grid=(b, nti),
        in_specs=stream_specs + [
                  pl.BlockSpec((1, TILE, SSD_INNER), lambda bi, i: (bi, i, 0)),
                  pl.BlockSpec((1, DIFF_HEADS * DIFF_V, TILE), lambda bi, i: (bi, 0, i)),
                  pl.BlockSpec((1, MLA_HEADS * MLA_V, TILE), lambda bi, i: (bi, 0, i)),
                  pl.BlockSpec((1, 1, N_MOD, d), lambda bi, i: (bi, jnp.minimum(i, 1), 0, 0))]
                 + [full(a) for a in ws],
        out_specs=[pl.BlockSpec((1, TILE, d), lambda bi, i: (bi, i, 0)),
                   pl.BlockSpec((1, TILE, d), lambda bi, i: (bi, i, 0)),
                   pl.BlockSpec((1, N_EXPERTS, TILE), lambda bi, i: (bi, 0, i))],
        out_shape=[jax.ShapeDtypeStruct((b, nt, d), F32),
                   jax.ShapeDtypeStruct((b, nt, d), BF16),
                   jax.ShapeDtypeStruct((b, N_EXPERTS, nt), F32)],
        compiler_params=_cparams(("arbitrary", "arbitrary"), VMEM_LIMIT),
        name="outproj",
    )(*streams, s, dt_, at_, mod, *ws)


def _route_kernel(aff_ref, pos_ref, gate_ref, cum_ref, *, nti, caps):
    ne = N_EXPERTS
    tri = (lax.broadcasted_iota(I32, (TILE, TILE), 0) < lax.broadcasted_iota(I32, (TILE, TILE), 1)).astype(BF16)
    lane = lax.broadcasted_iota(I32, (ne, LANES), 1)

    def excl_prefix(mask_f):
        return _dot(mask_f.astype(BF16), tri)

    cum_vec = jnp.zeros((ne, LANES), F32)
    total = jnp.zeros((ne, 1), F32)
    seg_bounds = ((0, 1, caps[0]), (1, nti, caps[1]))
    for t0, t1, cap in seg_bounds:
        xi = aff_ref[0, :, t0 * TILE:t1 * TILE]

        def bit_step(j, thr_bits, xi=xi, cap=cap):
            cand = thr_bits | (1 << (29 - j))
            cnt = jnp.sum((xi >= pltpu.bitcast(cand, F32)).astype(F32), axis=1, keepdims=True)
            return jnp.where(cnt >= cap, cand, thr_bits)

        thr = pltpu.bitcast(lax.fori_loop(0, 30, bit_step, jnp.zeros((ne, 1), I32)), F32)
        need = cap - jnp.sum((xi > thr).astype(F32), axis=1, keepdims=True)
        eq_seen = jnp.zeros((ne, 1), F32)
        for t in range(t0, t1):
            lo = (t - t0) * TILE
            xt = xi[:, lo:lo + TILE]
            eq = (xt == thr).astype(F32)
            eq_rank = eq_seen + excl_prefix(eq)
            sel = jnp.where(xt > thr, 1.0, eq * (eq_rank < need).astype(F32))
            eq_seen = eq_seen + jnp.sum(eq, axis=1, keepdims=True)
            rank = total + excl_prefix(sel)
            pos_ref[0, :, t * TILE:(t + 1) * TILE] = jnp.where(sel > 0.0, rank, -1.0).astype(I32)
            gate_ref[0, :, t * TILE:(t + 1) * TILE] = sel * aff_ref[0, :, t * TILE:(t + 1) * TILE]
            cum_vec = jnp.where(lane == t, total, cum_vec)
            total = total + jnp.sum(sel, axis=1, keepdims=True)
    cum_vec = jnp.where(lane == nti, total, cum_vec)
    cum_ref[0] = cum_vec.astype(I32)


def _route(aff, caps):
    b, ne, nt = aff.shape
    nti = nt // TILE
    kern = functools.partial(_route_kernel, nti=nti, caps=caps)
    return pl.pallas_call(
        kern,
        grid=(b,),
        in_specs=[pl.BlockSpec((1, ne, nt), lambda bi: (bi, 0, 0))],
        out_specs=[pl.BlockSpec((1, ne, nt), lambda bi: (bi, 0, 0)),
                   pl.BlockSpec((1, ne, nt), lambda bi: (bi, 0, 0)),
                   pl.BlockSpec((1, ne, LANES), lambda bi: (bi, 0, 0))],
        out_shape=[jax.ShapeDtypeStruct((b, ne, nt), I32),
                   jax.ShapeDtypeStruct((b, ne, nt), F32),
                   jax.ShapeDtypeStruct((b, ne, LANES), I32)],
        compiler_params=_cparams(("arbitrary",)),
        name="route",
    )(aff)


WIN = 64
GROUP = 4


def _tile_windows(cum_ref, b, t, rows):
    los = []
    rounds = jnp.int32(1)
    for e in range(N_EXPERTS):
        base = (b * N_EXPERTS + e) * LANES
        lo = (cum_ref[base + t] // 16) * 16
        los.append(lo)
        rounds = jnp.maximum(rounds, (cum_ref[base + t + 1] - lo + WIN - 1) // WIN)
    return los, rounds


def _window_onehot(pos_row, lo, r, rows):
    want = lo + WIN * r
    w0 = pl.multiple_of(jnp.minimum(want, rows - WIN), 16)
    rowid = w0 + lax.broadcasted_iota(I32, (WIN, TILE), 0)
    return w0, jnp.logical_and(pos_row == rowid, rowid >= want).astype(F32)


def _gather_kernel(cum_ref, hf_ref, pos_ref, gate_ref, xg_ref, gc_ref, *, rows):
    b = pl.program_id(0)
    t = pl.program_id(1)

    @pl.when(t == 0)
    def _():
        xg_ref[...] = jnp.zeros_like(xg_ref)
        gc_ref[...] = jnp.zeros_like(gc_ref)

    los, rounds = _tile_windows(cum_ref, b, t, rows)

    def round_step(r, carry):
        w0s, hots = [], []
        for e in range(N_EXPERTS):
            w0, hot = _window_onehot(pos_ref[0, e:e + 1, :], los[e], r, rows)
            w0s.append(w0)
            hots.append(hot)
            gc_ref[0, e, pl.ds(w0, WIN), :] += jnp.sum(hot * gate_ref[0, e:e + 1, :], axis=1, keepdims=True)
        res = _dot(jnp.concatenate(hots, axis=0).astype(BF16), hf_ref[0])
        for e in range(N_EXPERTS):
            xg_ref[0, e, pl.ds(w0s[e], WIN), :] += res[WIN * e:WIN * (e + 1)].astype(BF16)
        return carry

    lax.fori_loop(0, rounds, round_step, 0)


def _gather(cum_flat, hf, pos, gate, rows):
    b, nt, d = hf.shape
    nti = nt // TILE
    ne = pos.shape[1]
    kern = functools.partial(_gather_kernel, rows=rows)
    grid_spec = pltpu.PrefetchScalarGridSpec(
        num_scalar_prefetch=1,
        grid=(b, nti),
        in_specs=[pl.BlockSpec((1, TILE, d), lambda bi, i, cum: (bi, i, 0)),
                  pl.BlockSpec((1, ne, TILE), lambda bi, i, cum: (bi, 0, i)),
                  pl.BlockSpec((1, ne, TILE), lambda bi, i, cum: (bi, 0, i))],
        out_specs=[pl.BlockSpec((1, ne, rows, d), lambda bi, i, cum: (bi, 0, 0, 0)),
                   pl.BlockSpec((1, ne, rows, 1), lambda bi, i, cum: (bi, 0, 0, 0))],
    )
    return pl.pallas_call(
        kern,
        grid_spec=grid_spec,
        out_shape=[jax.ShapeDtypeStruct((b, ne, rows, d), BF16), jax.ShapeDtypeStruct((b, ne, rows, 1), F32)],
        compiler_params=_cparams(("arbitrary", "arbitrary"), VMEM_LIMIT),
        name="gather",
    )(cum_flat, hf, pos, gate)


def _experts_kernel(xg_ref, gc_ref, wg_ref, wu_ref, wd_ref, y_ref, wgb_ref, wub_ref, wdb_ref):
    @pl.when(pl.program_id(1) == 0)
    def _():
        wgb_ref[...] = wg_ref[0, 0].astype(BF16)
        wub_ref[...] = wu_ref[0, 0].astype(BF16)
        wdb_ref[...] = wd_ref[0, 0].astype(BF16)

    nb, _, rows, _ = xg_ref.shape
    half = rows // 2
    for bb in range(nb):
        for r in range(2):
            sl = slice(half * r, half * (r + 1))
            xg = xg_ref[bb, 0, sl, :]
            hid = (_silu(_dot(xg, wgb_ref[...])) * _dot(xg, wub_ref[...])).astype(BF16)
            y_ref[bb, 0, sl, :] = (_dot(hid, wdb_ref[...]) * gc_ref[bb, 0, sl, :]).astype(BF16)


def _experts(xg, gc, lw, l):
    b, ne, rows, d = xg.shape
    ff = lw["w_gate"].shape[3]
    nb = 2 if b % 2 == 0 else 1
    return pl.pallas_call(
        _experts_kernel,
        grid=(ne, b // nb),
        in_specs=[pl.BlockSpec((nb, 1, rows, d), lambda e, bi: (bi, e, 0, 0)),
                  pl.BlockSpec((nb, 1, rows, 1), lambda e, bi: (bi, e, 0, 0)),
                  pl.BlockSpec((1, 1, d, ff), lambda e, bi: (l, e, 0, 0)),
                  pl.BlockSpec((1, 1, d, ff), lambda e, bi: (l, e, 0, 0)),
                  pl.BlockSpec((1, 1, ff, d), lambda e, bi: (l, e, 0, 0))],
        out_specs=pl.BlockSpec((nb, 1, rows, d), lambda e, bi: (bi, e, 0, 0)),
        out_shape=jax.ShapeDtypeStruct((b, ne, rows, d), BF16),
        scratch_shapes=[pltpu.VMEM((d, ff), BF16), pltpu.VMEM((d, ff), BF16), pltpu.VMEM((ff, d), BF16)],
        compiler_params=_cparams(("arbitrary", "arbitrary"), VMEM_LIMIT),
        name="experts",
    )(xg, gc, lw["w_gate"], lw["w_up"], lw["w_down"])


def _combine_kernel(cum_ref, t_ref, y_ref, pos_ref, mod_ref, npost_ref, o_ref, f_ref, *, rows, latent_only):
    b = pl.program_id(0)
    t = pl.program_id(1)

    def run():
        los, rounds = _tile_windows(cum_ref, b, t, rows)

        def scatter_round(r):
            total = None
            for g in range(N_EXPERTS // GROUP):
                hots, wins = [], []
                for e in range(GROUP * g, GROUP * (g + 1)):
                    w0, hot = _window_onehot(pos_ref[0, e:e + 1, :], los[e], r, rows)
                    hots.append(hot)
                    wins.append(y_ref[0, e, pl.ds(w0, WIN), :])
                hot = jnp.concatenate(hots, axis=0).astype(BF16)
                part = lax.dot_general(hot, jnp.concatenate(wins, axis=0), TN_DIMS, preferred_element_type=F32)
                total = part if total is None else total + part
            return total

        f_ref[...] = scatter_round(0)

        def round_step(r, carry):
            f_ref[...] += scatter_round(r)
            return carry

        lax.fori_loop(1, rounds, round_step, 0)
        mod = mod_ref[0, 0]
        o_ref[0] = t_ref[0] + mod[5:6] * _rms(f_ref[...], npost_ref[...])

    if latent_only:
        pl.when(t > 0)(run)
    else:
        run()


def _combine(cum_flat, t, y, pos, mod, lw, latent_only):
    b, nt, d = t.shape
    nti = nt // TILE
    ne, rows = y.shape[1], y.shape[2]
    kern = functools.partial(_combine_kernel, rows=rows, latent_only=latent_only)
    if latent_only:
        out_rows, out_map = nt - TILE, lambda bi, i, cum: (bi, jnp.maximum(i - 1, 0), 0)
    else:
        out_rows, out_map = nt, lambda bi, i, cum: (bi, i, 0)
    grid_spec = pltpu.PrefetchScalarGridSpec(
        num_scalar_prefetch=1,
        grid=(b, nti),
        in_specs=[pl.BlockSpec((1, TILE, d), lambda bi, i, cum: (bi, i, 0)),
                  pl.BlockSpec((1, ne, rows, d), lambda bi, i, cum: (bi, 0, 0, 0)),
                  pl.BlockSpec((1, ne, TILE), lambda bi, i, cum: (bi, 0, i)),
                  pl.BlockSpec((1, 1, N_MOD, d), lambda bi, i, cum: (bi, jnp.minimum(i, 1), 0, 0)),
                  pl.BlockSpec((1, d), lambda bi, i, cum: (0, 0))],
        out_specs=pl.BlockSpec((1, TILE, d), out_map),
        scratch_shapes=[pltpu.VMEM((TILE, d), F32)],
    )
    return pl.pallas_call(
        kern,
        grid_spec=grid_spec,
        out_shape=jax.ShapeDtypeStruct((b, out_rows, d), F32),
        compiler_params=_cparams(("arbitrary", "arbitrary"), VMEM_LIMIT),
        name="combine",
    )(cum_flat, t, y, pos, mod, lw["norm_ffn_post"])


def _rope_tables(seq, ctx):
    quarter = MLA_ROPE // 4
    inv = ROPE_BASE ** (-jnp.arange(quarter, dtype=F32) / quarter)
    n_rows = seq // GRID_W
    rows = jnp.repeat(jnp.arange(n_rows, dtype=F32), GRID_W)
    cols = jnp.tile(jnp.arange(GRID_W, dtype=F32), n_rows)
    ar = rows[:, None] * inv[None, :]
    ac = cols[:, None] * inv[None, :]
    cos = jnp.concatenate([jnp.cos(ar), jnp.cos(ar), jnp.cos(ac), jnp.cos(ac)], axis=1)
    sin = jnp.concatenate([-jnp.sin(ar), jnp.sin(ar), -jnp.sin(ac), jnp.sin(ac)], axis=1)
    cos = jnp.concatenate([jnp.ones((ctx, MLA_ROPE), F32), cos], axis=0)
    sin = jnp.concatenate([jnp.zeros((ctx, MLA_ROPE), F32), sin], axis=0)
    return {"ck": jnp.tile(cos, (1, DIFF_MAPS)), "sk": jnp.tile(sin, (1, DIFF_MAPS)), "ct": cos.T, "st": sin.T}


def _partner_perm(width):
    idx = jnp.arange(width)
    r = idx % 16
    return jnp.where(r < 8, idx + 8, idx - 8)


def _layer_weights(l, p):
    d = p["w_in"].shape[1]
    w_in = p["w_in"][l]
    o_diff = 2 * SSD_INNER + 2 * SSD_BC + 2 * SSD_HEADS
    o_mla = o_diff + 3 * DIFF_HEADS * DIFF_V
    nk = DIFF_MAPS * DIFF_QK
    w_ssd = w_in[:, :o_diff]
    wa = jnp.concatenate([w_ssd, jnp.zeros((d, LANES - 2 * SSD_HEADS), F32)], axis=1)
    wq = w_in[:, o_diff:o_diff + nk]
    wk = w_in[:, o_diff + nk:o_diff + 2 * nk]
    wv = w_in[:, o_diff + 2 * nk:o_mla]
    wcq =w_in[:, o_mla:o_mla + MLA_Q_LORA]
    wckv = w_in[:, o_mla + MLA_Q_LORA:o_mla + MLA_Q_LORA + MLA_KV_LORA]
    wkr = w_in[:, o_mla + MLA_Q_LORA + MLA_KV_LORA:]
    zeros = lambda n: jnp.zeros((d, n), F32)
    wm = jnp.concatenate([wcq, zeros(256 - MLA_Q_LORA), wckv, wkr, wkr[:, _partner_perm(MLA_ROPE)],
                          zeros(512 - 448)], axis=1)

    wqu = p["mla_w_q_up"][l].reshape(MLA_Q_LORA, MLA_HEADS, MLA_NOPE + MLA_ROPE)
    pad = jnp.zeros((MLA_Q_LORA, MLA_HEADS, MLA_QK_PAD - MLA_NOPE - MLA_ROPE), F32)
    wqu_plain = jnp.concatenate([wqu, pad], axis=2).reshape(MLA_Q_LORA, -1)
    wkvu = p["mla_w_kv_up"][l].reshape(MLA_KV_LORA, MLA_HEADS, MLA_NOPE + MLA_V)
    wk2 = jnp.concatenate([wkvu[:, :, :MLA_NOPE],
                           jnp.zeros((MLA_KV_LORA, MLA_HEADS, MLA_QK_PAD - MLA_NOPE), F32)],
                          axis=2).reshape(MLA_KV_LORA, -1)
    eye = jnp.eye(MLA_ROPE, dtype=F32)
    ek_h = jnp.concatenate([jnp.zeros((MLA_ROPE, MLA_NOPE), F32), eye,
                            jnp.zeros((MLA_ROPE, MLA_QK_PAD - MLA_NOPE - MLA_ROPE), F32)], axis=1)
    ek = jnp.tile(ek_h, (1, MLA_HEADS))
    wv2 = wkvu[:, :, MLA_NOPE:].reshape(MLA_KV_LORA, -1)

    w_out = p["w_out"][l]
    row = lambda a: a.reshape(1, -1)
    col = lambda a: a.reshape(-1, 1)
    return {
        "norm_mix_pre": row(p["norm_mix_pre"][l]), "norm_mix_post": row(p["norm_mix_post"][l]),
        "norm_ffn_pre": row(p["norm_ffn_pre"][l]), "norm_ffn_post": row(p["norm_ffn_post"][l]),
        "wa": wa.astype(BF16),
        "wdk": wk.astype(BF16),
        "wm": wm.astype(BF16),
        "wqt": wq.T.astype(BF16),
        "wvt": wv.T.astype(BF16),
        "qnw": row(p["mla_q_norm"][l]), "kvnw": row(p["mla_kv_norm"][l]),
        "wqut": wqu_plain.T.astype(BF16),
        "wk2": wk2.astype(BF16), "ek": ek.astype(BF16), "wvt2": wv2.T.astype(BF16),
        "conv_w": p["ssd_conv_w"][l], "conv_b": row(p["ssd_conv_b"][l]),
        "alog": col(p["ssd_a_log"][l]), "dtb": col(p["ssd_dt_bias"][l]),
        "dsk": row(p["ssd_d"][l]), "ssd_norm": row(p["ssd_norm"][l]),
        "lq1": row(p["diff_lam_q1"][l]), "lk1": row(p["diff_lam_k1"][l]),
        "lq2": row(p["diff_lam_q2"][l]), "lk2": row(p["diff_lam_k2"][l]),
        "subw": p["diff_subln"][l].reshape(-1, 1),
        "wo_s": w_out[:SSD_INNER].astype(BF16),
        "wo_d": w_out[SSD_INNER:SSD_INNER + DIFF_HEADS * DIFF_V].astype(BF16),
        "wo_a": w_out[SSD_INNER + DIFF_HEADS * DIFF_V:].astype(BF16),
        "rwt": p["router_w"][l].T, "rb": p["router_b"][l].reshape(-1, 1),
        "w_gate": p["w_gate"], "w_up": p["w_up"], "w_down": p["w_down"],
    }


def kernel(x, c, ctx, c_ctx, ada_w, ada_b, norm_mix_pre, norm_mix_post, norm_ffn_pre, norm_ffn_post, w_in, ssd_conv_w, ssd_conv_b, ssd_a_log, ssd_dt_bias, ssd_d, ssd_norm, diff_lam_q1, diff_lam_k1, diff_lam_q2, diff_lam_k2, diff_subln, mla_q_norm, mla_w_q_up, mla_kv_norm, mla_w_kv_up, w_out, router_w, router_b, w_gate, w_up, w_down):
    p = dict(norm_mix_pre=norm_mix_pre, norm_mix_post=norm_mix_post, norm_ffn_pre=norm_ffn_pre,
             norm_ffn_post=norm_ffn_post, w_in=w_in, ssd_conv_w=ssd_conv_w, ssd_conv_b=ssd_conv_b,
             ssd_a_log=ssd_a_log, ssd_dt_bias=ssd_dt_bias, ssd_d=ssd_d, ssd_norm=ssd_norm,
             diff_lam_q1=diff_lam_q1, diff_lam_k1=diff_lam_k1, diff_lam_q2=diff_lam_q2, diff_lam_k2=diff_lam_k2,
             diff_subln=diff_subln, mla_q_norm=mla_q_norm, mla_w_q_up=mla_w_q_up, mla_kv_norm=mla_kv_norm,
             mla_w_kv_up=mla_w_kv_up, w_out=w_out, router_w=router_w, router_b=router_b,
             w_gate=w_gate, w_up=w_up, w_down=w_down)
    b, seq, d = x.shape
    nctx = ctx.shape[1]
    depth = ada_w.shape[0]
    assert nctx == TILE and seq % TILE == 0 and seq % GRID_W == 0
    nt = nctx + seq
    caps = (EC_CAPACITY * nctx // N_EXPERTS, EC_CAPACITY * seq // N_EXPERTS)
    assert caps[0] % 16 == 0 and caps[1] % 16 == 0 and caps[0] + caps[1] >= WIN

    cvec = jnp.concatenate([c, c_ctx[None, :], jnp.zeros((8 - b - 1, d), F32)], axis=0)
    mods = _adaln(cvec, ada_w, ada_b).reshape(depth, 8, N_MOD, d)
    tabs = _rope_tables(seq, nctx)
    t = (x, ctx)
    for l in range(depth):
        lw = _layer_weights(l, p)
        lambda_init = 0.8 - 0.6 * math.exp(-0.3 * l)
        mod = jnp.stack([jnp.broadcast_to(mods[l, b], (b, N_MOD, d)), mods[l, :b]], axis=1)
        z, xbc, dt, dq, dk, dv, mq, mk, mv = _inproj(t, mod, lw, tabs)
        s = _ssd(z, xbc, dt, lw)
        da = _diff_attn(dq, dk, dv, lw, lambda_init).reshape(b, DIFF_HEADS * DIFF_V, nt)
        aa = _mla_attn(mq, mk, mv).reshape(b, MLA_HEADS * MLA_V, nt)
        t, hf, aff = _outproj(t, s, da, aa, mod, lw)
        pos, gate, cum = _route(aff, caps)
        cum_flat = cum.reshape(-1)
        xg, gc = _gather(cum_flat, hf, pos, gate, caps[0] + caps[1])
        y = _experts(xg, gc, lw, l)
        t = _combine(cum_flat, t, y, pos, mod, lw, latent_only=(l == depth - 1))
    return t
```

```python
import functools
import math

import jax
import jax.numpy as jnp
from jax import lax
from jax.experimental import pallas as pl
from jax.experimental.pallas import tpu as pltpu

F32 = jnp.float32
BF16 = jnp.bfloat16
I32 = jnp.int32
HIGHEST = lax.Precision.HIGHEST

EPS = 1e-6
GRID_W = 64
ROPE_BASE = 10000.0
N_MOD = 6

SSD_HEADS = 8
SSD_HEAD_DIM = 64
SSD_INNER = SSD_HEADS * SSD_HEAD_DIM
SSD_GROUPS = 2
SSD_STATE = 64
SSD_CHUNK = 128
SSD_BC = SSD_GROUPS * SSD_STATE
SSD_CONV_DIM = SSD_INNER + 2 * SSD_BC
SSD_PAIRS = SSD_HEADS // 2

DIFF_HEADS = 4
DIFF_QK = 32
DIFF_V = 64
DIFF_MAPS = 2 * DIFF_HEADS

MLA_HEADS = 4
MLA_Q_LORA = 192
MLA_KV_LORA = 128
MLA_NOPE = 64
MLA_ROPE = 32
MLA_V = 64
MLA_QK_PAD = 128

N_EXPERTS = 16
EC_CAPACITY = 2

TILE = 256
LANES = 128
MAPS_PER_TILE = LANES // DIFF_QK
VMEM_LIMIT = 56 * 1024 * 1024

NT_DIMS = (((1,), (1,)), ((), ()))
TN_DIMS = (((0,), (0,)), ((), ()))


def _cparams(sem, vmem=None):
    return pltpu.CompilerParams(dimension_semantics=sem, vmem_limit_bytes=vmem)


def _rms(x, w):
    return x * lax.rsqrt(jnp.mean(x * x, axis=-1, keepdims=True) + EPS) * w


def _silu(x):
    return x * jax.nn.sigmoid(x)


def _dot(a, b):
    return jnp.dot(a, b, preferred_element_type=F32)


def _adaln_kernel(c_ref, w_ref, b_ref, o_ref):
    s = _silu(c_ref[...])
    o_ref[0] = lax.dot_general(s, w_ref[0], (((1,), (0,)), ((), ())), precision=HIGHEST,
                               preferred_element_type=F32) + b_ref[0]


def _adaln(cvec, ada_w, ada_b):
    depth, d, nd = ada_w.shape
    rows = cvec.shape[0]
    return pl.pallas_call(
        _adaln_kernel,
        grid=(depth, nd // d),
        in_specs=[pl.BlockSpec((rows, d), lambda l, j: (0, 0)),
                  pl.BlockSpec((1, d, d), lambda l, j: (l, 0, j)),
                  pl.BlockSpec((1, 1, d), lambda l, j: (l, 0, j))],
        out_specs=pl.BlockSpec((1, rows, d), lambda l, j: (l, 0, j)),
        out_shape=jax.ShapeDtypeStruct((depth, rows, nd), F32),
        compiler_params=_cparams(("arbitrary", "arbitrary")),
        name="adaln",
    )(cvec, ada_w, ada_b.reshape(depth, 1, nd))


def _token_stream(t):
    if isinstance(t, tuple):
        x, ctx = t
        b, seq, d = x.shape
        specs = [pl.BlockSpec((1, TILE, d), lambda bi, i: (bi, jnp.maximum(i - 1, 0), 0)),
                 pl.BlockSpec((1, TILE, d), lambda bi, i: (bi, 0, 0))]
        return (x, ctx), specs, (b, seq + ctx.shape[1], d)
    b, nt, d = t.shape
    specs = [pl.BlockSpec((1, TILE, d), lambda bi, i: (bi, i, 0)),
             pl.BlockSpec((1, TILE, d), lambda bi, i: (bi, 0, 0))]
    return (t, t), specs, (b, nt, d)


def _stream_tile(x_ref, c_ref):
    return jnp.where(pl.program_id(1) == 0, c_ref[0], x_ref[0])


def _partner_rows(x):
    parts = []
    for g in range(0, x.shape[0], 16):
        parts += [x[g + 8:g + 16], x[g:g + 8]]
    return jnp.concatenate(parts, axis=0)


def _partner_lanes(x):
    width = x.shape[1]
    lane = lax.broadcasted_iota(I32, x.shape, 1)
    return jnp.where((lane & 8) == 0, pltpu.roll(x, width - 8, 1), pltpu.roll(x, 8, 1))


def _inproj_kernel(x_ref, c_ref, mod_ref, nw_ref, wa_ref, wdk_ref, wm_ref, wqt_ref, wvt_ref,
                   ck_ref, sk_ref, ct_ref, st_ref, qnw_ref, kvnw_ref, wqut_ref, wk2_ref, ek_ref, wvt2_ref,
                   z_ref, xbc_ref, dt_ref, dq_ref, dk_ref, dv_ref, mq_ref, mk_ref, mv_ref):
    x = _stream_tile(x_ref, c_ref)
    mod = mod_ref[0, 0]
    h = (_rms(x, nw_ref[...]) * (1.0 + mod[1:2]) + mod[0:1]).astype(BF16)

    ra = _dot(h, wa_ref[...])
    z_ref[0] = ra[:, :SSD_INNER].astype(BF16)
    xbc_ref[0] = ra[:, SSD_INNER:SSD_INNER + SSD_CONV_DIM]
    dt_ref[0] = ra[:, SSD_INNER + SSD_CONV_DIM:]

    rk = _dot(h, wdk_ref[...])
    k = (rk * ck_ref[...] + _partner_lanes(rk) * sk_ref[...]).astype(BF16)
    for g in range(DIFF_MAPS // MAPS_PER_TILE):
        dk_ref[0, g, 0] = k[:, LANES * g:LANES * (g + 1)]

    ct = ct_ref[...]
    st = st_ref[...]
    rq = lax.dot_general(wqt_ref[...], h, NT_DIMS, preferred_element_type=F32)
    rq_partner = _partner_rows(rq)
    for m in range(DIFF_MAPS):
        lo = DIFF_QK * m
        qm = ((rq[lo:lo + DIFF_QK] * ct + rq_partner[lo:lo + DIFF_QK] * st) * DIFF_C_EXP).astype(BF16)
        above = DIFF_QK * (m % MAPS_PER_TILE)
        below = LANES - above - DIFF_QK
        parts = ([jnp.zeros((above, qm.shape[1]), BF16)] if above else []) + [qm]
        parts += [jnp.zeros((below, qm.shape[1]), BF16)] if below else []
        dq_ref[0, m] = jnp.concatenate(parts, axis=0)

    rv = lax.dot_general(wvt_ref[...], h, NT_DIMS, preferred_element_type=F32).astype(BF16)
    for hd in range(DIFF_HEADS):
        dv_ref[0, hd, 0] = rv[DIFF_V * hd:DIFF_V * (hd + 1)]

    rm = _dot(h, wm_ref[...])
    cq = _rms(rm[:, :MLA_Q_LORA], qnw_ref[...]).astype(BF16)
    ckv = _rms(rm[:, 256:256 + MLA_KV_LORA], kvnw_ref[...]).astype(BF16)
    kr = rm[:, 384:384 + MLA_ROPE] * ck_ref[:, :MLA_ROPE] + rm[:, 416:416 + MLA_ROPE] * sk_ref[:, :MLA_ROPE]

    rq2 = lax.dot_general(wqut_ref[...], cq, NT_DIMS, preferred_element_type=F32)
    rq2_partner = _partner_rows(rq2)
    ones = jnp.ones((MLA_NOPE, ct.shape[1]), F32)
    pad1 = jnp.ones((MLA_QK_PAD - MLA_NOPE - MLA_ROPE, ct.shape[1]), F32)
    ct_h = jnp.concatenate([ones, ct, pad1], axis=0)
    st_h = jnp.concatenate([0.0 * ones, st, 0.0 * pad1], axis=0)
    for hd in range(MLA_HEADS):
        lo = MLA_QK_PAD * hd
        qh = rq2[lo:lo + MLA_QK_PAD] * ct_h + rq2_partner[lo:lo + MLA_QK_PAD] * st_h
        mq_ref[0, hd] = (qh * MLA_C_EXP).astype(BF16)

    k2 = (_dot(ckv, wk2_ref[...]) + _dot(kr.astype(BF16), ek_ref[...])).astype(BF16)
    for hd in range(MLA_HEADS):
        mk_ref[0, hd, 0] = k2[:, MLA_QK_PAD * hd:MLA_QK_PAD * (hd + 1)]
    rv2 = lax.dot_general(wvt2_ref[...], ckv, NT_DIMS, preferred_element_type=F32).astype(BF16)
    for hd in range(MLA_HEADS):
        mv_ref[0, hd, 0] = rv2[MLA_V * hd:MLA_V * (hd + 1)]


def _inproj(t, mod, lw, tabs):
    streams, stream_specs, (b, nt, d) = _token_stream(t)
    nti = nt // TILE
    full = lambda a: pl.BlockSpec(a.shape, lambda bi, i: (0,) * a.ndim)
    tok = lambda w: pl.BlockSpec((TILE, w), lambda bi, i: (i, 0))
    tokt = lambda w: pl.BlockSpec((w, TILE), lambda bi, i: (0, i))
    ws = [lw["norm_mix_pre"], lw["wa"], lw["wdk"], lw["wm"], lw["wqt"], lw["wvt"]]
    ws2 = [lw["qnw"], lw["kvnw"], lw["wqut"], lw["wk2"], lw["ek"], lw["wvt2"]]
    out_shape = [
        jax.ShapeDtypeStruct((b, nt, SSD_INNER), BF16),
        jax.ShapeDtypeStruct((b, nt, SSD_CONV_DIM), F32),
        jax.ShapeDtypeStruct((b, nt, LANES), F32),
        jax.ShapeDtypeStruct((b, DIFF_MAPS, LANES, nt), BF16),
        jax.ShapeDtypeStruct((b, DIFF_MAPS // MAPS_PER_TILE, nti, TILE, LANES), BF16),
        jax.ShapeDtypeStruct((b, DIFF_HEADS, nti, DIFF_V, TILE), BF16),
        jax.ShapeDtypeStruct((b, MLA_HEADS, MLA_QK_PAD, nt), BF16),
        jax.ShapeDtypeStruct((b, MLA_HEADS, nti, TILE, MLA_QK_PAD), BF16),
        jax.ShapeDtypeStruct((b, MLA_HEADS, nti, MLA_V, TILE), BF16),
    ]
    out_specs = [
        pl.BlockSpec((1, TILE, SSD_INNER), lambda bi, i: (bi, i, 0)),
        pl.BlockSpec((1, TILE, SSD_CONV_DIM), lambda bi, i: (bi, i, 0)),
        pl.BlockSpec((1, TILE, LANES), lambda bi, i: (bi, i, 0)),
        pl.BlockSpec((1, DIFF_MAPS, LANES, TILE), lambda bi, i: (bi, 0, 0, i)),
        pl.BlockSpec((1, DIFF_MAPS // MAPS_PER_TILE, 1, TILE, LANES), lambda bi, i: (bi, 0, i, 0, 0)),
        pl.BlockSpec((1, DIFF_HEADS, 1, DIFF_V, TILE), lambda bi, i: (bi, 0, i, 0, 0)),
        pl.BlockSpec((1, MLA_HEADS, MLA_QK_PAD, TILE), lambda bi, i: (bi, 0, 0, i)),
        pl.BlockSpec((1, MLA_HEADS, 1, TILE, MLA_QK_PAD), lambda bi, i: (bi, 0, i, 0, 0)),
        pl.BlockSpec((1, MLA_HEADS, 1, MLA_V, TILE), lambda bi, i: (bi, 0, i, 0, 0)),
    ]
    in_specs = (stream_specs
                + [pl.BlockSpec((1, 1, N_MOD, d), lambda bi, i: (bi, jnp.minimum(i, 1), 0, 0))]
                + [full(a) for a in ws]
                + [tok(DIFF_MAPS * DIFF_QK), tok(DIFF_MAPS * DIFF_QK), tokt(DIFF_QK), tokt(DIFF_QK)]
                + [full(a) for a in ws2])
    return pl.pallas_call(
        _inproj_kernel,
        grid=(b, nti),
        in_specs=in_specs,
        out_specs=out_specs,
        out_shape=out_shape,
        compiler_params=_cparams(("arbitrary", "arbitrary"), VMEM_LIMIT),
        name="inproj",
    )(*streams, mod, *ws, tabs["ck"], tabs["sk"], tabs["ct"], tabs["st"], *ws2)


def _ssd_chunk_of(ph, i, nck, nctx):
    back = jnp.where(i < nctx, nctx - 1 - i, nck - 1 + nctx - i)
    return jnp.where(ph == 0, back, i)


def _ssd_kernel(xc_ref, xp_ref, xn_ref, dt_ref, z_ref, cw_ref, cb_ref, alog_ref, dtb_ref, dsk_ref, nw_ref,
                o_ref, s_ref, sb_ref, *cache, nck, nctx, nb):
    ph = pl.program_id(1)
    i = pl.program_id(2)
    c = _ssd_chunk_of(ph, i, nck, nctx)

    @pl.when(i == 0)
    def _():
        s_ref[...] = jnp.zeros_like(s_ref)

    one = lambda ref, bb: ref.at[pl.ds(bb, 1)]
    fns = [_ssd_sample(one(xc_ref, bb), one(xp_ref, bb), one(xn_ref, bb), one(dt_ref, bb), one(z_ref, bb),
                       cw_ref, cb_ref, alog_ref, dtb_ref, dsk_ref, nw_ref, one(o_ref, bb),
                       s_ref.at[bb], sb_ref.at[bb], [r.at[bb] for r in cache], c, nck=nck, nctx=nctx)
           for bb in range(nb)]

    @pl.when(ph == 0)
    def _():
        for backward_states, _ in fns:
            backward_states()

    @pl.when(ph == 1)
    def _():
        for _, forward_and_output in fns:
            forward_and_output()


def _ssd_sample(xc_ref, xp_ref, xn_ref, dt_ref, z_ref, cw_ref, cb_ref, alog_ref, dtb_ref, dsk_ref, nw_ref,
                o_ref, s_ref, sb_ref, cache, c, *, nck, nctx):
    q = SSD_CHUNK
    nh2 = 2 * SSD_HEADS
    cx_ref, cbc_ref, crow_ref, ccol_ref, cbt_ref = cache
    ri = lax.broadcasted_iota(I32, (q, q), 0)
    ci = lax.broadcasted_iota(I32, (q, q), 1)
    lower = ci <= ri
    upper = ci >= ri
    lane = ci
    first_half_s = ri < SSD_STATE
    first_half_l = lane < SSD_HEAD_DIM
    blockdiag = first_half_s == first_half_l

    def chunk_values():
        x = xc_ref[0]
        has_prev = jnp.logical_and(c != 0, c != nctx)
        has_next = jnp.logical_and(c != nctx - 1, c != nck - 1)
        prev_row = jnp.where(has_prev, xp_ref[0][7:8, :], 0.0)
        next_row = jnp.where(has_next, xn_ref[0][0:1, :], 0.0)
        row = lax.broadcasted_iota(I32, x.shape, 0)
        xm1 = jnp.where(row == 0, prev_row, pltpu.roll(x, 1, 0))
        xp1 = jnp.where(row == q - 1, next_row, pltpu.roll(x, q - 1, 0))
        cw = cw_ref[...]
        u = _silu(xm1 * cw[0:1] + x * cw[1:2] + xp1 * cw[2:3] + cb_ref[...])
        xs = u[:, :SSD_INNER].astype(BF16)
        bc = u[:, SSD_INNER:]
        xdt = dt_ref[0].T[:nh2] + dtb_ref[...]
        dtt = jnp.maximum(xdt, 0.0) + jnp.log1p(jnp.exp(-jnp.abs(xdt)))
        dat = dtt * (-jnp.exp(alog_ref[...]))
        tri_dims = (((1,), (0,)), ((), ()))
        acf = lax.dot_general(dat, upper.astype(F32), tri_dims, precision=HIGHEST, preferred_element_type=F32)
        acb = lax.dot_general(dat, lower.astype(F32), tri_dims, precision=HIGHEST, preferred_element_type=F32)
        act = jnp.where(ri[:nh2] < SSD_HEADS, acf, acb)
        rows = jnp.concatenate([dtt, act], axis=0)
        cols = jnp.concatenate([rows, jnp.zeros((q - 2 * nh2, q), F32)], axis=0).T
        bt = bc[:, :SSD_BC].T
        return xs, bc, rows, cols, bt

    def bodies(xs, bc, rows, cols, bt):
        bm = bc[:, :SSD_BC]
        cm = bc[:, SSD_BC:]
        dtt = rows[:nh2]
        act = rows[nh2:]
        dtc = cols
        acc = pltpu.roll(cols, LANES - nh2, 1)

        def pair_vals(arr_c, arr_t, h0):
            col = jnp.where(first_half_l, arr_c[:, h0:h0 + 1], arr_c[:, h0 + 1:h0 + 2])
            rowv = jnp.where(first_half_s, arr_t[h0:h0 + 1, :], arr_t[h0 + 1:h0 + 2, :])
            return col, rowv

        def state_update(p, d):
            g = (2 * p) // (SSD_HEADS // SSD_GROUPS)
            h0 = d * SSD_HEADS + 2 * p
            edge = q - 1 if d == 0 else 0
            alast_row = jnp.where(first_half_s[:, 0:1], acc[edge:edge + 1, h0:h0 + 1],
                                  acc[edge:edge + 1, h0 + 1:h0 + 2])
            _, ar = pair_vals(acc, act, h0)
            _, dr = pair_vals(dtc, dtt, h0)
            w = jnp.exp(alast_row - ar) * dr
            btg = bt[SSD_STATE * g:SSD_STATE * (g + 1)]
            lhs = (jnp.concatenate([btg, btg], axis=0) * w).astype(BF16)
            xs2 = xs[:, 2 * SSD_HEAD_DIM * p:2 * SSD_HEAD_DIM * (p + 1)]
            upd = jnp.where(blockdiag, _dot(lhs, xs2), 0.0)
            return jnp.exp(alast_row) * s_ref[d, p] + upd

        def backward():
            for p in range(SSD_PAIRS):
                sb_ref[c, p] = s_ref[1, p].astype(BF16)
                s_ref[1, p] = state_update(p, 1)

        def forward():
            roll_c = pltpu.roll(cm, SSD_STATE, 1)
            dsk = dsk_ref[...]
            ys = []
            for p in range(SSD_PAIRS):
                g = (2 * p) // (SSD_HEADS // SSD_GROUPS)
                cg_only = jnp.where((lane < SSD_STATE) == (g == 0), cm, 0.0).astype(BF16)
                cb = lax.dot_general(cg_only, bm.astype(BF16), NT_DIMS, preferred_element_type=F32)
                ms = []
                for hh in range(2):
                    hf = 2 * p + hh
                    hb = SSD_HEADS + hf
                    lf = (jnp.exp(jnp.where(lower, acc[:, hf:hf + 1] - act[hf:hf + 1, :], -jnp.inf))
                          * dtt[hf:hf + 1, :])
                    lb = (jnp.exp(jnp.where(upper, acc[:, hb:hb + 1] - act[hb:hb + 1, :], -jnp.inf))
                          * dtt[hb:hb + 1, :])
                    ms.append((cb * (lf + lb) + jnp.where(ri == ci, dsk[:, hf:hf + 1], 0.0)).astype(BF16))
                xs2 = xs[:, 2 * SSD_HEAD_DIM * p:2 * SSD_HEAD_DIM * (p + 1)]
                zero = jnp.zeros_like(xs2)
                rhs = jnp.concatenate([jnp.where(first_half_l, xs2, zero), jnp.where(first_half_l, zero, xs2)],
                                      axis=0)
                y = _dot(jnp.concatenate(ms, axis=1), rhs)
                cdup = jnp.where(first_half_l == (g == 0), cm, roll_c)
                ef, _ = pair_vals(acc, act, 2 * p)
                eb, _ = pair_vals(acc, act, SSD_HEADS + 2 * p)
                lhs_off = jnp.concatenate([cdup * jnp.exp(ef), cdup * jnp.exp(eb)], axis=1).astype(BF16)
                rhs_off = jnp.concatenate([s_ref[0, p].astype(BF16), sb_ref[c, p]], axis=0)
                ys.append(y + _dot(lhs_off, rhs_off))
                s_ref[0, p] = state_update(p, 0)
            y = jnp.concatenate(ys, axis=1)
            zf = z_ref[0].astype(F32)
            gt = y * _silu(zf)
            nw = nw_ref[...]
            gw = SSD_INNER // SSD_GROUPS
            outs = [_rms(gt[:, gw * g:gw * (g + 1)], nw[:, gw * g:gw * (g + 1)]) for g in range(SSD_GROUPS)]
            o_ref[0] = jnp.concatenate(outs, axis=1).astype(BF16)

        return backward, forward

    def backward_states():
        xs, bc, rows, cols, bt = chunk_values()
        cx_ref[c] = xs
        cbc_ref[c] = bc
        crow_ref[c] = rows
        ccol_ref[c] = cols
        cbt_ref[c] = bt
        bodies(xs, bc, rows, cols, bt)[0]()

    def forward_and_output():
        bodies(cx_ref[c], cbc_ref[c], crow_ref[c], ccol_ref[c], cbt_ref[c])[1]()

    return backward_states, forward_and_output


def _ssd(z, xbc, dt, lw):
    b, nt, _ = z.shape
    q = SSD_CHUNK
    nck = nt // q
    nctx = TILE // q
    rows8 = q // 8
    nb = 1
    chunk = functools.partial(_ssd_chunk_of, nck=nck, nctx=nctx)
    full = lambda a: pl.BlockSpec(a.shape, lambda bi, ph, i: (0,) * a.ndim)
    ws = [lw["conv_w"], lw["conv_b"], lw["alog"], lw["dtb"], lw["dsk"], lw["ssd_norm"]]
    kern = functools.partial(_ssd_kernel, nck=nck, nctx=nctx, nb=nb)
    return pl.pallas_call(
        kern,
        grid=(b // nb, 2, nck),
        in_specs=[pl.BlockSpec((nb, q, SSD_CONV_DIM), lambda bi, ph, i: (bi, chunk(ph, i), 0)),
                  pl.BlockSpec((nb, 8, SSD_CONV_DIM),
                               lambda bi, ph, i: (bi, jnp.maximum(chunk(ph, i) * rows8 - 1, 0), 0)),
                  pl.BlockSpec((nb, 8, SSD_CONV_DIM),
                               lambda bi, ph, i: (bi, jnp.minimum((chunk(ph, i) + 1) * rows8, nck * rows8 - 1), 0)),
                  pl.BlockSpec((nb, q, LANES), lambda bi, ph, i: (bi, chunk(ph, i), 0)),
                  pl.BlockSpec((nb, q, SSD_INNER), lambda bi, ph, i: (bi, chunk(ph, i), 0))]
                 + [full(a) for a in ws],
        out_specs=pl.BlockSpec((nb, q, SSD_INNER), lambda bi, ph, i: (bi, jnp.where(ph == 0, 0, i), 0)),
        out_shape=jax.ShapeDtypeStruct((b, nt, SSD_INNER), BF16),
        scratch_shapes=[pltpu.VMEM((nb, 2, SSD_PAIRS, 2 * SSD_STATE, 2 * SSD_HEAD_DIM), F32),
                        pltpu.VMEM((nb, nck, SSD_PAIRS, 2 * SSD_STATE, 2 * SSD_HEAD_DIM), BF16),
                        pltpu.VMEM((nb, nck, q, SSD_INNER), BF16),
                        pltpu.VMEM((nb, nck, q, 2 * SSD_BC), F32),
                        pltpu.VMEM((nb, nck, 4 * SSD_HEADS, q), F32),
                        pltpu.VMEM((nb, nck, q, LANES), F32),
                        pltpu.VMEM((nb, nck, SSD_BC, q), F32)],
        compiler_params=_cparams(("arbitrary", "arbitrary", "arbitrary"), VMEM_LIMIT),
        name="ssd",
    )(xbc, xbc, xbc, dt, z, *ws)


ACC_ROWS = 80
DIFF_C_EXP = (DIFF_QK ** -0.5) * math.log2(math.e)
MLA_C_EXP = ((MLA_NOPE + MLA_ROPE) ** -0.5) * math.log2(math.e)


def _ones_rows(tk):
    return (lax.broadcasted_iota(I32, (ACC_ROWS - DIFF_V, tk), 0) == 0).astype(BF16)


def _score_step(kqs, s_ref):
    for idx, (k, q) in enumerate(kqs):
        s_ref[idx, 0:k.shape[0], :] = _dot(k, q)


def _softmax_pv_step(n_keys, vaugs, s_ref, m_ref, acc_ref):
    for idx in range(len(vaugs)):
        s = s_ref[idx, 0:n_keys, :]
        m = m_ref[idx]
        mn = jnp.maximum(m, jnp.max(s, axis=0, keepdims=True))
        p = jnp.exp2(s - mn).astype(BF16)
        acc_ref[idx] = acc_ref[idx] * jnp.exp2(m - mn) + _dot(vaugs[idx], p)
        m_ref[idx] = mn


def _plain_pv_step(n_keys, vaugs, s_ref, acc_ref):
    for idx in range(len(vaugs)):
        p = jnp.exp2(s_ref[idx, 0:n_keys, :]).astype(BF16)
        acc_ref[idx] += _dot(vaugs[idx], p)


def _plain_keys(kq_fn, v_fn, acc_ref, nti, group):
    def run(steps):
        work = [(c0, n, idx) for c0, n in steps for idx in range(acc_ref.shape[0])]
        kqs = {}
        vaugs = {}

        def score(item):
            c0, n, idx = item
            if (c0, n) not in kqs:
                kqs[(c0, n)] = kq_fn(c0, n)
            k, q = kqs[(c0, n)][idx]
            return _dot(k, q)

        nxt = score(work[0])
        for pos, (c0, n, idx) in enumerate(work):
            s = nxt
            if pos + 1 < len(work):
                nxt = score(work[pos + 1])
            if (c0, n) not in vaugs:
                vaugs[(c0, n)] = v_fn(c0, n)
            acc_ref[idx] += _dot(vaugs[(c0, n)][idx], jnp.exp2(s).astype(BF16))

    latent = pl.program_id(1) > 0
    pl.when(jnp.logical_not(latent))(lambda: run([(0, 1)]))
    pl.when(latent)(lambda: run([(0, 1)] + [(1 + g * group, group) for g in range((nti - 1) // group)]))


def _attn_init(m_ref, acc_ref):
    m_ref[...] = jnp.full(m_ref.shape, -jnp.inf, F32)
    acc_ref[...] = jnp.zeros_like(acc_ref)


MAX_UNSHIFTED_SCORE = 96.0


def _key_abs_max(k_ref, kmax_ref):
    @pl.when(pl.program_id(1) == 0)
    def _():
        lead = k_ref.shape[1]
        nti = k_ref.shape[2]

        def body(t, best):
            for g in range(lead):
                best = jnp.maximum(best, jnp.max(jnp.abs(k_ref[0, g, t].astype(F32))))
            return best

        kmax_ref[0] = lax.fori_loop(0, nti, body, jnp.float32(0.0))


def _scores_are_bounded(q_ref, kmax_ref):
    q = jnp.abs(q_ref[0].astype(F32))
    return kmax_ref[0] * jnp.max(jnp.sum(q, axis=1)) <= MAX_UNSHIFTED_SCORE


DIFF_KEY_GROUP = 2
MLA_KEY_GROUP = 4


def _key_group(nti, want):
    n_lat = nti - 1
    assert n_lat % 2 == 0
    while n_lat % (2 * want):
        want //= 2
    return want


def _chunks_k(k_ref, lead, c0, n):
    return jnp.concatenate([k_ref[lead + (c0 + j,)] for j in range(n)], axis=0)


def _chunks_v(v_ref, lead, c0, n):
    v = jnp.concatenate([v_ref[lead + (c0 + j,)] for j in range(n)], axis=1)
    return jnp.concatenate([v, _ones_rows(n * TILE)], axis=0)


def _pipelined_keys(scores, consume, nti, group):
    steps = (nti - 1) // group
    first = lambda k: 1 + (k - 1) * group
    scores(0, 1, 0)
    latent = pl.program_id(1) > 0

    @pl.when(jnp.logical_not(latent))
    def _():
        consume(0, 1, 0)

    @pl.when(latent)
    def _():
        scores(first(1), group, 1)
        consume(0, 1, 0)
        scores(first(2), group, 0)
        consume(first(1), group, 1)

        def body(j, carry):
            k = 2 * j
            scores(first(k + 1), group, 1)
            consume(first(k), group, 0)
            scores(first(k + 2), group, 0)
            consume(first(k + 1), group, 1)
            return carry

        lax.fori_loop(1, steps // 2, body, 0)
        consume(first(steps), group, 0)


def _diff_attn_kernel(lq1_ref, lk1_ref, lq2_ref, lk2_ref, subw_ref, q_ref, k_ref, v_ref, o_ref,
                      m_ref, acc_ref, sa_ref, sb_ref, kmax_ref, *, nti, group, lambda_init):
    _attn_init(m_ref, acc_ref)
    _key_abs_max(k_ref, kmax_ref)
    slots = (sa_ref, sb_ref)

    def kq_pairs(c0, n):
        ks = [_chunks_k(k_ref, (0, g), c0, n) for g in range(DIFF_MAPS // MAPS_PER_TILE)]
        return [(ks[m // MAPS_PER_TILE], q_ref[0, m]) for m in range(DIFF_MAPS)]

    def values(c0, n):
        vaugs = []
        for h in range(DIFF_HEADS):
            vaugs += [_chunks_v(v_ref, (0, h), c0, n)] * 2
        return vaugs

    def scores(c0, n, slot):
        _score_step(kq_pairs(c0, n), slots[slot])

    def consume(c0, n, slot):
        _softmax_pv_step(n * TILE, values(c0, n), slots[slot], m_ref, acc_ref)

    bounded = _scores_are_bounded(q_ref, kmax_ref)
    pl.when(bounded)(lambda: _plain_keys(kq_pairs, values, acc_ref, nti, 4))
    pl.when(jnp.logical_not(bounded))(lambda: _pipelined_keys(scores, consume, nti, group))
    lam =(jnp.exp(jnp.sum(lq1_ref[...] * lk1_ref[...], keepdims=True))
           - jnp.exp(jnp.sum(lq2_ref[...] * lk2_ref[...], keepdims=True)) + lambda_init)
    for h in range(DIFF_HEADS):
        a1 = acc_ref[2 * h]
        a2 = acc_ref[2 * h + 1]
        o = a1[:DIFF_V] / a1[DIFF_V:DIFF_V + 1] - lam * (a2[:DIFF_V] / a2[DIFF_V:DIFF_V + 1])
        o = o * lax.rsqrt(jnp.mean(o * o, axis=0, keepdims=True) + EPS) * subw_ref[...]
        o_ref[0, h] = (o * (1.0 - lambda_init)).astype(BF16)


def _diff_attn(dq, dk, dv, lw, lambda_init):
    b, nmaps, _, nt = dq.shape
    nh = nmaps // 2
    nti = nt // TILE
    group = _key_group(nti, DIFF_KEY_GROUP)
    kern = functools.partial(_diff_attn_kernel, nti=nti, group=group, lambda_init=lambda_init)
    score_slot = pltpu.VMEM((2 * nh, group * TILE, TILE), F32)
    vec = pl.BlockSpec((1, DIFF_QK), lambda bi, i: (0, 0))
    return pl.pallas_call(
        kern,
        grid=(b, nti),
        in_specs=[vec, vec, vec, vec,
                  pl.BlockSpec((DIFF_V, 1), lambda bi, i: (0, 0)),
                  pl.BlockSpec((1, nmaps, LANES, TILE), lambda bi, i: (bi, 0, 0, i)),
                  pl.BlockSpec((1, nmaps // MAPS_PER_TILE, nti, TILE, LANES), lambda bi, i: (bi, 0, 0, 0, 0)),
                  pl.BlockSpec((1, nh, nti, DIFF_V, TILE), lambda bi, i: (bi, 0, 0, 0, 0))],
        out_specs=pl.BlockSpec((1, nh, DIFF_V, TILE), lambda bi, i: (bi, 0, 0, i)),
        out_shape=jax.ShapeDtypeStruct((b, nh, DIFF_V, nt), BF16),
        scratch_shapes=[pltpu.VMEM((2 * nh, 1, TILE), F32), pltpu.VMEM((2 * nh, ACC_ROWS, TILE), F32),
                        score_slot, score_slot, pltpu.SMEM((1,), F32)],
        compiler_params=_cparams(("arbitrary", "arbitrary"), VMEM_LIMIT),
        name="diff_attn",
    )(lw["lq1"], lw["lk1"], lw["lq2"], lw["lk2"], lw["subw"], dq, dk, dv)


def _mla_attn_kernel(q_ref, k_ref, v_ref, o_ref, m_ref, acc_ref, sa_ref, sb_ref, kmax_ref, *, nti, group):
    _attn_init(m_ref, acc_ref)
    _key_abs_max(k_ref, kmax_ref)
    slots = (sa_ref, sb_ref)

    def kq_pairs(c0, n):
        return [(_chunks_k(k_ref, (0, h), c0, n), q_ref[0, h]) for h in range(MLA_HEADS)]

    def values(c0, n):
        return [_chunks_v(v_ref, (0, h), c0, n) for h in range(MLA_HEADS)]

    def scores(c0, n, slot):
        _score_step(kq_pairs(c0, n), slots[slot])

    def consume(c0, n, slot):
        _softmax_pv_step(n * TILE, values(c0, n), slots[slot], m_ref, acc_ref)

    def consume_plain(c0, n, slot):
        _plain_pv_step(n * TILE, values(c0, n), slots[slot], acc_ref)

    bounded = _scores_are_bounded(q_ref, kmax_ref)
    pl.when(bounded)(lambda: _pipelined_keys(scores, consume_plain, nti, group))
    pl.when(jnp.logical_not(bounded))(lambda: _pipelined_keys(scores, consume, nti, group))
    for h in range(MLA_HEADS):
        a = acc_ref[h]
        o_ref[0, h] = (a[:MLA_V] / a[MLA_V:MLA_V + 1]).astype(BF16)


def _mla_attn(mq, mk, mv):
    b, nh, dpad, nt = mq.shape
    nti = nt // TILE
    group = _key_group(nti, MLA_KEY_GROUP)
    kern = functools.partial(_mla_attn_kernel, nti=nti, group=group)
    score_slot = pltpu.VMEM((nh, group * TILE, TILE), F32)
    return pl.pallas_call(
        kern,
        grid=(b, nti),
        in_specs=[pl.BlockSpec((1, nh, dpad, TILE), lambda bi, i: (bi, 0, 0, i)),
                  pl.BlockSpec((1, nh, nti, TILE, dpad), lambda bi, i: (bi, 0, 0, 0, 0)),
                  pl.BlockSpec((1, nh, nti, MLA_V, TILE), lambda bi, i: (bi, 0, 0, 0, 0))],
        out_specs=pl.BlockSpec((1, nh, MLA_V, TILE), lambda bi, i: (bi, 0, 0, i)),
        out_shape=jax.ShapeDtypeStruct((b, nh, MLA_V, nt), BF16),
        scratch_shapes=[pltpu.VMEM((nh, 1, TILE), F32), pltpu.VMEM((nh, ACC_ROWS, TILE), F32),
                        score_slot, score_slot, pltpu.SMEM((1,), F32)],
        compiler_params=_cparams(("arbitrary", "arbitrary"), VMEM_LIMIT),
        name="mla_attn",
    )(mq, mk, mv)


OUTPROJ_SPLIT = 2

def _outproj_kernel(t_ref, c_ref, s_ref, d_ref, a_ref, mod_ref, npost_ref, nffn_ref, ws_ref, wd_ref, wa_ref,
                    rwt_ref, rb_ref, tn_ref, hf_ref, aff_ref):
    mod = mod_ref[0, 0]
    t_in = _stream_tile(t_ref, c_ref)
    rwt = rwt_ref[...]
    rw_hi = rwt.astype(BF16)
    rw_lo = (rwt - rw_hi.astype(F32)).astype(BF16)
    rw_both = jnp.concatenate([rw_hi, rw_lo], axis=0)
    half = TILE // OUTPROJ_SPLIT
    for r in range(OUTPROJ_SPLIT):
        rows = slice(half * r, half * (r + 1))
        m = (_dot(s_ref[0, rows, :], ws_ref[...])
             + lax.dot_general(d_ref[0, :, rows], wd_ref[...], TN_DIMS, preferred_element_type=F32)
             + lax.dot_general(a_ref[0, :, rows], wa_ref[...], TN_DIMS, preferred_element_type=F32))
        tn = t_in[rows] + mod[2:3] * _rms(m, npost_ref[...])
        tn_ref[0, rows, :] = tn
        hf = _rms(tn, nffn_ref[...]) * (1.0 + mod[4:5]) + mod[3:4]
        hf_hi = hf.astype(BF16)
        hf_ref[0, rows, :] = hf_hi
        hf_lo = (hf - hf_hi.astype(F32)).astype(BF16)
        both = lax.dot_general(rw_both, hf_hi, NT_DIMS, preferred_element_type=F32)
        logits = (both[:N_EXPERTS] + both[N_EXPERTS:]
                  + lax.dot_general(rw_hi, hf_lo, NT_DIMS, preferred_element_type=F32) + rb_ref[...])
        e = jnp.exp(logits - jnp.max(logits, axis=0, keepdims=True))
        aff_ref[0, :, rows] = e / jnp.sum(e, axis=0, keepdims=True)


def _outproj(t, s, dt_, at_, mod, lw):
    streams, stream_specs, (b, nt, d) = _token_stream(t)
    nti = nt // TILE
    full = lambda a: pl.BlockSpec(a.shape, lambda bi, i: (0,) * a.ndim)
    ws = [lw["norm_mix_post"], lw["norm_ffn_pre"], lw["wo_s"], lw["wo_d"], lw["wo_a"], lw["rwt"], lw["rb"]]
    return pl.pallas_call(
        _outproj_kernel,
        grid=(b, nti),
        in_specs=stream_specs + [
                  pl.BlockSpec((1, TILE, SSD_INNER), lambda bi, i: (bi, i, 0)),
                  pl.BlockSpec((1, DIFF_HEADS * DIFF_V, TILE), lambda bi, i: (bi, 0, i)),
                  pl.BlockSpec((1, MLA_HEADS * MLA_V, TILE), lambda bi, i: (bi, 0, i)),
                  pl.BlockSpec((1, 1, N_MOD, d), lambda bi, i: (bi, jnp.minimum(i, 1), 0, 0))]
                 + [full(a) for a in ws],
        out_specs=[pl.BlockSpec((1, TILE, d), lambda bi, i: (bi, i, 0)),
                   pl.BlockSpec((1, TILE, d), lambda bi, i: (bi, i, 0)),
                   pl.BlockSpec((1, N_EXPERTS, TILE), lambda bi, i: (bi, 0, i))],
        out_shape=[jax.ShapeDtypeStruct((b, nt, d), F32),
                   jax.ShapeDtypeStruct((b, nt, d), BF16),
                   jax.ShapeDtypeStruct((b, N_EXPERTS, nt), F32)],
        compiler_params=_cparams(("arbitrary", "arbitrary"), VMEM_LIMIT),
        name="outproj",
    )(*streams, s, dt_, at_, mod, *ws)


def _route_kernel(aff_ref, pos_ref, gate_ref, cum_ref, *, nti, caps):
    ne = N_EXPERTS
    tri = (lax.broadcasted_iota(I32, (TILE, TILE), 0) < lax.broadcasted_iota(I32, (TILE, TILE), 1)).astype(BF16)
    lane = lax.broadcasted_iota(I32, (ne, LANES), 1)

    def excl_prefix(mask_f):
        return _dot(mask_f.astype(BF16), tri)

    cum_vec = jnp.zeros((ne, LANES), F32)
    total = jnp.zeros((ne, 1), F32)
    seg_bounds = ((0, 1, caps[0]), (1, nti, caps[1]))
    for t0, t1, cap in seg_bounds:
        xi = aff_ref[0, :, t0 * TILE:t1 * TILE]

        def bit_step(j, thr_bits, xi=xi, cap=cap):
            cand = thr_bits | (1 << (29 - j))
            cnt = jnp.sum((xi >= pltpu.bitcast(cand, F32)).astype(F32), axis=1, keepdims=True)
            return jnp.where(cnt >= cap, cand, thr_bits)

        thr = pltpu.bitcast(lax.fori_loop(0, 30, bit_step, jnp.zeros((ne, 1), I32)), F32)
        need = cap - jnp.sum((xi > thr).astype(F32), axis=1, keepdims=True)
        eq_seen = jnp.zeros((ne, 1), F32)
        for t in range(t0, t1):
            lo = (t - t0) * TILE
            xt = xi[:, lo:lo + TILE]
            eq = (xt == thr).astype(F32)
            eq_rank = eq_seen + excl_prefix(eq)
            sel = jnp.where(xt > thr, 1.0, eq * (eq_rank < need).astype(F32))
            eq_seen = eq_seen + jnp.sum(eq, axis=1, keepdims=True)
            rank = total + excl_prefix(sel)
            pos_ref[0, :, t * TILE:(t + 1) * TILE] = jnp.where(sel > 0.0, rank, -1.0).astype(I32)
            gate_ref[0, :, t * TILE:(t + 1) * TILE] = sel * aff_ref[0, :, t * TILE:(t + 1) * TILE]
            cum_vec = jnp.where(lane == t, total, cum_vec)
            total = total + jnp.sum(sel, axis=1, keepdims=True)
    cum_vec = jnp.where(lane == nti, total, cum_vec)
    cum_ref[0] = cum_vec.astype(I32)


def _route(aff, caps):
    b, ne, nt = aff.shape
    nti = nt // TILE
    kern = functools.partial(_route_kernel, nti=nti, caps=caps)
    return pl.pallas_call(
        kern,
        grid=(b,),
        in_specs=[pl.BlockSpec((1, ne, nt), lambda bi: (bi, 0, 0))],
        out_specs=[pl.BlockSpec((1, ne, nt), lambda bi: (bi, 0, 0)),
                   pl.BlockSpec((1, ne, nt), lambda bi: (bi, 0, 0)),
                   pl.BlockSpec((1, ne, LANES), lambda bi: (bi, 0, 0))],
        out_shape=[jax.ShapeDtypeStruct((b, ne, nt), I32),
                   jax.ShapeDtypeStruct((b, ne, nt), F32),
                   jax.ShapeDtypeStruct((b, ne, LANES), I32)],
        compiler_params=_cparams(("arbitrary",)),
        name="route",
    )(aff)


WIN = 64
GROUP = 4


def _tile_windows(cum_ref, b, t, rows):
    los = []
    rounds = jnp.int32(1)
    for e in range(N_EXPERTS):
        base = (b * N_EXPERTS + e) * LANES
        lo = (cum_ref[base + t] // 16) * 16
        los.append(lo)
        rounds = jnp.maximum(rounds, (cum_ref[base + t + 1] - lo + WIN - 1) // WIN)
    return los, rounds


def _window_onehot(pos_row, lo, r, rows):
    want = lo + WIN * r
    w0 = pl.multiple_of(jnp.minimum(want, rows - WIN), 16)
    rowid = w0 + lax.broadcasted_iota(I32, (WIN, TILE), 0)
    return w0, jnp.logical_and(pos_row == rowid, rowid >= want).astype(F32)


def _gather_kernel(cum_ref, hf_ref, pos_ref, gate_ref, xg_ref, gc_ref, *, rows):
    b = pl.program_id(0)
    t = pl.program_id(1)

    @pl.when(t == 0)
    def _():
        xg_ref[...] = jnp.zeros_like(xg_ref)
        gc_ref[...] = jnp.zeros_like(gc_ref)

    los, rounds = _tile_windows(cum_ref, b, t, rows)

    def round_step(r, carry):
        w0s, hots = [], []
        for e in range(N_EXPERTS):
            w0, hot = _window_onehot(pos_ref[0, e:e + 1, :], los[e], r, rows)
            w0s.append(w0)
            hots.append(hot)
            gc_ref[0, e, pl.ds(w0, WIN), :] += jnp.sum(hot * gate_ref[0, e:e + 1, :], axis=1, keepdims=True)
        res = _dot(jnp.concatenate(hots, axis=0).astype(BF16), hf_ref[0])
        for e in range(N_EXPERTS):
            xg_ref[0, e, pl.ds(w0s[e], WIN), :] += res[WIN * e:WIN * (e + 1)].astype(BF16)
        return carry

    lax.fori_loop(0, rounds, round_step, 0)


def _gather(cum_flat, hf, pos, gate, rows):
    b, nt, d = hf.shape
    nti = nt // TILE
    ne = pos.shape[1]
    kern = functools.partial(_gather_kernel, rows=rows)
    grid_spec = pltpu.PrefetchScalarGridSpec(
        num_scalar_prefetch=1,
        grid=(b, nti),
        in_specs=[pl.BlockSpec((1, TILE, d), lambda bi, i, cum: (bi, i, 0)),
                  pl.BlockSpec((1, ne, TILE), lambda bi, i, cum: (bi, 0, i)),
                  pl.BlockSpec((1, ne, TILE), lambda bi, i, cum: (bi, 0, i))],
        out_specs=[pl.BlockSpec((1, ne, rows, d), lambda bi, i, cum: (bi, 0, 0, 0)),
                   pl.BlockSpec((1, ne, rows, 1), lambda bi, i, cum: (bi, 0, 0, 0))],
    )
    return pl.pallas_call(
        kern,
        grid_spec=grid_spec,
        out_shape=[jax.ShapeDtypeStruct((b, ne, rows, d), BF16), jax.ShapeDtypeStruct((b, ne, rows, 1), F32)],
        compiler_params=_cparams(("arbitrary", "arbitrary"), VMEM_LIMIT),
        name="gather",
    )(cum_flat, hf, pos, gate)


def _experts_kernel(xg_ref, gc_ref, wg_ref, wu_ref, wd_ref, y_ref, wgb_ref, wub_ref, wdb_ref):
    @pl.when(pl.program_id(1) == 0)
    def _():
        wgb_ref[...] = wg_ref[0, 0].astype(BF16)
        wub_ref[...] = wu_ref[0, 0].astype(BF16)
        wdb_ref[...] = wd_ref[0, 0].astype(BF16)

    nb, _, rows, _ = xg_ref.shape
    half = rows // 2
    for bb in range(nb):
        for r in range(2):
            sl = slice(half * r, half * (r + 1))
            xg = xg_ref[bb, 0, sl, :]
            hid = (_silu(_dot(xg, wgb_ref[...])) * _dot(xg, wub_ref[...])).astype(BF16)
            y_ref[bb, 0, sl, :] = (_dot(hid, wdb_ref[...]) * gc_ref[bb, 0, sl, :]).astype(BF16)


def _experts(xg, gc, lw, l):
    b, ne, rows, d = xg.shape
    ff = lw["w_gate"].shape[3]
    nb = 2 if b % 2 == 0 else 1
    return pl.pallas_call(
        _experts_kernel,
        grid=(ne, b // nb),
        in_specs=[pl.BlockSpec((nb, 1, rows, d), lambda e, bi: (bi, e, 0, 0)),
                  pl.BlockSpec((nb, 1, rows, 1), lambda e, bi: (bi, e, 0, 0)),
                  pl.BlockSpec((1, 1, d, ff), lambda e, bi: (l, e, 0, 0)),
                  pl.BlockSpec((1, 1, d, ff), lambda e, bi: (l, e, 0, 0)),
                  pl.BlockSpec((1, 1, ff, d), lambda e, bi: (l, e, 0, 0))],
        out_specs=pl.BlockSpec((nb, 1, rows, d), lambda e, bi: (bi, e, 0, 0)),
        out_shape=jax.ShapeDtypeStruct((b, ne, rows, d), BF16),
        scratch_shapes=[pltpu.VMEM((d, ff), BF16), pltpu.VMEM((d, ff), BF16), pltpu.VMEM((ff, d), BF16)],
        compiler_params=_cparams(("arbitrary", "arbitrary"), VMEM_LIMIT),
        name="experts",
    )(xg, gc, lw["w_gate"], lw["w_up"], lw["w_down"])


def _combine_kernel(cum_ref, t_ref, y_ref, pos_ref, mod_ref, npost_ref, o_ref, f_ref, *, rows, latent_only):
    b = pl.program_id(0)
    t = pl.program_id(1)

    def run():
        los, rounds = _tile_windows(cum_ref, b, t, rows)

        def scatter_round(r):
            total = None
            for g in range(N_EXPERTS // GROUP):
                hots, wins = [], []
                for e in range(GROUP * g, GROUP * (g + 1)):
                    w0, hot = _window_onehot(pos_ref[0, e:e + 1, :], los[e], r, rows)
                    hots.append(hot)
                    wins.append(y_ref[0, e, pl.ds(w0, WIN), :])
                hot = jnp.concatenate(hots, axis=0).astype(BF16)
                part = lax.dot_general(hot, jnp.concatenate(wins, axis=0), TN_DIMS, preferred_element_type=F32)
                total = part if total is None else total + part
            return total

        f_ref[...] = scatter_round(0)

        def round_step(r, carry):
            f_ref[...] += scatter_round(r)
            return carry

        lax.fori_loop(1, rounds, round_step, 0)
        mod = mod_ref[0, 0]
        o_ref[0] = t_ref[0] + mod[5:6] * _rms(f_ref[...], npost_ref[...])

    if latent_only:
        pl.when(t > 0)(run)
    else:
        run()


def _combine(cum_flat, t, y, pos, mod, lw, latent_only):
    b, nt, d = t.shape
    nti = nt // TILE
    ne, rows = y.shape[1], y.shape[2]
    kern = functools.partial(_combine_kernel, rows=rows, latent_only=latent_only)
    if latent_only:
        out_rows, out_map = nt - TILE, lambda bi, i, cum: (bi, jnp.maximum(i - 1, 0), 0)
    else:
        out_rows, out_map = nt, lambda bi, i, cum: (bi, i, 0)
    grid_spec = pltpu.PrefetchScalarGridSpec(
        num_scalar_prefetch=1,
        grid=(b, nti),
        in_specs=[pl.BlockSpec((1, TILE, d), lambda bi, i, cum: (bi, i, 0)),
                  pl.BlockSpec((1, ne, rows, d), lambda bi, i, cum: (bi, 0, 0, 0)),
                  pl.BlockSpec((1, ne, TILE), lambda bi, i, cum: (bi, 0, i)),
                  pl.BlockSpec((1, 1, N_MOD, d), lambda bi, i, cum: (bi, jnp.minimum(i, 1), 0, 0)),
                  pl.BlockSpec((1, d), lambda bi, i, cum: (0, 0))],
        out_specs=pl.BlockSpec((1, TILE, d), out_map),
        scratch_shapes=[pltpu.VMEM((TILE, d), F32)],
    )
    return pl.pallas_call(
        kern,
        grid_spec=grid_spec,
        out_shape=jax.ShapeDtypeStruct((b, out_rows, d), F32),
        compiler_params=_cparams(("arbitrary", "arbitrary"), VMEM_LIMIT),
        name="combine",
    )(cum_flat, t, y, pos, mod, lw["norm_ffn_post"])


def _rope_tables(seq, ctx):
    quarter = MLA_ROPE // 4
    inv = ROPE_BASE ** (-jnp.arange(quarter, dtype=F32) / quarter)
    n_rows = seq // GRID_W
    rows = jnp.repeat(jnp.arange(n_rows, dtype=F32), GRID_W)
    cols = jnp.tile(jnp.arange(GRID_W, dtype=F32), n_rows)
    ar = rows[:, None] * inv[None, :]
    ac = cols[:, None] * inv[None, :]
    cos = jnp.concatenate([jnp.cos(ar), jnp.cos(ar), jnp.cos(ac), jnp.cos(ac)], axis=1)
    sin = jnp.concatenate([-jnp.sin(ar), jnp.sin(ar), -jnp.sin(ac), jnp.sin(ac)], axis=1)
    cos = jnp.concatenate([jnp.ones((ctx, MLA_ROPE), F32), cos], axis=0)
    sin = jnp.concatenate([jnp.zeros((ctx, MLA_ROPE), F32), sin], axis=0)
    return {"ck": jnp.tile(cos, (1, DIFF_MAPS)), "sk": jnp.tile(sin, (1, DIFF_MAPS)), "ct": cos.T, "st": sin.T}


def _partner_perm(width):
    idx = jnp.arange(width)
    r = idx % 16
    return jnp.where(r < 8, idx + 8, idx - 8)


def _layer_weights(l, p):
    d = p["w_in"].shape[1]
    w_in = p["w_in"][l]
    o_diff = 2 * SSD_INNER + 2 * SSD_BC + 2 * SSD_HEADS
    o_mla = o_diff + 3 * DIFF_HEADS * DIFF_V
    nk = DIFF_MAPS * DIFF_QK
    w_ssd = w_in[:, :o_diff]
    wa = jnp.concatenate([w_ssd, jnp.zeros((d, LANES - 2 * SSD_HEADS), F32)], axis=1)
    wq = w_in[:, o_diff:o_diff + nk]
    wk = w_in[:, o_diff + nk:o_diff + 2 * nk]
    wv = w_in[:, o_diff + 2 * nk:o_mla]
    wcq =w_in[:, o_mla:o_mla + MLA_Q_LORA]
    wckv = w_in[:, o_mla + MLA_Q_LORA:o_mla + MLA_Q_LORA + MLA_KV_LORA]
    wkr = w_in[:, o_mla + MLA_Q_LORA + MLA_KV_LORA:]
    zeros = lambda n: jnp.zeros((d, n), F32)
    wm = jnp.concatenate([wcq, zeros(256 - MLA_Q_LORA), wckv, wkr, wkr[:, _partner_perm(MLA_ROPE)],
                          zeros(512 - 448)], axis=1)

    wqu = p["mla_w_q_up"][l].reshape(MLA_Q_LORA, MLA_HEADS, MLA_NOPE + MLA_ROPE)
    pad = jnp.zeros((MLA_Q_LORA, MLA_HEADS, MLA_QK_PAD - MLA_NOPE - MLA_ROPE), F32)
    wqu_plain = jnp.concatenate([wqu, pad], axis=2).reshape(MLA_Q_LORA, -1)
    wkvu = p["mla_w_kv_up"][l].reshape(MLA_KV_LORA, MLA_HEADS, MLA_NOPE + MLA_V)
    wk2 = jnp.concatenate([wkvu[:, :, :MLA_NOPE],
                           jnp.zeros((MLA_KV_LORA, MLA_HEADS, MLA_QK_PAD - MLA_NOPE), F32)],
                          axis=2).reshape(MLA_KV_LORA, -1)
    eye = jnp.eye(MLA_ROPE, dtype=F32)
    ek_h = jnp.concatenate([jnp.zeros((MLA_ROPE, MLA_NOPE), F32), eye,
                            jnp.zeros((MLA_ROPE, MLA_QK_PAD - MLA_NOPE - MLA_ROPE), F32)], axis=1)
    ek = jnp.tile(ek_h, (1, MLA_HEADS))
    wv2 = wkvu[:, :, MLA_NOPE:].reshape(MLA_KV_LORA, -1)

    w_out = p["w_out"][l]
    row = lambda a: a.reshape(1, -1)
    col = lambda a: a.reshape(-1, 1)
    return {
        "norm_mix_pre": row(p["norm_mix_pre"][l]), "norm_mix_post": row(p["norm_mix_post"][l]),
        "norm_ffn_pre": row(p["norm_ffn_pre"][l]), "norm_ffn_post": row(p["norm_ffn_post"][l]),
        "wa": wa.astype(BF16),
        "wdk": wk.astype(BF16),
        "wm": wm.astype(BF16),
        "wqt": wq.T.astype(BF16),
        "wvt": wv.T.astype(BF16),
        "qnw": row(p["mla_q_norm"][l]), "kvnw": row(p["mla_kv_norm"][l]),
        "wqut": wqu_plain.T.astype(BF16),
        "wk2": wk2.astype(BF16), "ek": ek.astype(BF16), "wvt2": wv2.T.astype(BF16),
        "conv_w": p["ssd_conv_w"][l], "conv_b": row(p["ssd_conv_b"][l]),
        "alog": col(p["ssd_a_log"][l]), "dtb": col(p["ssd_dt_bias"][l]),
        "dsk": row(p["ssd_d"][l]), "ssd_norm": row(p["ssd_norm"][l]),
        "lq1": row(p["diff_lam_q1"][l]), "lk1": row(p["diff_lam_k1"][l]),
        "lq2": row(p["diff_lam_q2"][l]), "lk2": row(p["diff_lam_k2"][l]),
        "subw": p["diff_subln"][l].reshape(-1, 1),
        "wo_s": w_out[:SSD_INNER].astype(BF16),
        "wo_d": w_out[SSD_INNER:SSD_INNER + DIFF_HEADS * DIFF_V].astype(BF16),
        "wo_a": w_out[SSD_INNER + DIFF_HEADS * DIFF_V:].astype(BF16),
        "rwt": p["router_w"][l].T, "rb": p["router_b"][l].reshape(-1, 1),
        "w_gate": p["w_gate"], "w_up": p["w_up"], "w_down": p["w_down"],
    }


def kernel(x, c, ctx, c_ctx, ada_w, ada_b, norm_mix_pre, norm_mix_post, norm_ffn_pre, norm_ffn_post, w_in, ssd_conv_w, ssd_conv_b, ssd_a_log, ssd_dt_bias, ssd_d, ssd_norm, diff_lam_q1, diff_lam_k1, diff_lam_q2, diff_lam_k2, diff_subln, mla_q_norm, mla_w_q_up, mla_kv_norm, mla_w_kv_up, w_out, router_w, router_b, w_gate, w_up, w_down):
    p = dict(norm_mix_pre=norm_mix_pre, norm_mix_post=norm_mix_post, norm_ffn_pre=norm_ffn_pre,
             norm_ffn_post=norm_ffn_post, w_in=w_in, ssd_conv_w=ssd_conv_w, ssd_conv_b=ssd_conv_b,
             ssd_a_log=ssd_a_log, ssd_dt_bias=ssd_dt_bias, ssd_d=ssd_d, ssd_norm=ssd_norm,
             diff_lam_q1=diff_lam_q1, diff_lam_k1=diff_lam_k1, diff_lam_q2=diff_lam_q2, diff_lam_k2=diff_lam_k2,
             diff_subln=diff_subln, mla_q_norm=mla_q_norm, mla_w_q_up=mla_w_q_up, mla_kv_norm=mla_kv_norm,
             mla_w_kv_up=mla_w_kv_up, w_out=w_out, router_w=router_w, router_b=router_b,
             w_gate=w_gate, w_up=w_up, w_down=w_down)
    b, seq, d = x.shape
    nctx = ctx.shape[1]
    depth = ada_w.shape[0]
    assert nctx == TILE and seq % TILE == 0 and seq % GRID_W == 0
    nt = nctx + seq
    caps = (EC_CAPACITY * nctx // N_EXPERTS, EC_CAPACITY * seq // N_EXPERTS)
    assert caps[0] % 16 == 0 and caps[1] % 16 == 0 and caps[0] + caps[1] >= WIN

    cvec = jnp.concatenate([c, c_ctx[None, :], jnp.zeros((8 - b - 1, d), F32)], axis=0)
    mods = _adaln(cvec, ada_w, ada_b).reshape(depth, 8, N_MOD, d)
    tabs = _rope_tables(seq, nctx)
    t = (x, ctx)
    for l in range(depth):
        lw = _layer_weights(l, p)
        lambda_init = 0.8 - 0.6 * math.exp(-0.3 * l)
        mod = jnp.stack([jnp.broadcast_to(mods[l, b], (b, N_MOD, d)), mods[l, :b]], axis=1)
        z, xbc, dt, dq, dk, dv, mq, mk, mv = _inproj(t, mod, lw, tabs)
        s = _ssd(z, xbc, dt, lw)
        da = _diff_attn(dq, dk, dv, lw, lambda_init).reshape(b, DIFF_HEADS * DIFF_V, nt)
        aa = _mla_attn(mq, mk, mv).reshape(b, MLA_HEADS * MLA_V, nt)
        t, hf, aff = _outproj(t, s, da, aa, mod, lw)
        pos, gate, cum = _route(aff, caps)
        cum_flat = cum.reshape(-1)
        xg, gc = _gather(cum_flat, hf, pos, gate, caps[0] + caps[1])
        y = _experts(xg, gc, lw, l)
        t = _combine(cum_flat, t, y, pos, mod, lw, latent_only=(l == depth - 1))
    return t
```

```python
import functools
import math

import jax
import jax.numpy as jnp
from jax import lax
from jax.experimental import pallas as pl
from jax.experimental.pallas import tpu as pltpu

F32 = jnp.float32
BF16 = jnp.bfloat16
I32 = jnp.int32
HIGHEST = lax.Precision.HIGHEST

EPS = 1e-6
GRID_W = 64
ROPE_BASE = 10000.0
N_MOD = 6

SSD_HEADS = 8
SSD_HEAD_DIM = 64
SSD_INNER = SSD_HEADS * SSD_HEAD_DIM
SSD_GROUPS = 2
SSD_STATE = 64
SSD_CHUNK = 128
SSD_BC = SSD_GROUPS * SSD_STATE
SSD_CONV_DIM = SSD_INNER + 2 * SSD_BC
SSD_PAIRS = SSD_HEADS // 2

DIFF_HEADS = 4
DIFF_QK = 32
DIFF_V = 64
DIFF_MAPS = 2 * DIFF_HEADS

MLA_HEADS = 4
MLA_Q_LORA = 192
MLA_KV_LORA = 128
MLA_NOPE = 64
MLA_ROPE = 32
MLA_V = 64
MLA_QK_PAD = 128

N_EXPERTS = 16
EC_CAPACITY = 2

TILE = 256
LANES = 128
MAPS_PER_TILE = LANES // DIFF_QK
VMEM_LIMIT = 56 * 1024 * 1024

NT_DIMS = (((1,), (1,)), ((), ()))
TN_DIMS = (((0,), (0,)), ((), ()))


def _cparams(sem, vmem=None):
    return pltpu.CompilerParams(dimension_semantics=sem, vmem_limit_bytes=vmem)


def _rms(x, w):
    return x * lax.rsqrt(jnp.mean(x * x, axis=-1, keepdims=True) + EPS) * w


def _silu(x):
    return x * jax.nn.sigmoid(x)


def _dot(a, b):
    return jnp.dot(a, b, preferred_element_type=F32)


def _adaln_kernel(c_ref, w_ref, b_ref, o_ref):
    s = _silu(c_ref[...])
    o_ref[0] = lax.dot_general(s, w_ref[0], (((1,), (0,)), ((), ())), precision=HIGHEST,
                               preferred_element_type=F32) + b_ref[0]


def _adaln(cvec, ada_w, ada_b):
    depth, d, nd = ada_w.shape
    rows = cvec.shape[0]
    return pl.pallas_call(
        _adaln_kernel,
        grid=(depth, nd // d),
        in_specs=[pl.BlockSpec((rows, d), lambda l, j: (0, 0)),
                  pl.BlockSpec((1, d, d), lambda l, j: (l, 0, j)),
                  pl.BlockSpec((1, 1, d), lambda l, j: (l, 0, j))],
        out_specs=pl.BlockSpec((1, rows, d), lambda l, j: (l, 0, j)),
        out_shape=jax.ShapeDtypeStruct((depth, rows, nd), F32),
        compiler_params=_cparams(("arbitrary", "arbitrary")),
        name="adaln",
    )(cvec, ada_w, ada_b.reshape(depth, 1, nd))


def _token_stream(t):
    if isinstance(t, tuple):
        x, ctx = t
        b, seq, d = x.shape
        specs = [pl.BlockSpec((1, TILE, d), lambda bi, i: (bi, jnp.maximum(i - 1, 0), 0)),
                 pl.BlockSpec((1, TILE, d), lambda bi, i: (bi, 0, 0))]
        return (x, ctx), specs, (b, seq + ctx.shape[1], d)
    b, nt, d = t.shape
    specs = [pl.BlockSpec((1, TILE, d), lambda bi, i: (bi, i, 0)),
             pl.BlockSpec((1, TILE, d), lambda bi, i: (bi, 0, 0))]
    return (t, t), specs, (b, nt, d)


def _stream_tile(x_ref, c_ref):
    return jnp.where(pl.program_id(1) == 0, c_ref[0], x_ref[0])


def _partner_rows(x):
    parts = []
    for g in range(0, x.shape[0], 16):
        parts += [x[g + 8:g + 16], x[g:g + 8]]
    return jnp.concatenate(parts, axis=0)


def _partner_lanes(x):
    width = x.shape[1]
    lane = lax.broadcasted_iota(I32, x.shape, 1)
    return jnp.where((lane & 8) == 0, pltpu.roll(x, width - 8, 1), pltpu.roll(x, 8, 1))


def _inproj_kernel(x_ref, c_ref, mod_ref, nw_ref, wa_ref, wdk_ref, wm_ref, wqt_ref, wvt_ref,
                   ck_ref, sk_ref, ct_ref, st_ref, qnw_ref, kvnw_ref, wqut_ref, wk2_ref, ek_ref, wvt2_ref,
                   z_ref, xbc_ref, dt_ref, dq_ref, dk_ref, dv_ref, mq_ref, mk_ref, mv_ref):
    x = _stream_tile(x_ref, c_ref)
    mod = mod_ref[0, 0]
    h = (_rms(x, nw_ref[...]) * (1.0 + mod[1:2]) + mod[0:1]).astype(BF16)

    ra = _dot(h, wa_ref[...])
    z_ref[0] = ra[:, :SSD_INNER].astype(BF16)
    xbc_ref[0] = ra[:, SSD_INNER:SSD_INNER + SSD_CONV_DIM]
    dt_ref[0] = ra[:, SSD_INNER + SSD_CONV_DIM:]

    rk = _dot(h, wdk_ref[...])
    k = (rk * ck_ref[...] + _partner_lanes(rk) * sk_ref[...]).astype(BF16)
    for g in range(DIFF_MAPS // MAPS_PER_TILE):
        dk_ref[0, g, 0] = k[:, LANES * g:LANES * (g + 1)]

    ct = ct_ref[...]
    st = st_ref[...]
    rq = lax.dot_general(wqt_ref[...], h, NT_DIMS, preferred_element_type=F32)
    rq_partner = _partner_rows(rq)
    for m in range(DIFF_MAPS):
        lo = DIFF_QK * m
        qm = ((rq[lo:lo + DIFF_QK] * ct + rq_partner[lo:lo + DIFF_QK] * st) * DIFF_C_EXP).astype(BF16)
        above = DIFF_QK * (m % MAPS_PER_TILE)
        below = LANES - above - DIFF_QK
        parts = ([jnp.zeros((above, qm.shape[1]), BF16)] if above else []) + [qm]
        parts += [jnp.zeros((below, qm.shape[1]), BF16)] if below else []
        dq_ref[0, m] = jnp.concatenate(parts, axis=0)

    rv = lax.dot_general(wvt_ref[...], h, NT_DIMS, preferred_element_type=F32).astype(BF16)
    for hd in range(DIFF_HEADS):
        dv_ref[0, hd, 0] = rv[DIFF_V * hd:DIFF_V * (hd + 1)]

    rm = _dot(h, wm_ref[...])
    cq = _rms(rm[:, :MLA_Q_LORA], qnw_ref[...]).astype(BF16)
    ckv = _rms(rm[:, 256:256 + MLA_KV_LORA], kvnw_ref[...]).astype(BF16)
    kr = rm[:, 384:384 + MLA_ROPE] * ck_ref[:, :MLA_ROPE] + rm[:, 416:416 + MLA_ROPE] * sk_ref[:, :MLA_ROPE]

    rq2 = lax.dot_general(wqut_ref[...], cq, NT_DIMS, preferred_element_type=F32)
    rq2_partner = _partner_rows(rq2)
    ones = jnp.ones((MLA_NOPE, ct.shape[1]), F32)
    pad1 = jnp.ones((MLA_QK_PAD - MLA_NOPE - MLA_ROPE, ct.shape[1]), F32)
    ct_h = jnp.concatenate([ones, ct, pad1], axis=0)
    st_h = jnp.concatenate([0.0 * ones, st, 0.0 * pad1], axis=0)
    for hd in range(MLA_HEADS):
        lo = MLA_QK_PAD * hd
        qh = rq2[lo:lo + MLA_QK_PAD] * ct_h + rq2_partner[lo:lo + MLA_QK_PAD] * st_h
        mq_ref[0, hd] = (qh * MLA_C_EXP).astype(BF16)

    k2 = (_dot(ckv, wk2_ref[...]) + _dot(kr.astype(BF16), ek_ref[...])).astype(BF16)
    for hd in range(MLA_HEADS):
        mk_ref[0, hd, 0] = k2[:, MLA_QK_PAD * hd:MLA_QK_PAD * (hd + 1)]
    rv2 = lax.dot_general(wvt2_ref[...], ckv, NT_DIMS, preferred_element_type=F32).astype(BF16)
    for hd in range(MLA_HEADS):
        mv_ref[0, hd, 0] = rv2[MLA_V * hd:MLA_V * (hd + 1)]


def _inproj(t, mod, lw, tabs):
    streams, stream_specs, (b, nt, d) = _token_stream(t)
    nti = nt // TILE
    full = lambda a: pl.BlockSpec(a.shape, lambda bi, i: (0,) * a.ndim)
    tok = lambda w: pl.BlockSpec((TILE, w), lambda bi, i: (i, 0))
    tokt = lambda w: pl.BlockSpec((w, TILE), lambda bi, i: (0, i))
    ws = [lw["norm_mix_pre"], lw["wa"], lw["wdk"], lw["wm"], lw["wqt"], lw["wvt"]]
    ws2 = [lw["qnw"], lw["kvnw"], lw["wqut"], lw["wk2"], lw["ek"], lw["wvt2"]]
    out_shape = [
        jax.ShapeDtypeStruct((b, nt, SSD_INNER), BF16),
        jax.ShapeDtypeStruct((b, nt, SSD_CONV_DIM), F32),
        jax.ShapeDtypeStruct((b, nt, LANES), F32),
        jax.ShapeDtypeStruct((b, DIFF_MAPS, LANES, nt), BF16),
        jax.ShapeDtypeStruct((b, DIFF_MAPS // MAPS_PER_TILE, nti, TILE, LANES), BF16),
        jax.ShapeDtypeStruct((b, DIFF_HEADS, nti, DIFF_V, TILE), BF16),
        jax.ShapeDtypeStruct((b, MLA_HEADS, MLA_QK_PAD, nt), BF16),
        jax.ShapeDtypeStruct((b, MLA_HEADS, nti, TILE, MLA_QK_PAD), BF16),
        jax.ShapeDtypeStruct((b, MLA_HEADS, nti, MLA_V, TILE), BF16),
    ]
    out_specs = [
        pl.BlockSpec((1, TILE, SSD_INNER), lambda bi, i: (bi, i, 0)),
        pl.BlockSpec((1, TILE, SSD_CONV_DIM), lambda bi, i: (bi, i, 0)),
        pl.BlockSpec((1, TILE, LANES), lambda bi, i: (bi, i, 0)),
        pl.BlockSpec((1, DIFF_MAPS, LANES, TILE), lambda bi, i: (bi, 0, 0, i)),
        pl.BlockSpec((1, DIFF_MAPS // MAPS_PER_TILE, 1, TILE, LANES), lambda bi, i: (bi, 0, i, 0, 0)),
        pl.BlockSpec((1, DIFF_HEADS, 1, DIFF_V, TILE), lambda bi, i: (bi, 0, i, 0, 0)),
        pl.BlockSpec((1, MLA_HEADS, MLA_QK_PAD, TILE), lambda bi, i: (bi, 0, 0, i)),
        pl.BlockSpec((1, MLA_HEADS, 1, TILE, MLA_QK_PAD), lambda bi, i: (bi, 0, i, 0, 0)),
        pl.BlockSpec((1, MLA_HEADS, 1, MLA_V, TILE), lambda bi, i: (bi, 0, i, 0, 0)),
    ]
    in_specs = (stream_specs
                + [pl.BlockSpec((1, 1, N_MOD, d), lambda bi, i: (bi, jnp.minimum(i, 1), 0, 0))]
                + [full(a) for a in ws]
                + [tok(DIFF_MAPS * DIFF_QK), tok(DIFF_MAPS * DIFF_QK), tokt(DIFF_QK), tokt(DIFF_QK)]
                + [full(a) for a in ws2])
    return pl.pallas_call(
        _inproj_kernel,
        grid=(b, nti),
        in_specs=in_specs,
        out_specs=out_specs,
        out_shape=out_shape,
        compiler_params=_cparams(("arbitrary", "arbitrary"), VMEM_LIMIT),
        name="inproj",
    )(*streams, mod, *ws, tabs["ck"], tabs["sk"], tabs["ct"], tabs["st"], *ws2)


def _ssd_chunk_of(ph, i, nck, nctx):
    back = jnp.where(i < nctx, nctx - 1 - i, nck - 1 + nctx - i)
    return jnp.where(ph == 0, back, i)


def _ssd_kernel(xc_ref, xp_ref, xn_ref, dt_ref, z_ref, cw_ref, cb_ref, alog_ref, dtb_ref, dsk_ref, nw_ref,
                o_ref, s_ref, sb_ref, *cache, nck, nctx, nb):
    ph = pl.program_id(1)
    i = pl.program_id(2)
    c = _ssd_chunk_of(ph, i, nck, nctx)

    @pl.when(i == 0)
    def _():
        s_ref[...] = jnp.zeros_like(s_ref)

    one = lambda ref, bb: ref.at[pl.ds(bb, 1)]
    fns = [_ssd_sample(one(xc_ref, bb), one(xp_ref, bb), one(xn_ref, bb), one(dt_ref, bb), one(z_ref, bb),
                       cw_ref, cb_ref, alog_ref, dtb_ref, dsk_ref, nw_ref, one(o_ref, bb),
                       s_ref.at[bb], sb_ref.at[bb], [r.at[bb] for r in cache], c, nck=nck, nctx=nctx)
           for bb in range(nb)]

    @pl.when(ph == 0)
    def _():
        for backward_states, _ in fns:
            backward_states()

    @pl.when(ph == 1)
    def _():
        for _, forward_and_output in fns:
            forward_and_output()


def _ssd_sample(xc_ref, xp_ref, xn_ref, dt_ref, z_ref, cw_ref, cb_ref, alog_ref, dtb_ref, dsk_ref, nw_ref,
                o_ref, s_ref, sb_ref, cache, c, *, nck, nctx):
    q = SSD_CHUNK
    nh2 = 2 * SSD_HEADS
    cx_ref, cbc_ref, crow_ref, ccol_ref, cbt_ref = cache
    ri = lax.broadcasted_iota(I32, (q, q), 0)
    ci = lax.broadcasted_iota(I32, (q, q), 1)
    lower = ci <= ri
    upper = ci >= ri
    lane = ci
    first_half_s = ri < SSD_STATE
    first_half_l = lane < SSD_HEAD_DIM
    blockdiag = first_half_s == first_half_l

    def chunk_values():
        x = xc_ref[0]
        has_prev = jnp.logical_and(c != 0, c != nctx)
        has_next = jnp.logical_and(c != nctx - 1, c != nck - 1)
        prev_row = jnp.where(has_prev, xp_ref[0][7:8, :], 0.0)
        next_row = jnp.where(has_next, xn_ref[0][0:1, :], 0.0)
        row = lax.broadcasted_iota(I32, x.shape, 0)
        xm1 = jnp.where(row == 0, prev_row, pltpu.roll(x, 1, 0))
        xp1 = jnp.where(row == q - 1, next_row, pltpu.roll(x, q - 1, 0))
        cw = cw_ref[...]
        u = _silu(xm1 * cw[0:1] + x * cw[1:2] + xp1 * cw[2:3] + cb_ref[...])
        xs = u[:, :SSD_INNER].astype(BF16)
        bc = u[:, SSD_INNER:]
        xdt = dt_ref[0].T[:nh2] + dtb_ref[...]
        dtt = jnp.maximum(xdt, 0.0) + jnp.log1p(jnp.exp(-jnp.abs(xdt)))
        dat = dtt * (-jnp.exp(alog_ref[...]))
        tri_dims = (((1,), (0,)), ((), ()))
        acf = lax.dot_general(dat, upper.astype(F32), tri_dims, precision=HIGHEST, preferred_element_type=F32)
        acb = lax.dot_general(dat, lower.astype(F32), tri_dims, precision=HIGHEST, preferred_element_type=F32)
        act = jnp.where(ri[:nh2] < SSD_HEADS, acf, acb)
        rows = jnp.concatenate([dtt, act], axis=0)
        cols = jnp.concatenate([rows, jnp.zeros((q - 2 * nh2, q), F32)], axis=0).T
        bt = bc[:, :SSD_BC].T
        return xs, bc, rows, cols, bt

    def bodies(xs, bc, rows, cols, bt):
        bm = bc[:, :SSD_BC]
        cm = bc[:, SSD_BC:]
        dtt = rows[:nh2]
        act = rows[nh2:]
        acc = pltpu.roll(cols, LANES - nh2, 1)
        fwd_rows = lax.broadcasted_iota(I32, (nh2, 1), 0) < SSD_HEADS
        alast = jnp.where(fwd_rows, act[:, q - 1:q], act[:, 0:1])
        w_rows = jnp.exp(alast - act) * dtt
        decay_all = jnp.exp(alast)

        def pair_cols(arr_c, h0):
            return jnp.where(first_half_l, arr_c[:, h0:h0 + 1], arr_c[:, h0 + 1:h0 + 2])

        def pair_rows(arr_t, h0):
            return jnp.where(first_half_s[:, 0:arr_t.shape[1]], arr_t[h0:h0 + 1, :], arr_t[h0 + 1:h0 + 2, :])

        def state_update(p, d):
            g = (2 * p) // (SSD_HEADS // SSD_GROUPS)
            h0 = d * SSD_HEADS + 2 * p
            btg = bt[SSD_STATE * g:SSD_STATE * (g + 1)]
            lhs = (jnp.concatenate([btg, btg], axis=0) * pair_rows(w_rows, h0)).astype(BF16)
            xs2 = xs[:, 2 * SSD_HEAD_DIM * p:2 * SSD_HEAD_DIM * (p + 1)]
            upd = jnp.where(blockdiag, _dot(lhs, xs2), 0.0)
            return pair_rows(decay_all, h0) * s_ref[d, p] + upd

        def backward():
            for p in range(SSD_PAIRS):
                sb_ref[c, p] = s_ref[1, p].astype(BF16)
                s_ref[1, p] = state_update(p, 1)

        def forward():
            roll_c = pltpu.roll(cm, SSD_STATE, 1)
            dsk = dsk_ref[...]
            ys = []
            for p in range(SSD_PAIRS):
                g = (2 * p) // (SSD_HEADS // SSD_GROUPS)
                cg_only = jnp.where((lane < SSD_STATE) == (g == 0), cm, 0.0).astype(BF16)
                cb = lax.dot_general(cg_only, bm.astype(BF16), NT_DIMS, preferred_element_type=F32)
                ms = []
                for hh in range(2):
                    hf = 2 * p + hh
                    hb = SSD_HEADS + hf
                    lf = (jnp.exp(jnp.where(lower, acc[:, hf:hf + 1] - act[hf:hf + 1, :], -jnp.inf))
                          * dtt[hf:hf + 1, :])
                    lb = (jnp.exp(jnp.where(upper, acc[:, hb:hb + 1] - act[hb:hb + 1, :], -jnp.inf))
                          * dtt[hb:hb + 1, :])
                    ms.append((cb * (lf + lb) + jnp.where(ri == ci, dsk[:, hf:hf + 1], 0.0)).astype(BF16))
                xs2 = xs[:, 2 * SSD_HEAD_DIM * p:2 * SSD_HEAD_DIM * (p + 1)]
                zero = jnp.zeros_like(xs2)
                rhs = jnp.concatenate([jnp.where(first_half_l, xs2, zero), jnp.where(first_half_l, zero, xs2)],
                                      axis=0)
                y = _dot(jnp.concatenate(ms, axis=1), rhs)
                cdup = jnp.where(first_half_l == (g == 0), cm, roll_c)
                ef = jnp.exp(pair_cols(acc, 2 * p))
                eb = jnp.exp(pair_cols(acc, SSD_HEADS + 2 * p))
                lhs_off = jnp.concatenate([cdup * ef, cdup * eb], axis=1).astype(BF16)
                rhs_off = jnp.concatenate([s_ref[0, p].astype(BF16), sb_ref[c, p]], axis=0)
                ys.append(y + _dot(lhs_off, rhs_off))
                s_ref[0, p] = state_update(p, 0)
            y = jnp.concatenate(ys, axis=1)
            zf = z_ref[0].astype(F32)
            gt = y * _silu(zf)
            nw = nw_ref[...]
            gw = SSD_INNER // SSD_GROUPS
            outs = [_rms(gt[:, gw * g:gw * (g + 1)], nw[:, gw * g:gw * (g + 1)]) for g in range(SSD_GROUPS)]
            o_ref[0] = jnp.concatenate(outs, axis=1).astype(BF16)

        return backward, forward

    def backward_states():
        xs, bc, rows, cols, bt = chunk_values()
        cx_ref[c] = xs
        cbc_ref[c] = bc
        crow_ref[c] = rows
        ccol_ref[c] = cols
        cbt_ref[c] = bt
        bodies(xs, bc, rows, cols, bt)[0]()

    def forward_and_output():
        bodies(cx_ref[c], cbc_ref[c], crow_ref[c], ccol_ref[c], cbt_ref[c])[1]()

    return backward_states, forward_and_output


def _ssd(z, xbc, dt, lw):
    b, nt, _ = z.shape
    q = SSD_CHUNK
    nck = nt // q
    nctx = TILE // q
    rows8 = q // 8
    nb = 1
    chunk = functools.partial(_ssd_chunk_of, nck=nck, nctx=nctx)
    full = lambda a: pl.BlockSpec(a.shape, lambda bi, ph, i: (0,) * a.ndim)
    ws = [lw["conv_w"], lw["conv_b"], lw["alog"], lw["dtb"], lw["dsk"], lw["ssd_norm"]]
    kern = functools.partial(_ssd_kernel, nck=nck, nctx=nctx, nb=nb)
    return pl.pallas_call(
        kern,
        grid=(b // nb, 2, nck),
        in_specs=[pl.BlockSpec((nb, q, SSD_CONV_DIM), lambda bi, ph, i: (bi, chunk(ph, i), 0)),
                  pl.BlockSpec((nb, 8, SSD_CONV_DIM),
                               lambda bi, ph, i: (bi, jnp.maximum(chunk(ph, i) * rows8 - 1, 0), 0)),
                  pl.BlockSpec((nb, 8, SSD_CONV_DIM),
                               lambda bi, ph, i: (bi, jnp.minimum((chunk(ph, i) + 1) * rows8, nck * rows8 - 1), 0)),
                  pl.BlockSpec((nb, q, LANES), lambda bi, ph, i: (bi, chunk(ph, i), 0)),
                  pl.BlockSpec((nb, q, SSD_INNER), lambda bi, ph, i: (bi, chunk(ph, i), 0))]
                 + [full(a) for a in ws],
        out_specs=pl.BlockSpec((nb, q, SSD_INNER), lambda bi, ph, i: (bi, jnp.where(ph == 0, 0, i), 0)),
        out_shape=jax.ShapeDtypeStruct((b, nt, SSD_INNER), BF16),
        scratch_shapes=[pltpu.VMEM((nb, 2, SSD_PAIRS, 2 * SSD_STATE, 2 * SSD_HEAD_DIM), F32),
                        pltpu.VMEM((nb, nck, SSD_PAIRS, 2 * SSD_STATE, 2 * SSD_HEAD_DIM), BF16),
                        pltpu.VMEM((nb, nck, q, SSD_INNER), BF16),
                        pltpu.VMEM((nb, nck, q, 2 * SSD_BC), F32),
                        pltpu.VMEM((nb, nck, 4 * SSD_HEADS, q), F32),
                        pltpu.VMEM((nb, nck, q, LANES), F32),
                        pltpu.VMEM((nb, nck, SSD_BC, q), F32)],
        compiler_params=_cparams(("arbitrary", "arbitrary", "arbitrary"), VMEM_LIMIT),
        name="ssd",
    )(xbc, xbc, xbc, dt, z, *ws)


ACC_ROWS = 80
DIFF_C_EXP = (DIFF_QK ** -0.5) * math.log2(math.e)
MLA_C_EXP = ((MLA_NOPE + MLA_ROPE) ** -0.5) * math.log2(math.e)


def _ones_rows(tk):
    return (lax.broadcasted_iota(I32, (ACC_ROWS - DIFF_V, tk), 0) == 0).astype(BF16)


def _score_step(kqs, s_ref):
    for idx, (k, q) in enumerate(kqs):
        s_ref[idx, 0:k.shape[0], :] = _dot(k, q)


def _softmax_pv_step(n_keys, vaugs, s_ref, m_ref, acc_ref):
    for idx in range(len(vaugs)):
        s = s_ref[idx, 0:n_keys, :]
        m = m_ref[idx]
        mn = jnp.maximum(m, jnp.max(s, axis=0, keepdims=True))
        p = jnp.exp2(s - mn).astype(BF16)
        acc_ref[idx] = acc_ref[idx] * jnp.exp2(m - mn) + _dot(vaugs[idx], p)
        m_ref[idx] = mn


def _plain_pv_step(n_keys, vaugs, s_ref, acc_ref):
    for idx in range(len(vaugs)):
        p = jnp.exp2(s_ref[idx, 0:n_keys, :]).astype(BF16)
        acc_ref[idx] += _dot(vaugs[idx], p)


def _plain_keys(kq_fn, v_fn, acc_ref, nti, group):
    def run(steps):
        work = [(c0, n, idx) for c0, n in steps for idx in range(acc_ref.shape[0])]
        kqs = {}
        vaugs = {}

        def score(item):
            c0, n, idx = item
            if (c0, n) not in kqs:
                kqs[(c0, n)] = kq_fn(c0, n)
            k, q = kqs[(c0, n)][idx]
            return _dot(k, q)

        nxt = score(work[0])
        for pos, (c0, n, idx) in enumerate(work):
            s = nxt
            if pos + 1 < len(work):
                nxt = score(work[pos + 1])
            if (c0, n) not in vaugs:
                vaugs[(c0, n)] = v_fn(c0, n)
            acc_ref[idx] += _dot(vaugs[(c0, n)][idx], jnp.exp2(s).astype(BF16))

    latent = pl.program_id(1) > 0
    pl.when(jnp.logical_not(latent))(lambda: run([(0, 1)]))
    pl.when(latent)(lambda: run([(0, 1)] + [(1 + g * group, group) for g in range((nti - 1) // group)]))


def _attn_init(m_ref, acc_ref):
    m_ref[...] = jnp.full(m_ref.shape, -jnp.inf, F32)
    acc_ref[...] = jnp.zeros_like(acc_ref)


MAX_UNSHIFTED_SCORE = 96.0


def _key_abs_max(k_ref, kmax_ref):
    @pl.when(pl.program_id(1) == 0)
    def _():
        lead = k_ref.shape[1]
        nti = k_ref.shape[2]

        def body(t, best):
            for g in range(lead):
                best = jnp.maximum(best, jnp.max(jnp.abs(k_ref[0, g, t].astype(F32))))
            return best

        kmax_ref[0] = lax.fori_loop(0, nti, body, jnp.float32(0.0))


def _scores_are_bounded(q_ref, kmax_ref):
    q = jnp.abs(q_ref[0].astype(F32))
    return kmax_ref[0] * jnp.max(jnp.sum(q, axis=1)) <= MAX_UNSHIFTED_SCORE


DIFF_KEY_GROUP = 2
MLA_KEY_GROUP = 4


def _key_group(nti, want):
    n_lat = nti - 1
    assert n_lat % 2 == 0
    while n_lat % (2 * want):
        want //= 2
    return want


def _chunks_k(k_ref, lead, c0, n):
    return jnp.concatenate([k_ref[lead + (c0 + j,)] for j in range(n)], axis=0)


def _chunks_v(v_ref, lead, c0, n):
    v = jnp.concatenate([v_ref[lead + (c0 + j,)] for j in range(n)], axis=1)
    return jnp.concatenate([v, _ones_rows(n * TILE)], axis=0)


def _pipelined_keys(scores, consume, nti, group):
    steps = (nti - 1) // group
    first = lambda k: 1 + (k - 1) * group
    scores(0, 1, 0)
    latent = pl.program_id(1) > 0

    @pl.when(jnp.logical_not(latent))
    def _():
        consume(0, 1, 0)

    @pl.when(latent)
    def _():
        scores(first(1), group, 1)
        consume(0, 1, 0)
        scores(first(2), group, 0)
        consume(first(1), group, 1)

        def body(j, carry):
            k = 2 * j
            scores(first(k + 1), group, 1)
            consume(first(k), group, 0)
            scores(first(k + 2), group, 0)
            consume(first(k + 1), group, 1)
            return carry

        lax.fori_loop(1, steps // 2, body, 0)
        consume(first(steps), group, 0)


def _diff_attn_kernel(lq1_ref, lk1_ref, lq2_ref, lk2_ref, subw_ref, q_ref, k_ref, v_ref, o_ref,
                      m_ref, acc_ref, sa_ref, sb_ref, kmax_ref, *, nti, group, lambda_init):
    _attn_init(m_ref, acc_ref)
    _key_abs_max(k_ref, kmax_ref)
    slots = (sa_ref, sb_ref)

    def kq_pairs(c0, n):
        ks = [_chunks_k(k_ref, (0, g), c0, n) for g in range(DIFF_MAPS // MAPS_PER_TILE)]
        return [(ks[m // MAPS_PER_TILE], q_ref[0, m]) for m in range(DIFF_MAPS)]

    def values(c0, n):
        vaugs = []
        for h in range(DIFF_HEADS):
            vaugs += [_chunks_v(v_ref, (0, h), c0, n)] * 2
        return vaugs

    def scores(c0, n, slot):
        _score_step(kq_pairs(c0, n), slots[slot])

    def consume(c0, n, slot):
        _softmax_pv_step(n * TILE, values(c0, n), slots[slot], m_ref, acc_ref)

    bounded = _scores_are_bounded(q_ref, kmax_ref)
    pl.when(bounded)(lambda: _plain_keys(kq_pairs, values, acc_ref, nti, 4))
    pl.when(jnp.logical_not(bounded))(lambda: _pipelined_keys(scores, consume, nti, group))
    lam =(jnp.exp(jnp.sum(lq1_ref[...] * lk1_ref[...], keepdims=True))
           - jnp.exp(jnp.sum(lq2_ref[...] * lk2_ref[...], keepdims=True)) + lambda_init)
    for h in range(DIFF_HEADS):
        a1 = acc_ref[2 * h]
        a2 = acc_ref[2 * h + 1]
        o = a1[:DIFF_V] / a1[DIFF_V:DIFF_V + 1] - lam * (a2[:DIFF_V] / a2[DIFF_V:DIFF_V + 1])
        o = o * lax.rsqrt(jnp.mean(o * o, axis=0, keepdims=True) + EPS) * subw_ref[...]
        o_ref[0, h] = (o * (1.0 - lambda_init)).astype(BF16)


def _diff_attn(dq, dk, dv, lw, lambda_init):
    b, nmaps, _, nt = dq.shape
    nh = nmaps // 2
    nti = nt // TILE
    group = _key_group(nti, DIFF_KEY_GROUP)
    kern = functools.partial(_diff_attn_kernel, nti=nti, group=group, lambda_init=lambda_init)
    score_slot = pltpu.VMEM((2 * nh, group * TILE, TILE), F32)
    vec = pl.BlockSpec((1, DIFF_QK), lambda bi, i: (0, 0))
    return pl.pallas_call(
        kern,
        grid=(b, nti),
        in_specs=[vec, vec, vec, vec,
                  pl.BlockSpec((DIFF_V, 1), lambda bi, i: (0, 0)),
                  pl.BlockSpec((1, nmaps, LANES, TILE), lambda bi, i: (bi, 0, 0, i)),
                  pl.BlockSpec((1, nmaps // MAPS_PER_TILE, nti, TILE, LANES), lambda bi, i: (bi, 0, 0, 0, 0)),
                  pl.BlockSpec((1, nh, nti, DIFF_V, TILE), lambda bi, i: (bi, 0, 0, 0, 0))],
        out_specs=pl.BlockSpec((1, nh, DIFF_V, TILE), lambda bi, i: (bi, 0, 0, i)),
        out_shape=jax.ShapeDtypeStruct((b, nh, DIFF_V, nt), BF16),
        scratch_shapes=[pltpu.VMEM((2 * nh, 1, TILE), F32), pltpu.VMEM((2 * nh, ACC_ROWS, TILE), F32),
                        score_slot, score_slot, pltpu.SMEM((1,), F32)],
        compiler_params=_cparams(("arbitrary", "arbitrary"), VMEM_LIMIT),
        name="diff_attn",
    )(lw["lq1"], lw["lk1"], lw["lq2"], lw["lk2"], lw["subw"], dq, dk, dv)


def _mla_attn_kernel(q_ref, k_ref, v_ref, o_ref, m_ref, acc_ref, sa_ref, sb_ref, kmax_ref, *, nti, group):
    _attn_init(m_ref, acc_ref)
    _key_abs_max(k_ref, kmax_ref)
    slots = (sa_ref, sb_ref)

    def kq_pairs(c0, n):
        return [(_chunks_k(k_ref, (0, h), c0, n), q_ref[0, h]) for h in range(MLA_HEADS)]

    def values(c0, n):
        return [_chunks_v(v_ref, (0, h), c0, n) for h in range(MLA_HEADS)]

    def scores(c0, n, slot):
        _score_step(kq_pairs(c0, n), slots[slot])

    def consume(c0, n, slot):
        _softmax_pv_step(n * TILE, values(c0, n), slots[slot], m_ref, acc_ref)

    def consume_plain(c0, n, slot):
        _plain_pv_step(n * TILE, values(c0, n), slots[slot], acc_ref)

    bounded = _scores_are_bounded(q_ref, kmax_ref)
    pl.when(bounded)(lambda: _pipelined_keys(scores, consume_plain, nti, group))
    pl.when(jnp.logical_not(bounded))(lambda: _pipelined_keys(scores, consume, nti, group))
    for h in range(MLA_HEADS):
        a = acc_ref[h]
        o_ref[0, h] = (a[:MLA_V] / a[MLA_V:MLA_V + 1]).astype(BF16)


def _mla_attn(mq, mk, mv):
    b, nh, dpad, nt = mq.shape
    nti = nt // TILE
    group = _key_group(nti, MLA_KEY_GROUP)
    kern = functools.partial(_mla_attn_kernel, nti=nti, group=group)
    score_slot = pltpu.VMEM((nh, group * TILE, TILE), F32)
    return pl.pallas_call(
        kern,
        grid=(b, nti),
        in_specs=[pl.BlockSpec((1, nh, dpad, TILE), lambda bi, i: (bi, 0, 0, i)),
                  pl.BlockSpec((1, nh, nti, TILE, dpad), lambda bi, i: (bi, 0, 0, 0, 0)),
                  pl.BlockSpec((1, nh, nti, MLA_V, TILE), lambda bi, i: (bi, 0, 0, 0, 0))],
        out_specs=pl.BlockSpec((1, nh, MLA_V, TILE), lambda bi, i: (bi, 0, 0, i)),
        out_shape=jax.ShapeDtypeStruct((b, nh, MLA_V, nt), BF16),
        scratch_shapes=[pltpu.VMEM((nh, 1, TILE), F32), pltpu.VMEM((nh, ACC_ROWS, TILE), F32),
                        score_slot, score_slot, pltpu.SMEM((1,), F32)],
        compiler_params=_cparams(("arbitrary", "arbitrary"), VMEM_LIMIT),
        name="mla_attn",
    )(mq, mk, mv)


OUTPROJ_SPLIT = 2

def _outproj_kernel(t_ref, c_ref, s_ref, d_ref, a_ref, mod_ref, npost_ref, nffn_ref, ws_ref, wd_ref, wa_ref,
                    rwt_ref, rb_ref, tn_ref, hf_ref, aff_ref):
    mod = mod_ref[0, 0]
    t_in = _stream_tile(t_ref, c_ref)
    rwt = rwt_ref[...]
    rw_hi = rwt.astype(BF16)
    rw_lo = (rwt - rw_hi.astype(F32)).astype(BF16)
    rw_both = jnp.concatenate([rw_hi, rw_lo], axis=0)
    half = TILE // OUTPROJ_SPLIT
    for r in range(OUTPROJ_SPLIT):
        rows = slice(half * r, half * (r + 1))
        m = (_dot(s_ref[0, rows, :], ws_ref[...])
             + lax.dot_general(d_ref[0, :, rows], wd_ref[...], TN_DIMS, preferred_element_type=F32)
             + lax.dot_general(a_ref[0, :, rows], wa_ref[...], TN_DIMS, preferred_element_type=F32))
        tn = t_in[rows] + mod[2:3] * _rms(m, npost_ref[...])
        tn_ref[0, rows, :] = tn
        hf = _rms(tn, nffn_ref[...]) * (1.0 + mod[4:5]) + mod[3:4]
        hf_hi = hf.astype(BF16)
        hf_ref[0, rows, :] = hf_hi
        hf_lo = (hf - hf_hi.astype(F32)).astype(BF16)
        both = lax.dot_general(rw_both, hf_hi, NT_DIMS, preferred_element_type=F32)
        logits = (both[:N_EXPERTS] + both[N_EXPERTS:]
                  + lax.dot_general(rw_hi, hf_lo, NT_DIMS, preferred_element_type=F32) + rb_ref[...])
        e = jnp.exp(logits - jnp.max(logits, axis=0, keepdims=True))
        aff_ref[0, :, rows] = e / jnp.sum(e, axis=0, keepdims=True)


def _outproj(t, s, dt_, at_, mod, lw):
    streams, stream_specs, (b, nt, d) = _token_stream(t)
    nti = nt // TILE
    full = lambda a: pl.BlockSpec(a.shape, lambda bi, i: (0,) * a.ndim)
    ws = [lw["norm_mix_post"], lw["norm_ffn_pre"], lw["wo_s"], lw["wo_d"], lw["wo_a"], lw["rwt"], lw["rb"]]
    return pl.pallas_call(
        _outproj_kernel,
        grid=(b, nti),
        in_specs=stream_specs + [
                  pl.BlockSpec((1, TILE, SSD_INNER), lambda bi, i: (bi, i, 0)),
                  pl.BlockSpec((1, DIFF_HEADS * DIFF_V, TILE), lambda bi, i: (bi, 0, i)),
                  pl.BlockSpec((1, MLA_HEADS * MLA_V, TILE), lambda bi, i: (bi, 0, i)),
                  pl.BlockSpec((1, 1, N_MOD, d), lambda bi, i: (bi, jnp.minimum(i, 1), 0, 0))]
                 + [full(a) for a in ws],
        out_specs=[pl.BlockSpec((1, TILE, d), lambda bi, i: (bi, i, 0)),
                   pl.BlockSpec((1, TILE, d), lambda bi, i: (bi, i, 0)),
                   pl.BlockSpec((1, N_EXPERTS, TILE), lambda bi, i: (bi, 0, i))],
        out_shape=[jax.ShapeDtypeStruct((b, nt, d), F32),
                   jax.ShapeDtypeStruct((b, nt, d), BF16),
                   jax.ShapeDtypeStruct((b, N_EXPERTS, nt), F32)],
        compiler_params=_cparams(("arbitrary", "arbitrary"), VMEM_LIMIT),
        name="outproj",
    )(*streams, s, dt_, at_, mod, *ws)


def _route_kernel(aff_ref, pos_ref, gate_ref, cum_ref, *, nti, caps):
    ne = N_EXPERTS
    tri = (lax.broadcasted_iota(I32, (TILE, TILE), 0) < lax.broadcasted_iota(I32, (TILE, TILE), 1)).astype(BF16)
    lane = lax.broadcasted_iota(I32, (ne, LANES), 1)

    def excl_prefix(mask_f):
        return _dot(mask_f.astype(BF16), tri)

    cum_vec = jnp.zeros((ne, LANES), F32)
    total = jnp.zeros((ne, 1), F32)
    seg_bounds = ((0, 1, caps[0]), (1, nti, caps[1]))
    for t0, t1, cap in seg_bounds:
        xi = aff_ref[0, :, t0 * TILE:t1 * TILE]

        def bit_step(j, thr_bits, xi=xi, cap=cap):
            cand = thr_bits | (1 << (29 - j))
            cnt = jnp.sum((xi >= pltpu.bitcast(cand, F32)).astype(F32), axis=1, keepdims=True)
            return jnp.where(cnt >= cap, cand, thr_bits)

        thr = pltpu.bitcast(lax.fori_loop(0, 30, bit_step, jnp.zeros((ne, 1), I32)), F32)
        need = cap - jnp.sum((xi > thr).astype(F32), axis=1, keepdims=True)
        eq_seen = jnp.zeros((ne, 1), F32)
        for t in range(t0, t1):
            lo = (t - t0) * TILE
            xt = xi[:, lo:lo + TILE]
            eq = (xt == thr).astype(F32)
            eq_rank = eq_seen + excl_prefix(eq)
            sel = jnp.where(xt > thr, 1.0, eq * (eq_rank < need).astype(F32))
            eq_seen = eq_seen + jnp.sum(eq, axis=1, keepdims=True)
            rank = total + excl_prefix(sel)
            pos_ref[0, :, t * TILE:(t + 1) * TILE] = jnp.where(sel > 0.0, rank, -1.0).astype(I32)
            gate_ref[0, :, t * TILE:(t + 1) * TILE] = sel * aff_ref[0, :, t * TILE:(t + 1) * TILE]
            cum_vec = jnp.where(lane == t, total, cum_vec)
            total = total + jnp.sum(sel, axis=1, keepdims=True)
    cum_vec = jnp.where(lane == nti, total, cum_vec)
    cum_ref[0] = cum_vec.astype(I32)


def _route(aff, caps):
    b, ne, nt = aff.shape
    nti = nt // TILE
    kern = functools.partial(_route_kernel, nti=nti, caps=caps)
    return pl.pallas_call(
        kern,
        grid=(b,),
        in_specs=[pl.BlockSpec((1, ne, nt), lambda bi: (bi, 0, 0))],
        out_specs=[pl.BlockSpec((1, ne, nt), lambda bi: (bi, 0, 0)),
                   pl.BlockSpec((1, ne, nt), lambda bi: (bi, 0, 0)),
                   pl.BlockSpec((1, ne, LANES), lambda bi: (bi, 0, 0))],
        out_shape=[jax.ShapeDtypeStruct((b, ne, nt), I32),
                   jax.ShapeDtypeStruct((b, ne, nt), F32),
                   jax.ShapeDtypeStruct((b, ne, LANES), I32)],
        compiler_params=_cparams(("arbitrary",)),
        name="route",
    )(aff)


WIN = 64
GROUP = 4


def _tile_windows(cum_ref, b, t, rows):
    los = []
    rounds = jnp.int32(1)
    for e in range(N_EXPERTS):
        base = (b * N_EXPERTS + e) * LANES
        lo = (cum_ref[base + t] // 16) * 16
        los.append(lo)
        rounds = jnp.maximum(rounds, (cum_ref[base + t + 1] - lo + WIN - 1) // WIN)
    return los, rounds


def _window_onehot(pos_row, lo, r, rows):
    want = lo + WIN * r
    w0 = pl.multiple_of(jnp.minimum(want, rows - WIN), 16)
    rowid = w0 + lax.broadcasted_iota(I32, (WIN, TILE), 0)
    return w0, jnp.logical_and(pos_row == rowid, rowid >= want).astype(F32)


def _gather_kernel(cum_ref, hf_ref, pos_ref, gate_ref, xg_ref, gc_ref, *, rows):
    b = pl.program_id(0)
    t = pl.program_id(1)

    @pl.when(t == 0)
    def _():
        xg_ref[...] = jnp.zeros_like(xg_ref)
        gc_ref[...] = jnp.zeros_like(gc_ref)

    los, rounds = _tile_windows(cum_ref, b, t, rows)

    def round_step(r, carry):
        w0s, hots = [], []
        for e in range(N_EXPERTS):
            w0, hot = _window_onehot(pos_ref[0, e:e + 1, :], los[e], r, rows)
            w0s.append(w0)
            hots.append(hot)
            gc_ref[0, e, pl.ds(w0, WIN), :] += jnp.sum(hot * gate_ref[0, e:e + 1, :], axis=1, keepdims=True)
        res = _dot(jnp.concatenate(hots, axis=0).astype(BF16), hf_ref[0])
        for e in range(N_EXPERTS):
            xg_ref[0, e, pl.ds(w0s[e], WIN), :] += res[WIN * e:WIN * (e + 1)].astype(BF16)
        return carry

    lax.fori_loop(0, rounds, round_step, 0)


def _gather(cum_flat, hf, pos, gate, rows):
    b, nt, d = hf.shape
    nti = nt // TILE
    ne = pos.shape[1]
    kern = functools.partial(_gather_kernel, rows=rows)
    grid_spec = pltpu.PrefetchScalarGridSpec(
        num_scalar_prefetch=1,
        grid=(b, nti),
        in_specs=[pl.BlockSpec((1, TILE, d), lambda bi, i, cum: (bi, i, 0)),
                  pl.BlockSpec((1, ne, TILE), lambda bi, i, cum: (bi, 0, i)),
                  pl.BlockSpec((1, ne, TILE), lambda bi, i, cum: (bi, 0, i))],
        out_specs=[pl.BlockSpec((1, ne, rows, d), lambda bi, i, cum: (bi, 0, 0, 0)),
                   pl.BlockSpec((1, ne, rows, 1), lambda bi, i, cum: (bi, 0, 0, 0))],
    )
    return pl.pallas_call(
        kern,
        grid_spec=grid_spec,
        out_shape=[jax.ShapeDtypeStruct((b, ne, rows, d), BF16), jax.ShapeDtypeStruct((b, ne, rows, 1), F32)],
        compiler_params=_cparams(("arbitrary", "arbitrary"), VMEM_LIMIT),
        name="gather",
    )(cum_flat, hf, pos, gate)


def _experts_kernel(xg_ref, gc_ref, wg_ref, wu_ref, wd_ref, y_ref, wgb_ref, wub_ref, wdb_ref):
    @pl.when(pl.program_id(1) == 0)
    def _():
        wgb_ref[...] = wg_ref[0, 0].astype(BF16)
        wub_ref[...] = wu_ref[0, 0].astype(BF16)
        wdb_ref[...] = wd_ref[0, 0].astype(BF16)

    nb, _, rows, _ = xg_ref.shape
    half = rows // 2
    for bb in range(nb):
        for r in range(2):
            sl = slice(half * r, half * (r + 1))
            xg = xg_ref[bb, 0, sl, :]
            hid = (_silu(_dot(xg, wgb_ref[...])) * _dot(xg, wub_ref[...])).astype(BF16)
            y_ref[bb, 0, sl, :] = (_dot(hid, wdb_ref[...]) * gc_ref[bb, 0, sl, :]).astype(BF16)


def _experts(xg, gc, lw, l):
    b, ne, rows, d = xg.shape
    ff = lw["w_gate"].shape[3]
    nb = 2 if b % 2 == 0 else 1
    return pl.pallas_call(
        _experts_kernel,
        grid=(ne, b // nb),
        in_specs=[pl.BlockSpec((nb, 1, rows, d), lambda e, bi: (bi, e, 0, 0)),
                  pl.BlockSpec((nb, 1, rows, 1), lambda e, bi: (bi, e, 0, 0)),
                  pl.BlockSpec((1, 1, d, ff), lambda e, bi: (l, e, 0, 0)),
                  pl.BlockSpec((1, 1, d, ff), lambda e, bi: (l, e, 0, 0)),
                  pl.BlockSpec((1, 1, ff, d), lambda e, bi: (l, e, 0, 0))],
        out_specs=pl.BlockSpec((nb, 1, rows, d), lambda e, bi: (bi, e, 0, 0)),
        out_shape=jax.ShapeDtypeStruct((b, ne, rows, d), BF16),
        scratch_shapes=[pltpu.VMEM((d, ff), BF16), pltpu.VMEM((d, ff), BF16), pltpu.VMEM((ff, d), BF16)],
        compiler_params=_cparams(("arbitrary", "arbitrary"), VMEM_LIMIT),
        name="experts",
    )(xg, gc, lw["w_gate"], lw["w_up"], lw["w_down"])


def _combine_kernel(cum_ref, t_ref, y_ref, pos_ref, mod_ref, npost_ref, o_ref, f_ref, *, rows, latent_only):
    b = pl.program_id(0)
    t = pl.program_id(1)

    def run():
        los, rounds = _tile_windows(cum_ref, b, t, rows)

        def scatter_round(r):
            total = None
            for g in range(N_EXPERTS // GROUP):
                hots, wins = [], []
                for e in range(GROUP * g, GROUP * (g + 1)):
                    w0, hot = _window_onehot(pos_ref[0, e:e + 1, :], los[e], r, rows)
                    hots.append(hot)
                    wins.append(y_ref[0, e, pl.ds(w0, WIN), :])
                hot = jnp.concatenate(hots, axis=0).astype(BF16)
                part = lax.dot_general(hot, jnp.concatenate(wins, axis=0), TN_DIMS, preferred_element_type=F32)
                total = part if total is None else total + part
            return total

        f_ref[...] = scatter_round(0)

        def round_step(r, carry):
            f_ref[...] += scatter_round(r)
            return carry

        lax.fori_loop(1, rounds, round_step, 0)
        mod = mod_ref[0, 0]
        o_ref[0] = t_ref[0] + mod[5:6] * _rms(f_ref[...], npost_ref[...])

    if latent_only:
        pl.when(t > 0)(run)
    else:
        run()


def _combine(cum_flat, t, y, pos, mod, lw, latent_only):
    b, nt, d = t.shape
    nti = nt // TILE
    ne, rows = y.shape[1], y.shape[2]
    kern = functools.partial(_combine_kernel, rows=rows, latent_only=latent_only)
    if latent_only:
        out_rows, out_map = nt - TILE, lambda bi, i, cum: (bi, jnp.maximum(i - 1, 0), 0)
    else:
        out_rows, out_map = nt, lambda bi, i, cum: (bi, i, 0)
    grid_spec = pltpu.PrefetchScalarGridSpec(
        num_scalar_prefetch=1,
        grid=(b, nti),
        in_specs=[pl.BlockSpec((1, TILE, d), lambda bi, i, cum: (bi, i, 0)),
                  pl.BlockSpec((1, ne, rows, d), lambda bi, i, cum: (bi, 0, 0, 0)),
                  pl.BlockSpec((1, ne, TILE), lambda bi, i, cum: (bi, 0, i)),
                  pl.BlockSpec((1, 1, N_MOD, d), lambda bi, i, cum: (bi, jnp.minimum(i, 1), 0, 0)),
                  pl.BlockSpec((1, d), lambda bi, i, cum: (0, 0))],
        out_specs=pl.BlockSpec((1, TILE, d), out_map),
        scratch_shapes=[pltpu.VMEM((TILE, d), F32)],
    )
    return pl.pallas_call(
        kern,
        grid_spec=grid_spec,
        out_shape=jax.ShapeDtypeStruct((b, out_rows, d), F32),
        compiler_params=_cparams(("arbitrary", "arbitrary"), VMEM_LIMIT),
        name="combine",
    )(cum_flat, t, y, pos, mod, lw["norm_ffn_post"])


def _rope_tables(seq, ctx):
    quarter = MLA_ROPE // 4
    inv = ROPE_BASE ** (-jnp.arange(quarter, dtype=F32) / quarter)
    n_rows = seq // GRID_W
    rows = jnp.repeat(jnp.arange(n_rows, dtype=F32), GRID_W)
    cols = jnp.tile(jnp.arange(GRID_W, dtype=F32), n_rows)
    ar = rows[:, None] * inv[None, :]
    ac = cols[:, None] * inv[None, :]
    cos = jnp.concatenate([jnp.cos(ar), jnp.cos(ar), jnp.cos(ac), jnp.cos(ac)], axis=1)
    sin = jnp.concatenate([-jnp.sin(ar), jnp.sin(ar), -jnp.sin(ac), jnp.sin(ac)], axis=1)
    cos = jnp.concatenate([jnp.ones((ctx, MLA_ROPE), F32), cos], axis=0)
    sin = jnp.concatenate([jnp.zeros((ctx, MLA_ROPE), F32), sin], axis=0)
    return {"ck": jnp.tile(cos, (1, DIFF_MAPS)), "sk": jnp.tile(sin, (1, DIFF_MAPS)), "ct": cos.T, "st": sin.T}


def _partner_perm(width):
    idx = jnp.arange(width)
    r = idx % 16
    return jnp.where(r < 8, idx + 8, idx - 8)


def _layer_weights(l, p):
    d = p["w_in"].shape[1]
    w_in = p["w_in"][l]
    o_diff = 2 * SSD_INNER + 2 * SSD_BC + 2 * SSD_HEADS
    o_mla = o_diff + 3 * DIFF_HEADS * DIFF_V
    nk = DIFF_MAPS * DIFF_QK
    w_ssd = w_in[:, :o_diff]
    wa = jnp.concatenate([w_ssd, jnp.zeros((d, LANES - 2 * SSD_HEADS), F32)], axis=1)
    wq = w_in[:, o_diff:o_diff + nk]
    wk = w_in[:, o_diff + nk:o_diff + 2 * nk]
    wv = w_in[:, o_diff + 2 * nk:o_mla]
    wcq =w_in[:, o_mla:o_mla + MLA_Q_LORA]
    wckv = w_in[:, o_mla + MLA_Q_LORA:o_mla + MLA_Q_LORA + MLA_KV_LORA]
    wkr = w_in[:, o_mla + MLA_Q_LORA + MLA_KV_LORA:]
    zeros = lambda n: jnp.zeros((d, n), F32)
    wm = jnp.concatenate([wcq, zeros(256 - MLA_Q_LORA), wckv, wkr, wkr[:, _partner_perm(MLA_ROPE)],
                          zeros(512 - 448)], axis=1)

    wqu = p["mla_w_q_up"][l].reshape(MLA_Q_LORA, MLA_HEADS, MLA_NOPE + MLA_ROPE)
    pad = jnp.zeros((MLA_Q_LORA, MLA_HEADS, MLA_QK_PAD - MLA_NOPE - MLA_ROPE), F32)
    wqu_plain = jnp.concatenate([wqu, pad], axis=2).reshape(MLA_Q_LORA, -1)
    wkvu = p["mla_w_kv_up"][l].reshape(MLA_KV_LORA, MLA_HEADS, MLA_NOPE + MLA_V)
    wk2 = jnp.concatenate([wkvu[:, :, :MLA_NOPE],
                           jnp.zeros((MLA_KV_LORA, MLA_HEADS, MLA_QK_PAD - MLA_NOPE), F32)],
                          axis=2).reshape(MLA_KV_LORA, -1)
    eye = jnp.eye(MLA_ROPE, dtype=F32)
    ek_h = jnp.concatenate([jnp.zeros((MLA_ROPE, MLA_NOPE), F32), eye,
                            jnp.zeros((MLA_ROPE, MLA_QK_PAD - MLA_NOPE - MLA_ROPE), F32)], axis=1)
    ek = jnp.tile(ek_h, (1, MLA_HEADS))
    wv2 = wkvu[:, :, MLA_NOPE:].reshape(MLA_KV_LORA, -1)

    w_out = p["w_out"][l]
    row = lambda a: a.reshape(1, -1)
    col = lambda a: a.reshape(-1, 1)
    return {
        "norm_mix_pre": row(p["norm_mix_pre"][l]), "norm_mix_post": row(p["norm_mix_post"][l]),
        "norm_ffn_pre": row(p["norm_ffn_pre"][l]), "norm_ffn_post": row(p["norm_ffn_post"][l]),
        "wa": wa.astype(BF16),
        "wdk": wk.astype(BF16),
        "wm": wm.astype(BF16),
        "wqt": wq.T.astype(BF16),
        "wvt": wv.T.astype(BF16),
        "qnw": row(p["mla_q_norm"][l]), "kvnw": row(p["mla_kv_norm"][l]),
        "wqut": wqu_plain.T.astype(BF16),
        "wk2": wk2.astype(BF16), "ek": ek.astype(BF16), "wvt2": wv2.T.astype(BF16),
        "conv_w": p["ssd_conv_w"][l], "conv_b": row(p["ssd_conv_b"][l]),
        "alog": col(p["ssd_a_log"][l]), "dtb": col(p["ssd_dt_bias"][l]),
        "dsk": row(p["ssd_d"][l]), "ssd_norm": row(p["ssd_norm"][l]),
        "lq1": row(p["diff_lam_q1"][l]), "lk1": row(p["diff_lam_k1"][l]),
        "lq2": row(p["diff_lam_q2"][l]), "lk2": row(p["diff_lam_k2"][l]),
        "subw": p["diff_subln"][l].reshape(-1, 1),
        "wo_s": w_out[:SSD_INNER].astype(BF16),
        "wo_d": w_out[SSD_INNER:SSD_INNER + DIFF_HEADS * DIFF_V].astype(BF16),
        "wo_a": w_out[SSD_INNER + DIFF_HEADS * DIFF_V:].astype(BF16),
        "rwt": p["router_w"][l].T, "rb": p["router_b"][l].reshape(-1, 1),
        "w_gate": p["w_gate"], "w_up": p["w_up"], "w_down": p["w_down"],
    }


def kernel(x, c, ctx, c_ctx, ada_w, ada_b, norm_mix_pre, norm_mix_post, norm_ffn_pre, norm_ffn_post, w_in, ssd_conv_w, ssd_conv_b, ssd_a_log, ssd_dt_bias, ssd_d, ssd_norm, diff_lam_q1, diff_lam_k1, diff_lam_q2, diff_lam_k2, diff_subln, mla_q_norm, mla_w_q_up, mla_kv_norm, mla_w_kv_up, w_out, router_w, router_b, w_gate, w_up, w_down):
    p = dict(norm_mix_pre=norm_mix_pre, norm_mix_post=norm_mix_post, norm_ffn_pre=norm_ffn_pre,
             norm_ffn_post=norm_ffn_post, w_in=w_in, ssd_conv_w=ssd_conv_w, ssd_conv_b=ssd_conv_b,
             ssd_a_log=ssd_a_log, ssd_dt_bias=ssd_dt_bias, ssd_d=ssd_d, ssd_norm=ssd_norm,
             diff_lam_q1=diff_lam_q1, diff_lam_k1=diff_lam_k1, diff_lam_q2=diff_lam_q2, diff_lam_k2=diff_lam_k2,
             diff_subln=diff_subln, mla_q_norm=mla_q_norm, mla_w_q_up=mla_w_q_up, mla_kv_norm=mla_kv_norm,
             mla_w_kv_up=mla_w_kv_up, w_out=w_out, router_w=router_w, router_b=router_b,
             w_gate=w_gate, w_up=w_up, w_down=w_down)
    b, seq, d = x.shape
    nctx = ctx.shape[1]
    depth = ada_w.shape[0]
    assert nctx == TILE and seq % TILE == 0 and seq % GRID_W == 0
    nt = nctx + seq
    caps = (EC_CAPACITY * nctx // N_EXPERTS, EC_CAPACITY * seq // N_EXPERTS)
    assert caps[0] % 16 == 0 and caps[1] % 16 == 0 and caps[0] + caps[1] >= WIN

    cvec = jnp.concatenate([c, c_ctx[None, :], jnp.zeros((8 - b - 1, d), F32)], axis=0)
    mods = _adaln(cvec, ada_w, ada_b).reshape(depth, 8, N_MOD, d)
    tabs = _rope_tables(seq, nctx)
    t = (x, ctx)
    for l in range(depth):
        lw = _layer_weights(l, p)
        lambda_init = 0.8 - 0.6 * math.exp(-0.3 * l)
        mod = jnp.stack([jnp.broadcast_to(mods[l, b], (b, N_MOD, d)), mods[l, :b]], axis=1)
        z, xbc, dt, dq, dk, dv, mq, mk, mv = _inproj(t, mod, lw, tabs)
        s = _ssd(z, xbc, dt, lw)
        da = _diff_attn(dq, dk, dv, lw, lambda_init).reshape(b, DIFF_HEADS * DIFF_V, nt)
        aa = _mla_attn(mq, mk, mv).reshape(b, MLA_HEADS * MLA_V, nt)
        t, hf, aff = _outproj(t, s, da, aa, mod, lw)
        pos, gate, cum = _route(aff, caps)
        cum_flat = cum.reshape(-1)
        xg, gc = _gather(cum_flat, hf, pos, gate, caps[0] + caps[1])
        y = _experts(xg, gc, lw, l)
        t = _combine(cum_flat, t, y, pos, mod, lw, latent_only=(l == depth - 1))
    return t
```

```python
import functools
import math

import jax
import jax.numpy as jnp
from jax import lax
from jax.experimental import pallas as pl
from jax.experimental.pallas import tpu as pltpu

F32 = jnp.float32
BF16 = jnp.bfloat16
I32 = jnp.int32
HIGHEST = lax.Precision.HIGHEST

EPS = 1e-6
GRID_W = 64
ROPE_BASE = 10000.0
N_MOD = 6

SSD_HEADS = 8
SSD_HEAD_DIM = 64
SSD_INNER = SSD_HEADS * SSD_HEAD_DIM
SSD_GROUPS = 2
SSD_STATE = 64
SSD_CHUNK = 128
SSD_BC = SSD_GROUPS * SSD_STATE
SSD_CONV_DIM = SSD_INNER + 2 * SSD_BC
SSD_PAIRS = SSD_HEADS // 2

DIFF_HEADS = 4
DIFF_QK = 32
DIFF_V = 64
DIFF_MAPS = 2 * DIFF_HEADS

MLA_HEADS = 4
MLA_Q_LORA = 192
MLA_KV_LORA = 128
MLA_NOPE = 64
MLA_ROPE = 32
MLA_V = 64
MLA_QK_PAD = 128

N_EXPERTS = 16
EC_CAPACITY = 2

TILE = 256
LANES = 128
MAPS_PER_TILE = LANES // DIFF_QK
VMEM_LIMIT = 56 * 1024 * 1024

NT_DIMS = (((1,), (1,)), ((), ()))
TN_DIMS = (((0,), (0,)), ((), ()))


def _cparams(sem, vmem=None):
    return pltpu.CompilerParams(dimension_semantics=sem, vmem_limit_bytes=vmem)


def _rms(x, w):
    return x * lax.rsqrt(jnp.mean(x * x, axis=-1, keepdims=True) + EPS) * w


def _silu(x):
    return x * jax.nn.sigmoid(x)


def _dot(a, b):
    return jnp.dot(a, b, preferred_element_type=F32)


def _adaln_kernel(c_ref, w_ref, b_ref, o_ref):
    s = _silu(c_ref[...])
    w = w_ref[0]
    rows = s.shape[0]
    s_hi = s.astype(BF16).astype(F32)
    w_hi = w.astype(BF16)
    w_lo = (w - w_hi.astype(F32)).astype(BF16)
    both = _dot(jnp.concatenate([s_hi, s - s_hi], axis=0).astype(BF16), w_hi)
    o_ref[0] = both[:rows] + both[rows:] + _dot(s_hi.astype(BF16), w_lo) + b_ref[0]


def _adaln(cvec, ada_w, ada_b):
    depth, d, nd = ada_w.shape
    rows = cvec.shape[0]
    return pl.pallas_call(
        _adaln_kernel,
        grid=(depth, nd // d),
        in_specs=[pl.BlockSpec((rows, d), lambda l, j: (0, 0)),
                  pl.BlockSpec((1, d, d), lambda l, j: (l, 0, j)),
                  pl.BlockSpec((1, 1, d), lambda l, j: (l, 0, j))],
        out_specs=pl.BlockSpec((1, rows, d), lambda l, j: (l, 0, j)),
        out_shape=jax.ShapeDtypeStruct((depth, rows, nd), F32),
        compiler_params=_cparams(("arbitrary", "arbitrary")),
        name="adaln",
    )(cvec, ada_w, ada_b.reshape(depth, 1, nd))


def _token_stream(t, nb=1):
    if isinstance(t, tuple):
        x, ctx = t
        b, seq, d = x.shape
        specs = [pl.BlockSpec((nb, TILE, d), lambda bi, i: (bi, jnp.maximum(i - 1, 0), 0)),
                 pl.BlockSpec((nb, TILE, d), lambda bi, i: (bi, 0, 0))]
        return (x, ctx), specs, (b, seq + ctx.shape[1], d)
    b, nt, d = t.shape
    specs = [pl.BlockSpec((nb, TILE, d), lambda bi, i: (bi, i, 0)),
             pl.BlockSpec((nb, TILE, d), lambda bi, i: (bi, 0, 0))]
    return (t, t), specs, (b, nt, d)


def _stream_tile(x_ref, c_ref):
    return jnp.where(pl.program_id(1) == 0, c_ref[0], x_ref[0])


def _partner_rows(x):
    parts = []
    for g in range(0, x.shape[0], 16):
        parts += [x[g + 8:g + 16], x[g:g + 8]]
    return jnp.concatenate(parts, axis=0)


def _partner_lanes(x):
    width = x.shape[1]
    lane = lax.broadcasted_iota(I32, x.shape, 1)
    return jnp.where((lane & 8) == 0, pltpu.roll(x, width - 8, 1), pltpu.roll(x, 8, 1))


def _inproj_kernel(x_ref, c_ref, mod_ref, nw_ref, wa_ref, wdk_ref, wm_ref, wqt_ref, wvt_ref,
                   ck_ref, sk_ref, ct_ref, st_ref, qnw_ref, kvnw_ref, wqut_ref, wk2_ref, ek_ref, wvt2_ref,
                   z_ref, xbc_ref, dt_ref, dq_ref, dk_ref, dv_ref, mq_ref, mk_ref, mv_ref):
    x = _stream_tile(x_ref, c_ref)
    mod = mod_ref[0, 0]
    h = (_rms(x, nw_ref[...]) * (1.0 + mod[1:2]) + mod[0:1]).astype(BF16)

    ra = _dot(h, wa_ref[...])
    z_ref[0] = ra[:, :SSD_INNER].astype(BF16)
    xbc_ref[0] = ra[:, SSD_INNER:SSD_INNER + SSD_CONV_DIM]
    dt_ref[0] = ra[:, SSD_INNER + SSD_CONV_DIM:]

    rk = _dot(h, wdk_ref[...])
    k = (rk * ck_ref[...] + _partner_lanes(rk) * sk_ref[...]).astype(BF16)
    for g in range(DIFF_MAPS // MAPS_PER_TILE):
        dk_ref[0, g, 0] = k[:, LANES * g:LANES * (g + 1)]

    ct = ct_ref[...]
    st = st_ref[...]
    rq = lax.dot_general(wqt_ref[...], h, NT_DIMS, preferred_element_type=F32)
    rq_partner = _partner_rows(rq)
    for m in range(DIFF_MAPS):
        lo = DIFF_QK * m
        qm = ((rq[lo:lo + DIFF_QK] * ct + rq_partner[lo:lo + DIFF_QK] * st) * DIFF_C_EXP).astype(BF16)
        above = DIFF_QK * (m % MAPS_PER_TILE)
        below = LANES - above - DIFF_QK
        parts = ([jnp.zeros((above, qm.shape[1]), BF16)] if above else []) + [qm]
        parts += [jnp.zeros((below, qm.shape[1]), BF16)] if below else []
        dq_ref[0, m] = jnp.concatenate(parts, axis=0)

    rv = lax.dot_general(wvt_ref[...], h, NT_DIMS, preferred_element_type=F32).astype(BF16)
    for hd in range(DIFF_HEADS):
        dv_ref[0, hd, 0] = rv[DIFF_V * hd:DIFF_V * (hd + 1)]

    rm = _dot(h, wm_ref[...])
    cq = _rms(rm[:, :MLA_Q_LORA], qnw_ref[...]).astype(BF16)
    ckv = _rms(rm[:, 256:256 + MLA_KV_LORA], kvnw_ref[...]).astype(BF16)
    kr = rm[:, 384:384 + MLA_ROPE] * ck_ref[:, :MLA_ROPE] + rm[:, 416:416 + MLA_ROPE] * sk_ref[:, :MLA_ROPE]

    rq2 = lax.dot_general(wqut_ref[...], cq, NT_DIMS, preferred_element_type=F32)
    rq2_partner = _partner_rows(rq2)
    ones = jnp.ones((MLA_NOPE, ct.shape[1]), F32)
    pad1 = jnp.ones((MLA_QK_PAD - MLA_NOPE - MLA_ROPE, ct.shape[1]), F32)
    ct_h = jnp.concatenate([ones, ct, pad1], axis=0)
    st_h = jnp.concatenate([0.0 * ones, st, 0.0 * pad1], axis=0)
    for hd in range(MLA_HEADS):
        lo = MLA_QK_PAD * hd
        qh = rq2[lo:lo + MLA_QK_PAD] * ct_h + rq2_partner[lo:lo + MLA_QK_PAD] * st_h
        mq_ref[0, hd] = (qh * MLA_C_EXP).astype(BF16)

    k2 = (_dot(ckv, wk2_ref[...]) + _dot(kr.astype(BF16), ek_ref[...])).astype(BF16)
    for hd in range(MLA_HEADS):
        mk_ref[0, hd, 0] = k2[:, MLA_QK_PAD * hd:MLA_QK_PAD * (hd + 1)]
    rv2 = lax.dot_general(wvt2_ref[...], ckv, NT_DIMS, preferred_element_type=F32).astype(BF16)
    for hd in range(MLA_HEADS):
        mv_ref[0, hd, 0] = rv2[MLA_V * hd:MLA_V * (hd + 1)]


def _inproj(t, mod, lw, tabs):
    streams, stream_specs, (b, nt, d) = _token_stream(t)
    nti = nt // TILE
    full = lambda a: pl.BlockSpec(a.shape, lambda bi, i: (0,) * a.ndim)
    tok = lambda w: pl.BlockSpec((TILE, w), lambda bi, i: (i, 0))
    tokt = lambda w: pl.BlockSpec((w, TILE), lambda bi, i: (0, i))
    ws = [lw["norm_mix_pre"], lw["wa"], lw["wdk"], lw["wm"], lw["wqt"], lw["wvt"]]
    ws2 = [lw["qnw"], lw["kvnw"], lw["wqut"], lw["wk2"], lw["ek"], lw["wvt2"]]
    out_shape = [
        jax.ShapeDtypeStruct((b, nt, SSD_INNER), BF16),
        jax.ShapeDtypeStruct((b, nt, SSD_CONV_DIM), F32),
        jax.ShapeDtypeStruct((b, nt, LANES), F32),
        jax.ShapeDtypeStruct((b, DIFF_MAPS, LANES, nt), BF16),
        jax.ShapeDtypeStruct((b, DIFF_MAPS // MAPS_PER_TILE, nti, TILE, LANES), BF16),
        jax.ShapeDtypeStruct((b, DIFF_HEADS, nti, DIFF_V, TILE), BF16),
        jax.ShapeDtypeStruct((b, MLA_HEADS, MLA_QK_PAD, nt), BF16),
        jax.ShapeDtypeStruct((b, MLA_HEADS, nti, TILE, MLA_QK_PAD), BF16),
        jax.ShapeDtypeStruct((b, MLA_HEADS, nti, MLA_V, TILE), BF16),
    ]
    out_specs = [
        pl.BlockSpec((1, TILE, SSD_INNER), lambda bi, i: (bi, i, 0)),
        pl.BlockSpec((1, TILE, SSD_CONV_DIM), lambda bi, i: (bi, i, 0)),
        pl.BlockSpec((1, TILE, LANES), lambda bi, i: (bi, i, 0)),
        pl.BlockSpec((1, DIFF_MAPS, LANES, TILE), lambda bi, i: (bi, 0, 0, i)),
        pl.BlockSpec((1, DIFF_MAPS // MAPS_PER_TILE, 1, TILE, LANES), lambda bi, i: (bi, 0, i, 0, 0)),
        pl.BlockSpec((1, DIFF_HEADS, 1, DIFF_V, TILE), lambda bi, i: (bi, 0, i, 0, 0)),
        pl.BlockSpec((1, MLA_HEADS, MLA_QK_PAD, TILE), lambda bi, i: (bi, 0, 0, i)),
        pl.BlockSpec((1, MLA_HEADS, 1, TILE, MLA_QK_PAD), lambda bi, i: (bi, 0, i, 0, 0)),
        pl.BlockSpec((1, MLA_HEADS, 1, MLA_V, TILE), lambda bi, i: (bi, 0, i, 0, 0)),
    ]
    in_specs = (stream_specs
                + [pl.BlockSpec((1, 1, N_MOD, d), lambda bi, i: (bi, jnp.minimum(i, 1), 0, 0))]
                + [full(a) for a in ws]
                + [tok(DIFF_MAPS * DIFF_QK), tok(DIFF_MAPS * DIFF_QK), tokt(DIFF_QK), tokt(DIFF_QK)]
                + [full(a) for a in ws2])
    return pl.pallas_call(
        _inproj_kernel,
        grid=(b, nti),
        in_specs=in_specs,
        out_specs=out_specs,
        out_shape=out_shape,
        compiler_params=_cparams(("arbitrary", "arbitrary"), VMEM_LIMIT),
        name="inproj",
    )(*streams, mod, *ws, tabs["ck"], tabs["sk"], tabs["ct"], tabs["st"], *ws2)


def _ssd_chunk_of(ph, i, nck, nctx):
    back = jnp.where(i < nctx, nctx - 1 - i, nck - 1 + nctx - i)
    return jnp.where(ph == 0, back, i)


def _ssd_kernel(xc_ref, xp_ref, xn_ref, dt_ref, z_ref, cw_ref, cb_ref, alog_ref, dtb_ref, dsk_ref, nw_ref,
                o_ref, s_ref, sb_ref, *cache, nck, nctx, nb):
    ph = pl.program_id(1)
    i = pl.program_id(2)
    c = _ssd_chunk_of(ph, i, nck, nctx)

    @pl.when(i == 0)
    def _():
        s_ref[...] = jnp.zeros_like(s_ref)

    one = lambda ref, bb: ref.at[pl.ds(bb, 1)]
    fns = [_ssd_sample(one(xc_ref, bb), one(xp_ref, bb), one(xn_ref, bb), one(dt_ref, bb), one(z_ref, bb),
                       cw_ref, cb_ref, alog_ref, dtb_ref, dsk_ref, nw_ref, one(o_ref, bb),
                       s_ref.at[bb], sb_ref.at[bb], [r.at[bb] for r in cache], c, nck=nck, nctx=nctx)
           for bb in range(nb)]

    @pl.when(ph == 0)
    def _():
        for backward_states, _ in fns:
            backward_states()

    @pl.when(ph == 1)
    def _():
        for _, forward_and_output in fns:
            forward_and_output()


def _ssd_sample(xc_ref, xp_ref, xn_ref, dt_ref, z_ref, cw_ref, cb_ref, alog_ref, dtb_ref, dsk_ref, nw_ref,
                o_ref, s_ref, sb_ref, cache, c, *, nck, nctx):
    q = SSD_CHUNK
    nh2 = 2 * SSD_HEADS
    cx_ref, cbc_ref, crow_ref, ccol_ref, cbt_ref = cache
    ri = lax.broadcasted_iota(I32, (q, q), 0)
    ci = lax.broadcasted_iota(I32, (q, q), 1)
    lower = ci <= ri
    upper = ci >= ri
    lane = ci
    first_half_s = ri < SSD_STATE
    first_half_l = lane < SSD_HEAD_DIM
    blockdiag = first_half_s == first_half_l

    def chunk_values():
        x = xc_ref[0]
        has_prev = jnp.logical_and(c != 0, c != nctx)
        has_next = jnp.logical_and(c != nctx - 1, c != nck - 1)
        prev_row = jnp.where(has_prev, xp_ref[0][7:8, :], 0.0)
        next_row = jnp.where(has_next, xn_ref[0][0:1, :], 0.0)
        row = lax.broadcasted_iota(I32, x.shape, 0)
        xm1 = jnp.where(row == 0, prev_row, pltpu.roll(x, 1, 0))
        xp1 = jnp.where(row == q - 1, next_row, pltpu.roll(x, q - 1, 0))
        cw = cw_ref[...]
        u = _silu(xm1 * cw[0:1] + x * cw[1:2] + xp1 * cw[2:3] + cb_ref[...])
        xs = u[:, :SSD_INNER].astype(BF16)
        bc = u[:, SSD_INNER:]
        xdt = dt_ref[0].T[:nh2] + dtb_ref[...]
        dtt = jnp.maximum(xdt, 0.0) + jnp.log1p(jnp.exp(-jnp.abs(xdt)))
        dat = dtt * (-jnp.exp(alog_ref[...]) * math.log2(math.e))
        tri_dims = (((1,), (0,)), ((), ()))
        acf = lax.dot_general(dat, upper.astype(F32), tri_dims, precision=HIGHEST, preferred_element_type=F32)
        acb = lax.dot_general(dat, lower.astype(F32), tri_dims, precision=HIGHEST, preferred_element_type=F32)
        act = jnp.where(ri[:nh2] < SSD_HEADS, acf, acb)
        rows = jnp.concatenate([dtt, act], axis=0)
        cols = jnp.concatenate([rows, jnp.zeros((q - 2 * nh2, q), F32)], axis=0).T
        bt = bc[:, :SSD_BC].T
        return xs, bc, rows, cols, bt

    def bodies(xs, bc, rows, cols, bt):
        bm = bc[:, :SSD_BC]
        cm = bc[:, SSD_BC:]
        dtt = rows[:nh2]
        act = rows[nh2:]
        acc = pltpu.roll(cols, LANES - nh2, 1)
        fwd_rows = lax.broadcasted_iota(I32, (nh2, 1), 0) < SSD_HEADS
        alast = jnp.where(fwd_rows, act[:, q - 1:q], act[:, 0:1])
        w_rows = jnp.exp2(alast - act) * dtt
        decay_all = jnp.exp2(alast)

        def pair_cols(arr_c, h0):
            return jnp.where(first_half_l, arr_c[:, h0:h0 + 1], arr_c[:, h0 + 1:h0 + 2])

        def pair_rows(arr_t, h0):
            return jnp.where(first_half_s[:, 0:arr_t.shape[1]], arr_t[h0:h0 + 1, :], arr_t[h0 + 1:h0 + 2, :])

        def state_update(p, d):
            g = (2 * p) // (SSD_HEADS // SSD_GROUPS)
            h0 = d * SSD_HEADS + 2 * p
            btg = bt[SSD_STATE * g:SSD_STATE * (g + 1)]
            lhs = (jnp.concatenate([btg, btg], axis=0) * pair_rows(w_rows, h0)).astype(BF16)
            xs2 = xs[:, 2 * SSD_HEAD_DIM * p:2 * SSD_HEAD_DIM * (p + 1)]
            upd = jnp.where(blockdiag, _dot(lhs, xs2), 0.0)
            return pair_rows(decay_all, h0) * s_ref[d, p] + upd

        def backward():
            for p in range(SSD_PAIRS):
                sb_ref[c, p] = s_ref[1, p].astype(BF16)
                s_ref[1, p] = state_update(p, 1)

        def forward():
            roll_c = pltpu.roll(cm, SSD_STATE, 1)
            dsk = dsk_ref[...]
            ys = []
            for p in range(SSD_PAIRS):
                g = (2 * p) // (SSD_HEADS // SSD_GROUPS)
                cg_only = jnp.where((lane < SSD_STATE) == (g == 0), cm, 0.0).astype(BF16)
                cb = lax.dot_general(cg_only, bm.astype(BF16), NT_DIMS, preferred_element_type=F32)
                ms = []
                for hh in range(2):
                    hf = 2 * p + hh
                    hb = SSD_HEADS + hf
                    lf = (jnp.exp2(jnp.where(lower, acc[:, hf:hf + 1] - act[hf:hf + 1, :], -jnp.inf))
                          * dtt[hf:hf + 1, :])
                    lb = (jnp.exp2(jnp.where(upper, acc[:, hb:hb + 1] - act[hb:hb + 1, :], -jnp.inf))
                          * dtt[hb:hb + 1, :])
                    ms.append((cb * (lf + lb) + jnp.where(ri == ci, dsk[:, hf:hf + 1], 0.0)).astype(BF16))
                xs2 = xs[:, 2 * SSD_HEAD_DIM * p:2 * SSD_HEAD_DIM * (p + 1)]
                zero = jnp.zeros_like(xs2)
                rhs = jnp.concatenate([jnp.where(first_half_l, xs2, zero), jnp.where(first_half_l, zero, xs2)],
                                      axis=0)
                y = _dot(jnp.concatenate(ms, axis=1), rhs)
                cdup = jnp.where(first_half_l == (g == 0), cm, roll_c)
                ef = jnp.exp2(pair_cols(acc, 2 * p))
                eb = jnp.exp2(pair_cols(acc, SSD_HEADS + 2 * p))
                lhs_off = jnp.concatenate([cdup * ef, cdup * eb], axis=1).astype(BF16)
                rhs_off = jnp.concatenate([s_ref[0, p].astype(BF16), sb_ref[c, p]], axis=0)
                ys.append(y + _dot(lhs_off, rhs_off))
                s_ref[0, p] = state_update(p, 0)
            y = jnp.concatenate(ys, axis=1)
            zf = z_ref[0].astype(F32)
            gt = y * _silu(zf)
            nw = nw_ref[...]
            gw = SSD_INNER // SSD_GROUPS
            outs = [_rms(gt[:, gw * g:gw * (g + 1)], nw[:, gw * g:gw * (g + 1)]) for g in range(SSD_GROUPS)]
            o_ref[0] = jnp.concatenate(outs, axis=1).astype(BF16)

        return backward, forward

    def backward_states():
        xs, bc, rows, cols, bt = chunk_values()
        cx_ref[c] = xs
        cbc_ref[c] = bc
        crow_ref[c] = rows
        ccol_ref[c] = cols
        cbt_ref[c] = bt
        bodies(xs, bc, rows, cols, bt)[0]()

    def forward_and_output():
        bodies(cx_ref[c], cbc_ref[c], crow_ref[c], ccol_ref[c], cbt_ref[c])[1]()

    return backward_states, forward_and_output


def _ssd(z, xbc, dt, lw):
    b, nt, _ = z.shape
    q = SSD_CHUNK
    nck = nt // q
    nctx = TILE // q
    rows8 = q // 8
    nb = 1
    chunk = functools.partial(_ssd_chunk_of, nck=nck, nctx=nctx)
    full = lambda a: pl.BlockSpec(a.shape, lambda bi, ph, i: (0,) * a.ndim)
    ws = [lw["conv_w"], lw["conv_b"], lw["alog"], lw["dtb"], lw["dsk"], lw["ssd_norm"]]
    kern = functools.partial(_ssd_kernel, nck=nck, nctx=nctx, nb=nb)
    return pl.pallas_call(
        kern,
        grid=(b // nb, 2, nck),
        in_specs=[pl.BlockSpec((nb, q, SSD_CONV_DIM), lambda bi, ph, i: (bi, chunk(ph, i), 0)),
                  pl.BlockSpec((nb, 8, SSD_CONV_DIM),
                               lambda bi, ph, i: (bi, jnp.maximum(chunk(ph, i) * rows8 - 1, 0), 0)),
                  pl.BlockSpec((nb, 8, SSD_CONV_DIM),
                               lambda bi, ph, i: (bi, jnp.minimum((chunk(ph, i) + 1) * rows8, nck * rows8 - 1), 0)),
                  pl.BlockSpec((nb, q, LANES), lambda bi, ph, i: (bi, chunk(ph, i), 0)),
                  pl.BlockSpec((nb, q, SSD_INNER), lambda bi, ph, i: (bi, chunk(ph, i), 0))]
                 + [full(a) for a in ws],
        out_specs=pl.BlockSpec((nb, q, SSD_INNER), lambda bi, ph, i: (bi, jnp.where(ph == 0, 0, i), 0)),
        out_shape=jax.ShapeDtypeStruct((b, nt, SSD_INNER), BF16),
        scratch_shapes=[pltpu.VMEM((nb, 2, SSD_PAIRS, 2 * SSD_STATE, 2 * SSD_HEAD_DIM), F32),
                        pltpu.VMEM((nb, nck, SSD_PAIRS, 2 * SSD_STATE, 2 * SSD_HEAD_DIM), BF16),
                        pltpu.VMEM((nb, nck, q, SSD_INNER), BF16),
                        pltpu.VMEM((nb, nck, q, 2 * SSD_BC), F32),
                        pltpu.VMEM((nb, nck, 4 * SSD_HEADS, q), F32),
                        pltpu.VMEM((nb, nck, q, LANES), F32),
                        pltpu.VMEM((nb, nck, SSD_BC, q), F32)],
        compiler_params=_cparams(("arbitrary", "arbitrary", "arbitrary"), VMEM_LIMIT),
        name="ssd",
    )(xbc, xbc, xbc, dt, z, *ws)


ACC_ROWS = 80
DIFF_C_EXP = (DIFF_QK ** -0.5) * math.log2(math.e)
MLA_C_EXP = ((MLA_NOPE + MLA_ROPE) ** -0.5) * math.log2(math.e)


def _ones_rows(tk):
    return (lax.broadcasted_iota(I32, (ACC_ROWS - DIFF_V, tk), 0) == 0).astype(BF16)


def _score_step(kqs, s_ref):
    for idx, (k, q) in enumerate(kqs):
        s_ref[idx, 0:k.shape[0], :] = _dot(k, q)


def _softmax_pv_step(n_keys, vaugs, s_ref, m_ref, acc_ref):
    for idx in range(len(vaugs)):
        s = s_ref[idx, 0:n_keys, :]
        m = m_ref[idx]
        mn = jnp.maximum(m, jnp.max(s, axis=0, keepdims=True))
        p = jnp.exp2(s - mn).astype(BF16)
        acc_ref[idx] = acc_ref[idx] * jnp.exp2(m - mn) + _dot(vaugs[idx], p)
        m_ref[idx] = mn


def _plain_pv_step(n_keys, vaugs, s_ref, acc_ref):
    for idx in range(len(vaugs)):
        p = jnp.exp2(s_ref[idx, 0:n_keys, :]).astype(BF16)
        acc_ref[idx] += _dot(vaugs[idx], p)


def _plain_keys(kq_fn, v_fn, acc_ref, nti, group):
    def run(steps):
        work = [(c0, n, idx) for c0, n in steps for idx in range(acc_ref.shape[0])]
        kqs = {}
        vaugs = {}

        def score(item):
            c0, n, idx = item
            if (c0, n) not in kqs:
                kqs[(c0, n)] = kq_fn(c0, n)
            k, q = kqs[(c0, n)][idx]
            return _dot(k, q)

        nxt = score(work[0])
        for pos, (c0, n, idx) in enumerate(work):
            s = nxt
            if pos + 1 < len(work):
                nxt = score(work[pos + 1])
            if (c0, n) not in vaugs:
                vaugs[(c0, n)] = v_fn(c0, n)
            acc_ref[idx] += _dot(vaugs[(c0, n)][idx], jnp.exp2(s).astype(BF16))

    latent = pl.program_id(1) > 0
    pl.when(jnp.logical_not(latent))(lambda: run([(0, 1)]))
    pl.when(latent)(lambda: run([(0, 1)] + [(1 + g * group, group) for g in range((nti - 1) // group)]))


def _attn_init(m_ref, acc_ref):
    m_ref[...] = jnp.full(m_ref.shape, -jnp.inf, F32)
    acc_ref[...] = jnp.zeros_like(acc_ref)


MAX_UNSHIFTED_SCORE = 96.0


def _key_abs_max(k_ref, kmax_ref):
    @pl.when(pl.program_id(1) == 0)
    def _():
        lead = k_ref.shape[1]
        nti = k_ref.shape[2]

        def body(t, best):
            for g in range(lead):
                best = jnp.maximum(best, jnp.max(jnp.abs(k_ref[0, g, t].astype(F32))))
            return best

        kmax_ref[0] = lax.fori_loop(0, nti, body, jnp.float32(0.0))


def _scores_are_bounded(q_ref, kmax_ref):
    q = jnp.abs(q_ref[0].astype(F32))
    return kmax_ref[0] * jnp.max(jnp.sum(q, axis=1)) <= MAX_UNSHIFTED_SCORE


DIFF_KEY_GROUP = 2
MLA_KEY_GROUP = 4


def _key_group(nti, want):
    n_lat = nti - 1
    assert n_lat % 2 == 0
    while n_lat % (2 * want):
        want //= 2
    return want


def _chunks_k(k_ref, lead, c0, n):
    return jnp.concatenate([k_ref[lead + (c0 + j,)] for j in range(n)], axis=0)


def _chunks_v(v_ref, lead, c0, n):
    v = jnp.concatenate([v_ref[lead + (c0 + j,)] for j in range(n)], axis=1)
    return jnp.concatenate([v, _ones_rows(n * TILE)], axis=0)


def _pipelined_keys(scores, consume, nti, group):
    steps = (nti - 1) // group
    first = lambda k: 1 + (k - 1) * group
    scores(0, 1, 0)
    latent = pl.program_id(1) > 0

    @pl.when(jnp.logical_not(latent))
    def _():
        consume(0, 1, 0)

    @pl.when(latent)
    def _():
        scores(first(1), group, 1)
        consume(0, 1, 0)
        scores(first(2), group, 0)
        consume(first(1), group, 1)

        def body(j, carry):
            k = 2 * j
            scores(first(k + 1), group, 1)
            consume(first(k), group, 0)
            scores(first(k + 2), group, 0)
            consume(first(k + 1), group, 1)
            return carry

        lax.fori_loop(1, steps // 2, body, 0)
        consume(first(steps), group, 0)


def _diff_attn_kernel(lq1_ref, lk1_ref, lq2_ref, lk2_ref, subw_ref, q_ref, k_ref, v_ref, o_ref,
                      m_ref, acc_ref, sa_ref, sb_ref, kmax_ref, *, nti, group, lambda_init):
    _attn_init(m_ref, acc_ref)
    _key_abs_max(k_ref, kmax_ref)
    slots = (sa_ref, sb_ref)

    def kq_pairs(c0, n):
        ks = [_chunks_k(k_ref, (0, g), c0, n) for g in range(DIFF_MAPS // MAPS_PER_TILE)]
        return [(ks[m // MAPS_PER_TILE], q_ref[0, m]) for m in range(DIFF_MAPS)]

    def values(c0, n):
        vaugs = []
        for h in range(DIFF_HEADS):
            vaugs += [_chunks_v(v_ref, (0, h), c0, n)] * 2
        return vaugs

    def scores(c0, n, slot):
        _score_step(kq_pairs(c0, n), slots[slot])

    def consume(c0, n, slot):
        _softmax_pv_step(n * TILE, values(c0, n), slots[slot], m_ref, acc_ref)

    bounded = _scores_are_bounded(q_ref, kmax_ref)
    pl.when(bounded)(lambda: _plain_keys(kq_pairs, values, acc_ref, nti, 4))
    pl.when(jnp.logical_not(bounded))(lambda: _pipelined_keys(scores, consume, nti, group))
    lam =(jnp.exp(jnp.sum(lq1_ref[...] * lk1_ref[...], keepdims=True))
           - jnp.exp(jnp.sum(lq2_ref[...] * lk2_ref[...], keepdims=True)) + lambda_init)
    for h in range(DIFF_HEADS):
        a1 = acc_ref[2 * h]
        a2 = acc_ref[2 * h + 1]
        o = a1[:DIFF_V] / a1[DIFF_V:DIFF_V + 1] - lam * (a2[:DIFF_V] / a2[DIFF_V:DIFF_V + 1])
        o = o * lax.rsqrt(jnp.mean(o * o, axis=0, keepdims=True) + EPS) * subw_ref[...]
        o_ref[0, h] = (o * (1.0 - lambda_init)).astype(BF16)


def _diff_attn(dq, dk, dv, lw, lambda_init):
    b, nmaps, _, nt = dq.shape
    nh = nmaps // 2
    nti = nt // TILE
    group = _key_group(nti, DIFF_KEY_GROUP)
    kern = functools.partial(_diff_attn_kernel, nti=nti, group=group, lambda_init=lambda_init)
    score_slot = pltpu.VMEM((2 * nh, group * TILE, TILE), F32)
    vec = pl.BlockSpec((1, DIFF_QK), lambda bi, i: (0, 0))
    return pl.pallas_call(
        kern,
        grid=(b, nti),
        in_specs=[vec, vec, vec, vec,
                  pl.BlockSpec((DIFF_V, 1), lambda bi, i: (0, 0)),
                  pl.BlockSpec((1, nmaps, LANES, TILE), lambda bi, i: (bi, 0, 0, i)),
                  pl.BlockSpec((1, nmaps // MAPS_PER_TILE, nti, TILE, LANES), lambda bi, i: (bi, 0, 0, 0, 0)),
                  pl.BlockSpec((1, nh, nti, DIFF_V, TILE), lambda bi, i: (bi, 0, 0, 0, 0))],
        out_specs=pl.BlockSpec((1, nh, DIFF_V, TILE), lambda bi, i: (bi, 0, 0, i)),
        out_shape=jax.ShapeDtypeStruct((b, nh, DIFF_V, nt), BF16),
        scratch_shapes=[pltpu.VMEM((2 * nh, 1, TILE), F32), pltpu.VMEM((2 * nh, ACC_ROWS, TILE), F32),
                        score_slot, score_slot, pltpu.SMEM((1,), F32)],
        compiler_params=_cparams(("arbitrary", "arbitrary"), VMEM_LIMIT),
        name="diff_attn",
    )(lw["lq1"], lw["lk1"], lw["lq2"], lw["lk2"], lw["subw"], dq, dk, dv)


def _mla_attn_kernel(q_ref, k_ref, v_ref, o_ref, m_ref, acc_ref, sa_ref, sb_ref, kmax_ref, *, nti, group):
    _attn_init(m_ref, acc_ref)
    _key_abs_max(k_ref, kmax_ref)
    slots = (sa_ref, sb_ref)

    def kq_pairs(c0, n):
        return [(_chunks_k(k_ref, (0, h), c0, n), q_ref[0, h]) for h in range(MLA_HEADS)]

    def values(c0, n):
        return [_chunks_v(v_ref, (0, h), c0, n) for h in range(MLA_HEADS)]

    def scores(c0, n, slot):
        _score_step(kq_pairs(c0, n), slots[slot])

    def consume(c0, n, slot):
        _softmax_pv_step(n * TILE, values(c0, n), slots[slot], m_ref, acc_ref)

    def consume_plain(c0, n, slot):
        _plain_pv_step(n * TILE, values(c0, n), slots[slot], acc_ref)

    bounded = _scores_are_bounded(q_ref, kmax_ref)
    pl.when(bounded)(lambda: _pipelined_keys(scores, consume_plain, nti, group))
    pl.when(jnp.logical_not(bounded))(lambda: _pipelined_keys(scores, consume, nti, group))
    for h in range(MLA_HEADS):
        a = acc_ref[h]
        o_ref[0, h] = (a[:MLA_V] / a[MLA_V:MLA_V + 1]).astype(BF16)


def _mla_attn(mq, mk, mv):
    b, nh, dpad, nt = mq.shape
    nti = nt // TILE
    group = _key_group(nti, MLA_KEY_GROUP)
    kern = functools.partial(_mla_attn_kernel, nti=nti, group=group)
    score_slot = pltpu.VMEM((nh, group * TILE, TILE), F32)
    return pl.pallas_call(
        kern,
        grid=(b, nti),
        in_specs=[pl.BlockSpec((1, nh, dpad, TILE), lambda bi, i: (bi, 0, 0, i)),
                  pl.BlockSpec((1, nh, nti, TILE, dpad), lambda bi, i: (bi, 0, 0, 0, 0)),
                  pl.BlockSpec((1, nh, nti, MLA_V, TILE), lambda bi, i: (bi, 0, 0, 0, 0))],
        out_specs=pl.BlockSpec((1, nh, MLA_V, TILE), lambda bi, i: (bi, 0, 0, i)),
        out_shape=jax.ShapeDtypeStruct((b, nh, MLA_V, nt), BF16),
        scratch_shapes=[pltpu.VMEM((nh, 1, TILE), F32), pltpu.VMEM((nh, ACC_ROWS, TILE), F32),
                        score_slot, score_slot, pltpu.SMEM((1,), F32)],
        compiler_params=_cparams(("arbitrary", "arbitrary"), VMEM_LIMIT),
        name="mla_attn",
    )(mq, mk, mv)


def _outproj_kernel(t_ref, c_ref, s_ref, d_ref, a_ref, mod_ref, npost_ref, nffn_ref, ws_ref, wd_ref, wa_ref,
                    rwt_ref, rb_ref, tn_ref, hf_ref, aff_ref):
    nb = s_ref.shape[0]
    first = pl.program_id(1) == 0
    rwt = rwt_ref[...]
    rw_hi = rwt.astype(BF16)
    rw_lo = (rwt - rw_hi.astype(F32)).astype(BF16)
    rw_both = jnp.concatenate([rw_hi, rw_lo], axis=0)
    m_all = (_dot(jnp.concatenate([s_ref[bb] for bb in range(nb)], axis=0), ws_ref[...])
             + lax.dot_general(jnp.concatenate([d_ref[bb] for bb in range(nb)], axis=1), wd_ref[...], TN_DIMS,
                               preferred_element_type=F32)
             + lax.dot_general(jnp.concatenate([a_ref[bb] for bb in range(nb)], axis=1), wa_ref[...], TN_DIMS,
                               preferred_element_type=F32))
    for bb in range(nb):
        mod = mod_ref[bb, 0]
        t_in = jnp.where(first, c_ref[bb], t_ref[bb])
        tn = t_in + mod[2:3] * _rms(m_all[TILE * bb:TILE * (bb + 1)], npost_ref[...])
        tn_ref[bb] = tn
        hf = _rms(tn, nffn_ref[...]) * (1.0 + mod[4:5]) + mod[3:4]
        hf_hi = hf.astype(BF16)
        hf_ref[bb] = hf_hi
        hf_lo = (hf - hf_hi.astype(F32)).astype(BF16)
        both = lax.dot_general(rw_both, hf_hi, NT_DIMS, preferred_element_type=F32)
        logits = (both[:N_EXPERTS] + both[N_EXPERTS:]
                  + lax.dot_general(rw_hi, hf_lo, NT_DIMS, preferred_element_type=F32) + rb_ref[...])
        e = jnp.exp(logits - jnp.max(logits, axis=0, keepdims=True))
        aff_ref[bb] = e / jnp.sum(e, axis=0, keepdims=True)


def _outproj(t, s, dt_, at_, mod, lw):
    nb = 2 if s.shape[0] % 2 == 0 else 1
    streams, stream_specs, (b, nt, d) = _token_stream(t, nb)
    nti = nt // TILE
    full = lambda a: pl.BlockSpec(a.shape, lambda bi, i: (0,) * a.ndim)
    ws = [lw["norm_mix_post"], lw["norm_ffn_pre"], lw["wo_s"], lw["wo_d"], lw["wo_a"], lw["rwt"], lw["rb"]]
    return pl.pallas_call(
        _outproj_kernel,
        grid=(b // nb, nti),
        in_specs=stream_specs + [
                  pl.BlockSpec((nb, TILE, SSD_INNER), lambda bi, i: (bi, i, 0)),
                  pl.BlockSpec((nb, DIFF_HEADS * DIFF_V, TILE), lambda bi, i: (bi, 0, i)),
                  pl.BlockSpec((nb, MLA_HEADS * MLA_V, TILE), lambda bi, i: (bi, 0, i)),
                  pl.BlockSpec((nb, 1, N_MOD, d), lambda bi, i: (bi, jnp.minimum(i, 1), 0, 0))]
                 + [full(a) for a in ws],
        out_specs=[pl.BlockSpec((nb, TILE, d), lambda bi, i: (bi, i, 0)),
                   pl.BlockSpec((nb, TILE, d), lambda bi, i: (bi, i, 0)),
                   pl.BlockSpec((nb, N_EXPERTS, TILE), lambda bi, i: (bi, 0, i))],
        out_shape=[jax.ShapeDtypeStruct((b, nt, d), F32),
                   jax.ShapeDtypeStruct((b, nt, d), BF16),
                   jax.ShapeDtypeStruct((b, N_EXPERTS, nt), F32)],
        compiler_params=_cparams(("arbitrary", "arbitrary"), VMEM_LIMIT),
        name="outproj",
    )(*streams, s, dt_, at_, mod, *ws)


def _route_kernel(aff_ref, pos_ref, gate_ref, cum_ref, *, nti, caps):
    ne = N_EXPERTS
    tri = (lax.broadcasted_iota(I32, (TILE, TILE), 0) < lax.broadcasted_iota(I32, (TILE, TILE), 1)).astype(BF16)
    lane = lax.broadcasted_iota(I32, (ne, LANES), 1)

    def excl_prefix(mask_f):
        return _dot(mask_f.astype(BF16), tri)

    cum_vec = jnp.zeros((ne, LANES), F32)
    total = jnp.zeros((ne, 1), F32)
    seg_bounds = ((0, 1, caps[0]), (1, nti, caps[1]))
    for t0, t1, cap in seg_bounds:
        xi = aff_ref[0, :, t0 * TILE:t1 * TILE]

        def bit_step(j, thr_bits, xi=xi, cap=cap):
            cand = thr_bits | (1 << (29 - j))
            cnt = jnp.sum((xi >= pltpu.bitcast(cand, F32)).astype(F32), axis=1, keepdims=True)
            return jnp.where(cnt >= cap, cand, thr_bits)

        thr = pltpu.bitcast(lax.fori_loop(0, 30, bit_step, jnp.zeros((ne, 1), I32)), F32)
        need = cap - jnp.sum((xi > thr).astype(F32), axis=1, keepdims=True)
        eq_seen = jnp.zeros((ne, 1), F32)
        for t in range(t0, t1):
            lo = (t - t0) * TILE
            xt = xi[:, lo:lo + TILE]
            eq = (xt == thr).astype(F32)
            eq_rank = eq_seen + excl_prefix(eq)
            sel = jnp.where(xt > thr, 1.0, eq * (eq_rank < need).astype(F32))
            eq_seen = eq_seen + jnp.sum(eq, axis=1, keepdims=True)
            rank = total + excl_prefix(sel)
            pos_ref[0, :, t * TILE:(t + 1) * TILE] = jnp.where(sel > 0.0, rank, -1.0).astype(I32)
            gate_ref[0, :, t * TILE:(t + 1) * TILE] = sel * aff_ref[0, :, t * TILE:(t + 1) * TILE]
            cum_vec = jnp.where(lane == t, total, cum_vec)
            total = total + jnp.sum(sel, axis=1, keepdims=True)
    cum_vec = jnp.where(lane == nti, total, cum_vec)
    cum_ref[0] = cum_vec.astype(I32)


def _route(aff, caps):
    b, ne, nt = aff.shape
    nti = nt // TILE
    kern = functools.partial(_route_kernel, nti=nti, caps=caps)
    return pl.pallas_call(
        kern,
        grid=(b,),
        in_specs=[pl.BlockSpec((1, ne, nt), lambda bi: (bi, 0, 0))],
        out_specs=[pl.BlockSpec((1, ne, nt), lambda bi: (bi, 0, 0)),
                   pl.BlockSpec((1, ne, nt), lambda bi: (bi, 0, 0)),
                   pl.BlockSpec((1, ne, LANES), lambda bi: (bi, 0, 0))],
        out_shape=[jax.ShapeDtypeStruct((b, ne, nt), I32),
                   jax.ShapeDtypeStruct((b, ne, nt), F32),
                   jax.ShapeDtypeStruct((b, ne, LANES), I32)],
        compiler_params=_cparams(("arbitrary",)),
        name="route",
    )(aff)


WIN = 64
GROUP = 4


def _tile_windows(cum_ref, b, t, rows):
    los = []
    rounds = jnp.int32(1)
    for e in range(N_EXPERTS):
        base = (b * N_EXPERTS + e) * LANES
        lo = (cum_ref[base + t] // 16) * 16
        los.append(lo)
        rounds = jnp.maximum(rounds, (cum_ref[base + t + 1] - lo + WIN - 1) // WIN)
    return los, rounds


def _window_onehot(pos_row, lo, r, rows):
    want = lo + WIN * r
    w0 = pl.multiple_of(jnp.minimum(want, rows - WIN), 16)
    rowid = w0 + lax.broadcasted_iota(I32, (WIN, TILE), 0)
    return w0, jnp.logical_and(pos_row == rowid, rowid >= want).astype(F32)


def _gather_kernel(cum_ref, hf_ref, pos_ref, gate_ref, xg_ref, gc_ref, *, rows):
    b = pl.program_id(0)
    t = pl.program_id(1)

    @pl.when(t == 0)
    def _():
        xg_ref[...] = jnp.zeros_like(xg_ref)
        gc_ref[...] = jnp.zeros_like(gc_ref)

    los, rounds = _tile_windows(cum_ref, b, t, rows)

    def round_step(r, carry):
        w0s, hots = [], []
        for e in range(N_EXPERTS):
            w0, hot = _window_onehot(pos_ref[0, e:e + 1, :], los[e], r, rows)
            w0s.append(w0)
            hots.append(hot)
            gc_ref[0, e, pl.ds(w0, WIN), :] += jnp.sum(hot * gate_ref[0, e:e + 1, :], axis=1, keepdims=True)
        res = _dot(jnp.concatenate(hots, axis=0).astype(BF16), hf_ref[0])
        for e in range(N_EXPERTS):
            xg_ref[0, e, pl.ds(w0s[e], WIN), :] += res[WIN * e:WIN * (e + 1)].astype(BF16)
        return carry

    lax.fori_loop(0, rounds, round_step, 0)


def _gather(cum_flat, hf, pos, gate, rows):
    b, nt, d = hf.shape
    nti = nt // TILE
    ne = pos.shape[1]
    kern = functools.partial(_gather_kernel, rows=rows)
    grid_spec = pltpu.PrefetchScalarGridSpec(
        num_scalar_prefetch=1,
        grid=(b, nti),
        in_specs=[pl.BlockSpec((1, TILE, d), lambda bi, i, cum: (bi, i, 0)),
                  pl.BlockSpec((1, ne, TILE), lambda bi, i, cum: (bi, 0, i)),
                  pl.BlockSpec((1, ne, TILE), lambda bi, i, cum: (bi, 0, i))],
        out_specs=[pl.BlockSpec((1, ne, rows, d), lambda bi, i, cum: (bi, 0, 0, 0)),
                   pl.BlockSpec((1, ne, rows, 1), lambda bi, i, cum: (bi, 0, 0, 0))],
    )
    return pl.pallas_call(
        kern,
        grid_spec=grid_spec,
        out_shape=[jax.ShapeDtypeStruct((b, ne, rows, d), BF16), jax.ShapeDtypeStruct((b, ne, rows, 1), F32)],
        compiler_params=_cparams(("arbitrary", "arbitrary"), VMEM_LIMIT),
        name="gather",
    )(cum_flat, hf, pos, gate)


def _experts_kernel(xg_ref, gc_ref, wg_ref, wu_ref, wd_ref, y_ref, wgb_ref, wub_ref, wdb_ref):
    @pl.when(pl.program_id(1) == 0)
    def _():
        wgb_ref[...] = wg_ref[0, 0].astype(BF16)
        wub_ref[...] = wu_ref[0, 0].astype(BF16)
        wdb_ref[...] = wd_ref[0, 0].astype(BF16)

    nb, _, rows, _ = xg_ref.shape
    half = rows // 2
    for bb in range(nb):
        for r in range(2):
            sl = slice(half * r, half * (r + 1))
            xg = xg_ref[bb, 0, sl, :]
            hid = (_silu(_dot(xg, wgb_ref[...])) * _dot(xg, wub_ref[...])).astype(BF16)
            y_ref[bb, 0, sl, :] = (_dot(hid, wdb_ref[...]) * gc_ref[bb, 0, sl, :]).astype(BF16)


def _experts(xg, gc, lw, l):
    b, ne, rows, d = xg.shape
    ff = lw["w_gate"].shape[3]
    nb = 2 if b % 2 == 0 else 1
    return pl.pallas_call(
        _experts_kernel,
        grid=(ne, b // nb),
        in_specs=[pl.BlockSpec((nb, 1, rows, d), lambda e, bi: (bi, e, 0, 0)),
                  pl.BlockSpec((nb, 1, rows, 1), lambda e, bi: (bi, e, 0, 0)),
                  pl.BlockSpec((1, 1, d, ff), lambda e, bi: (l, e, 0, 0)),
                  pl.BlockSpec((1, 1, d, ff), lambda e, bi: (l, e, 0, 0)),
                  pl.BlockSpec((1, 1, ff, d), lambda e, bi: (l, e, 0, 0))],
        out_specs=pl.BlockSpec((nb, 1, rows, d), lambda e, bi: (bi, e, 0, 0)),
        out_shape=jax.ShapeDtypeStruct((b, ne, rows, d), BF16),
        scratch_shapes=[pltpu.VMEM((d, ff), BF16), pltpu.VMEM((d, ff), BF16), pltpu.VMEM((ff, d), BF16)],
        compiler_params=_cparams(("arbitrary", "arbitrary"), VMEM_LIMIT),
        name="experts",
    )(xg, gc, lw["w_gate"], lw["w_up"], lw["w_down"])


def _combine_kernel(cum_ref, t_ref, y_ref, pos_ref, mod_ref, npost_ref, o_ref, f_ref, *, rows, latent_only):
    b = pl.program_id(0)
    t = pl.program_id(1)

    def run():
        los, rounds = _tile_windows(cum_ref, b, t, rows)

        def scatter_round(r):
            total = None
            for g in range(N_EXPERTS // GROUP):
                hots, wins = [], []
                for e in range(GROUP * g, GROUP * (g + 1)):
                    w0, hot = _window_onehot(pos_ref[0, e:e + 1, :], los[e], r, rows)
                    hots.append(hot)
                    wins.append(y_ref[0, e, pl.ds(w0, WIN), :])
                hot = jnp.concatenate(hots, axis=0).astype(BF16)
                part = lax.dot_general(hot, jnp.concatenate(wins, axis=0), TN_DIMS, preferred_element_type=F32)
                total = part if total is None else total + part
            return total

        f_ref[...] = scatter_round(0)

        def round_step(r, carry):
            f_ref[...] += scatter_round(r)
            return carry

        lax.fori_loop(1, rounds, round_step, 0)
        mod = mod_ref[0, 0]
        o_ref[0] = t_ref[0] + mod[5:6] * _rms(f_ref[...], npost_ref[...])

    if latent_only:
        pl.when(t > 0)(run)
    else:
        run()


def _combine(cum_flat, t, y, pos, mod, lw, latent_only):
    b, nt, d = t.shape
    nti = nt // TILE
    ne, rows = y.shape[1], y.shape[2]
    kern = functools.partial(_combine_kernel, rows=rows, latent_only=latent_only)
    if latent_only:
        out_rows, out_map = nt - TILE, lambda bi, i, cum: (bi, jnp.maximum(i - 1, 0), 0)
    else:
        out_rows, out_map = nt, lambda bi, i, cum: (bi, i, 0)
    grid_spec = pltpu.PrefetchScalarGridSpec(
        num_scalar_prefetch=1,
        grid=(b, nti),
        in_specs=[pl.BlockSpec((1, TILE, d), lambda bi, i, cum: (bi, i, 0)),
                  pl.BlockSpec((1, ne, rows, d), lambda bi, i, cum: (bi, 0, 0, 0)),
                  pl.BlockSpec((1, ne, TILE), lambda bi, i, cum: (bi, 0, i)),
                  pl.BlockSpec((1, 1, N_MOD, d), lambda bi, i, cum: (bi, jnp.minimum(i, 1), 0, 0)),
                  pl.BlockSpec((1, d), lambda bi, i, cum: (0, 0))],
        out_specs=pl.BlockSpec((1, TILE, d), out_map),
        scratch_shapes=[pltpu.VMEM((TILE, d), F32)],
    )
    return pl.pallas_call(
        kern,
        grid_spec=grid_spec,
        out_shape=jax.ShapeDtypeStruct((b, out_rows, d), F32),
        compiler_params=_cparams(("arbitrary", "arbitrary"), VMEM_LIMIT),
        name="combine",
    )(cum_flat, t, y, pos, mod, lw["norm_ffn_post"])


def _rope_tables(seq, ctx):
    quarter = MLA_ROPE // 4
    inv = ROPE_BASE ** (-jnp.arange(quarter, dtype=F32) / quarter)
    n_rows = seq // GRID_W
    rows = jnp.repeat(jnp.arange(n_rows, dtype=F32), GRID_W)
    cols = jnp.tile(jnp.arange(GRID_W, dtype=F32), n_rows)
    ar = rows[:, None] * inv[None, :]
    ac = cols[:, None] * inv[None, :]
    cos = jnp.concatenate([jnp.cos(ar), jnp.cos(ar), jnp.cos(ac), jnp.cos(ac)], axis=1)
    sin = jnp.concatenate([-jnp.sin(ar), jnp.sin(ar), -jnp.sin(ac), jnp.sin(ac)], axis=1)
    cos = jnp.concatenate([jnp.ones((ctx, MLA_ROPE), F32), cos], axis=0)
    sin = jnp.concatenate([jnp.zeros((ctx, MLA_ROPE), F32), sin], axis=0)
    return {"ck": jnp.tile(cos, (1, DIFF_MAPS)), "sk": jnp.tile(sin, (1, DIFF_MAPS)), "ct": cos.T, "st": sin.T}


def _partner_perm(width):
    idx = jnp.arange(width)
    r = idx % 16
    return jnp.where(r < 8, idx + 8, idx - 8)


def _layer_weights(l, p):
    d = p["w_in"].shape[1]
    w_in = p["w_in"][l]
    o_diff = 2 * SSD_INNER + 2 * SSD_BC + 2 * SSD_HEADS
    o_mla = o_diff + 3 * DIFF_HEADS * DIFF_V
    nk = DIFF_MAPS * DIFF_QK
    w_ssd = w_in[:, :o_diff]
    wa = jnp.concatenate([w_ssd, jnp.zeros((d, LANES - 2 * SSD_HEADS), F32)], axis=1)
    wq = w_in[:, o_diff:o_diff + nk]
    wk = w_in[:, o_diff + nk:o_diff + 2 * nk]
    wv = w_in[:, o_diff + 2 * nk:o_mla]
    wcq =w_in[:, o_mla:o_mla + MLA_Q_LORA]
    wckv = w_in[:, o_mla + MLA_Q_LORA:o_mla + MLA_Q_LORA + MLA_KV_LORA]
    wkr = w_in[:, o_mla + MLA_Q_LORA + MLA_KV_LORA:]
    zeros = lambda n: jnp.zeros((d, n), F32)
    wm = jnp.concatenate([wcq, zeros(256 - MLA_Q_LORA), wckv, wkr, wkr[:, _partner_perm(MLA_ROPE)],
                          zeros(512 - 448)], axis=1)

    wqu = p["mla_w_q_up"][l].reshape(MLA_Q_LORA, MLA_HEADS, MLA_NOPE + MLA_ROPE)
    pad = jnp.zeros((MLA_Q_LORA, MLA_HEADS, MLA_QK_PAD - MLA_NOPE - MLA_ROPE), F32)
    wqu_plain = jnp.concatenate([wqu, pad], axis=2).reshape(MLA_Q_LORA, -1)
    wkvu = p["mla_w_kv_up"][l].reshape(MLA_KV_LORA, MLA_HEADS, MLA_NOPE + MLA_V)
    wk2 = jnp.concatenate([wkvu[:, :, :MLA_NOPE],
                           jnp.zeros((MLA_KV_LORA, MLA_HEADS, MLA_QK_PAD - MLA_NOPE), F32)],
                          axis=2).reshape(MLA_KV_LORA, -1)
    eye = jnp.eye(MLA_ROPE, dtype=F32)
    ek_h = jnp.concatenate([jnp.zeros((MLA_ROPE, MLA_NOPE), F32), eye,
                            jnp.zeros((MLA_ROPE, MLA_QK_PAD - MLA_NOPE - MLA_ROPE), F32)], axis=1)
    ek = jnp.tile(ek_h, (1, MLA_HEADS))
    wv2 = wkvu[:, :, MLA_NOPE:].reshape(MLA_KV_LORA, -1)

    w_out = p["w_out"][l]
    row = lambda a: a.reshape(1, -1)
    col = lambda a: a.reshape(-1, 1)
    return {
        "norm_mix_pre": row(p["norm_mix_pre"][l]), "norm_mix_post": row(p["norm_mix_post"][l]),
        "norm_ffn_pre": row(p["norm_ffn_pre"][l]), "norm_ffn_post": row(p["norm_ffn_post"][l]),
        "wa": wa.astype(BF16),
        "wdk": wk.astype(BF16),
        "wm": wm.astype(BF16),
        "wqt": wq.T.astype(BF16),
        "wvt": wv.T.astype(BF16),
        "qnw": row(p["mla_q_norm"][l]), "kvnw": row(p["mla_kv_norm"][l]),
        "wqut": wqu_plain.T.astype(BF16),
        "wk2": wk2.astype(BF16), "ek": ek.astype(BF16), "wvt2": wv2.T.astype(BF16),
        "conv_w": p["ssd_conv_w"][l], "conv_b": row(p["ssd_conv_b"][l]),
        "alog": col(p["ssd_a_log"][l]), "dtb": col(p["ssd_dt_bias"][l]),
        "dsk": row(p["ssd_d"][l]), "ssd_norm": row(p["ssd_norm"][l]),
        "lq1": row(p["diff_lam_q1"][l]), "lk1": row(p["diff_lam_k1"][l]),
        "lq2": row(p["diff_lam_q2"][l]), "lk2": row(p["diff_lam_k2"][l]),
        "subw": p["diff_subln"][l].reshape(-1, 1),
        "wo_s": w_out[:SSD_INNER].astype(BF16),
        "wo_d": w_out[SSD_INNER:SSD_INNER + DIFF_HEADS * DIFF_V].astype(BF16),
        "wo_a": w_out[SSD_INNER + DIFF_HEADS * DIFF_V:].astype(BF16),
        "rwt": p["router_w"][l].T, "rb": p["router_b"][l].reshape(-1, 1),
        "w_gate": p["w_gate"], "w_up": p["w_up"], "w_down": p["w_down"],
    }


def kernel(x, c, ctx, c_ctx, ada_w, ada_b, norm_mix_pre, norm_mix_post, norm_ffn_pre, norm_ffn_post, w_in, ssd_conv_w, ssd_conv_b, ssd_a_log, ssd_dt_bias, ssd_d, ssd_norm, diff_lam_q1, diff_lam_k1, diff_lam_q2, diff_lam_k2, diff_subln, mla_q_norm, mla_w_q_up, mla_kv_norm, mla_w_kv_up, w_out, router_w, router_b, w_gate, w_up, w_down):
    p = dict(norm_mix_pre=norm_mix_pre, norm_mix_post=norm_mix_post, norm_ffn_pre=norm_ffn_pre,
             norm_ffn_post=norm_ffn_post, w_in=w_in, ssd_conv_w=ssd_conv_w, ssd_conv_b=ssd_conv_b,
             ssd_a_log=ssd_a_log, ssd_dt_bias=ssd_dt_bias, ssd_d=ssd_d, ssd_norm=ssd_norm,
             diff_lam_q1=diff_lam_q1, diff_lam_k1=diff_lam_k1, diff_lam_q2=diff_lam_q2, diff_lam_k2=diff_lam_k2,
             diff_subln=diff_subln, mla_q_norm=mla_q_norm, mla_w_q_up=mla_w_q_up, mla_kv_norm=mla_kv_norm,
             mla_w_kv_up=mla_w_kv_up, w_out=w_out, router_w=router_w, router_b=router_b,
             w_gate=w_gate, w_up=w_up, w_down=w_down)
    b, seq, d = x.shape
    nctx = ctx.shape[1]
    depth = ada_w.shape[0]
    assert nctx == TILE and seq % TILE == 0 and seq % GRID_W == 0
    nt = nctx + seq
    caps = (EC_CAPACITY * nctx // N_EXPERTS, EC_CAPACITY * seq // N_EXPERTS)
    assert caps[0] % 16 == 0 and caps[1] % 16 == 0 and caps[0] + caps[1] >= WIN

    cvec = jnp.concatenate([c, c_ctx[None, :], jnp.zeros((8 - b - 1, d), F32)], axis=0)
    mods = _adaln(cvec, ada_w, ada_b).reshape(depth, 8, N_MOD, d)
    tabs = _rope_tables(seq, nctx)
    t = (x, ctx)
    for l in range(depth):
        lw = _layer_weights(l, p)
        lambda_init = 0.8 - 0.6 * math.exp(-0.3 * l)
        mod = jnp.stack([jnp.broadcast_to(mods[l, b], (b, N_MOD, d)), mods[l, :b]], axis=1)
        z, xbc, dt, dq, dk, dv, mq, mk, mv = _inproj(t, mod, lw, tabs)
        s = _ssd(z, xbc, dt, lw)
        da = _diff_attn(dq, dk, dv, lw, lambda_init).reshape(b, DIFF_HEADS * DIFF_V, nt)
        aa = _mla_attn(mq, mk, mv).reshape(b, MLA_HEADS * MLA_V, nt)
        t, hf, aff = _outproj(t, s, da, aa, mod, lw)
        pos, gate, cum = _route(aff, caps)
        cum_flat = cum.reshape(-1)
        xg, gc = _gather(cum_flat, hf, pos, gate, caps[0] + caps[1])
        y = _experts(xg, gc, lw, l)
        t = _combine(cum_flat, t, y, pos, mod, lw, latent_only=(l == depth - 1))
    return t
```

```python
import functools
import math

import jax
import jax.numpy as jnp
from jax import lax
from jax.experimental import pallas as pl
from jax.experimental.pallas import tpu as pltpu

F32 = jnp.float32
BF16 = jnp.bfloat16
I32 = jnp.int32
HIGHEST = lax.Precision.HIGHEST

EPS = 1e-6
GRID_W = 64
ROPE_BASE = 10000.0
N_MOD = 6

SSD_HEADS = 8
SSD_HEAD_DIM = 64
SSD_INNER = SSD_HEADS * SSD_HEAD_DIM
SSD_GROUPS = 2
SSD_STATE = 64
SSD_CHUNK = 128
SSD_BC = SSD_GROUPS * SSD_STATE
SSD_CONV_DIM = SSD_INNER + 2 * SSD_BC
SSD_PAIRS = SSD_HEADS // 2

DIFF_HEADS = 4
DIFF_QK = 32
DIFF_V = 64
DIFF_MAPS = 2 * DIFF_HEADS

MLA_HEADS = 4
MLA_Q_LORA = 192
MLA_KV_LORA = 128
MLA_NOPE = 64
MLA_ROPE = 32
MLA_V = 64
MLA_QK_PAD = 128

N_EXPERTS = 16
EC_CAPACITY = 2

TILE = 256
LANES = 128
MAPS_PER_TILE = LANES // DIFF_QK
VMEM_LIMIT = 56 * 1024 * 1024

NT_DIMS = (((1,), (1,)), ((), ()))
TN_DIMS = (((0,), (0,)), ((), ()))


def _cparams(sem, vmem=None):
    return pltpu.CompilerParams(dimension_semantics=sem, vmem_limit_bytes=vmem)


def _rms(x, w):
    return x * lax.rsqrt(jnp.mean(x * x, axis=-1, keepdims=True) + EPS) * w


def _silu(x):
    return x * jax.nn.sigmoid(x)


def _dot(a, b):
    return jnp.dot(a, b, preferred_element_type=F32)


def _adaln_kernel(c_ref, w_ref, b_ref, o_ref):
    s = _silu(c_ref[...])
    w = w_ref[0]
    rows = s.shape[0]
    s_hi = s.astype(BF16).astype(F32)
    w_hi = w.astype(BF16)
    w_lo = (w - w_hi.astype(F32)).astype(BF16)
    both = _dot(jnp.concatenate([s_hi, s - s_hi], axis=0).astype(BF16), w_hi)
    o_ref[0] = both[:rows] + both[rows:] + _dot(s_hi.astype(BF16), w_lo) + b_ref[0]


def _adaln(cvec, ada_w, ada_b):
    depth, d, nd = ada_w.shape
    rows = cvec.shape[0]
    return pl.pallas_call(
        _adaln_kernel,
        grid=(depth, nd // d),
        in_specs=[pl.BlockSpec((rows, d), lambda l, j: (0, 0)),
                  pl.BlockSpec((1, d, d), lambda l, j: (l, 0, j)),
                  pl.BlockSpec((1, 1, d), lambda l, j: (l, 0, j))],
        out_specs=pl.BlockSpec((1, rows, d), lambda l, j: (l, 0, j)),
        out_shape=jax.ShapeDtypeStruct((depth, rows, nd), F32),
        compiler_params=_cparams(("arbitrary", "arbitrary")),
        name="adaln",
    )(cvec, ada_w, ada_b.reshape(depth, 1, nd))


def _token_stream(t, nb=1):
    if isinstance(t, tuple):
        x, ctx = t
        b, seq, d = x.shape
        specs = [pl.BlockSpec((nb, TILE, d), lambda bi, i: (bi, jnp.maximum(i - 1, 0), 0)),
                 pl.BlockSpec((nb, TILE, d), lambda bi, i: (bi, 0, 0))]
        return (x, ctx), specs, (b, seq + ctx.shape[1], d)
    b, nt, d = t.shape
    specs = [pl.BlockSpec((nb, TILE, d), lambda bi, i: (bi, i, 0)),
             pl.BlockSpec((nb, TILE, d), lambda bi, i: (bi, 0, 0))]
    return (t, t), specs, (b, nt, d)


def _partner_rows(x):
    parts = []
    for g in range(0, x.shape[0], 16):
        parts += [x[g + 8:g + 16], x[g:g + 8]]
    return jnp.concatenate(parts, axis=0)


def _partner_lanes(x):
    width = x.shape[1]
    lane = lax.broadcasted_iota(I32, x.shape, 1)
    return jnp.where((lane & 8) == 0, pltpu.roll(x, width - 8, 1), pltpu.roll(x, 8, 1))


def _inproj_kernel(x_ref, c_ref, mod_ref, nw_ref, wa_ref, wdk_ref, wm_ref, wqt_ref, wvt_ref,
                   ck_ref, sk_ref, ct_ref, st_ref, qnw_ref, kvnw_ref, wqut_ref, wk2_ref, ek_ref, wvt2_ref,
                   z_ref, xbc_ref, dt_ref, dq_ref, dk_ref, dv_ref, mq_ref, mk_ref, mv_ref):
    nb = x_ref.shape[0]
    first = pl.program_id(1) == 0
    hs = []
    for bb in range(nb):
        mod = mod_ref[bb, 0]
        x = jnp.where(first, c_ref[bb], x_ref[bb])
        hs.append((_rms(x, nw_ref[...]) * (1.0 + mod[1:2]) + mod[0:1]).astype(BF16))
    h = jnp.concatenate(hs, axis=0)
    tile = lambda bb: slice(TILE * bb, TILE * (bb + 1))
    tile_rows = lambda a: jnp.concatenate([a] * nb, axis=0)
    tile_lanes = lambda a: jnp.concatenate([a] * nb, axis=1)
    ck = tile_rows(ck_ref[...])
    sk = tile_rows(sk_ref[...])
    ct = tile_lanes(ct_ref[...])
    st = tile_lanes(st_ref[...])

    ra = _dot(h, wa_ref[...])
    for bb in range(nb):
        z_ref[bb] = ra[tile(bb), :SSD_INNER].astype(BF16)
        xbc_ref[bb] = ra[tile(bb), SSD_INNER:SSD_INNER + SSD_CONV_DIM]
        dt_ref[bb] = ra[tile(bb), SSD_INNER + SSD_CONV_DIM:]

    rk = _dot(h, wdk_ref[...])
    k = (rk * ck + _partner_lanes(rk) * sk).astype(BF16)
    for bb in range(nb):
        for g in range(DIFF_MAPS // MAPS_PER_TILE):
            dk_ref[bb, g, 0] = k[tile(bb), LANES * g:LANES * (g + 1)]

    rq = lax.dot_general(wqt_ref[...], h, NT_DIMS, preferred_element_type=F32)
    rq_partner = _partner_rows(rq)
    for m in range(DIFF_MAPS):
        lo = DIFF_QK * m
        qm = ((rq[lo:lo + DIFF_QK] * ct + rq_partner[lo:lo + DIFF_QK] * st) * DIFF_C_EXP).astype(BF16)
        above = DIFF_QK * (m % MAPS_PER_TILE)
        below = LANES - above - DIFF_QK
        parts = ([jnp.zeros((above, qm.shape[1]), BF16)] if above else []) + [qm]
        parts += [jnp.zeros((below, qm.shape[1]), BF16)] if below else []
        padded = jnp.concatenate(parts, axis=0)
        for bb in range(nb):
            dq_ref[bb, m] = padded[:, tile(bb)]

    rv = lax.dot_general(wvt_ref[...], h, NT_DIMS, preferred_element_type=F32).astype(BF16)
    for bb in range(nb):
        for hd in range(DIFF_HEADS):
            dv_ref[bb, hd, 0] = rv[DIFF_V * hd:DIFF_V * (hd + 1), tile(bb)]

    rm = _dot(h, wm_ref[...])
    cq = _rms(rm[:, :MLA_Q_LORA], qnw_ref[...]).astype(BF16)
    ckv = _rms(rm[:, 256:256 + MLA_KV_LORA], kvnw_ref[...]).astype(BF16)
    kr = rm[:, 384:384 + MLA_ROPE] * ck[:, :MLA_ROPE] + rm[:, 416:416 + MLA_ROPE] * sk[:, :MLA_ROPE]

    rq2 = lax.dot_general(wqut_ref[...], cq, NT_DIMS, preferred_element_type=F32)
    rq2_partner = _partner_rows(rq2)
    ones = jnp.ones((MLA_NOPE, ct.shape[1]), F32)
    pad1 = jnp.ones((MLA_QK_PAD - MLA_NOPE - MLA_ROPE, ct.shape[1]), F32)
    ct_h = jnp.concatenate([ones, ct, pad1], axis=0)
    st_h = jnp.concatenate([0.0 * ones, st, 0.0 * pad1], axis=0)
    for hd in range(MLA_HEADS):
        lo = MLA_QK_PAD * hd
        qh = ((rq2[lo:lo + MLA_QK_PAD] * ct_h + rq2_partner[lo:lo + MLA_QK_PAD] * st_h) * MLA_C_EXP).astype(BF16)
        for bb in range(nb):
            mq_ref[bb, hd] = qh[:, tile(bb)]

    k2 = (_dot(ckv, wk2_ref[...]) + _dot(kr.astype(BF16), ek_ref[...])).astype(BF16)
    rv2 = lax.dot_general(wvt2_ref[...], ckv, NT_DIMS, preferred_element_type=F32).astype(BF16)
    for bb in range(nb):
        for hd in range(MLA_HEADS):
            mk_ref[bb, hd, 0] = k2[tile(bb), MLA_QK_PAD * hd:MLA_QK_PAD * (hd + 1)]
            mv_ref[bb, hd, 0] = rv2[MLA_V * hd:MLA_V * (hd + 1), tile(bb)]


def _inproj(t, mod, lw, tabs):
    nb = 2 if mod.shape[0] % 2 == 0 else 1
    streams, stream_specs, (b, nt, d) = _token_stream(t, nb)
    nti = nt // TILE
    full = lambda a: pl.BlockSpec(a.shape, lambda bi, i: (0,) * a.ndim)
    tok = lambda w: pl.BlockSpec((TILE, w), lambda bi, i: (i, 0))
    tokt = lambda w: pl.BlockSpec((w, TILE), lambda bi, i: (0, i))
    ws = [lw["norm_mix_pre"], lw["wa"], lw["wdk"], lw["wm"], lw["wqt"], lw["wvt"]]
    ws2 = [lw["qnw"], lw["kvnw"], lw["wqut"], lw["wk2"], lw["ek"], lw["wvt2"]]
    out_shape = [
        jax.ShapeDtypeStruct((b, nt, SSD_INNER), BF16),
        jax.ShapeDtypeStruct((b, nt, SSD_CONV_DIM), F32),
        jax.ShapeDtypeStruct((b, nt, LANES), F32),
        jax.ShapeDtypeStruct((b, DIFF_MAPS, LANES, nt), BF16),
        jax.ShapeDtypeStruct((b, DIFF_MAPS // MAPS_PER_TILE, nti, TILE, LANES), BF16),
        jax.ShapeDtypeStruct((b, DIFF_HEADS, nti, DIFF_V, TILE), BF16),
        jax.ShapeDtypeStruct((b, MLA_HEADS, MLA_QK_PAD, nt), BF16),
        jax.ShapeDtypeStruct((b, MLA_HEADS, nti, TILE, MLA_QK_PAD), BF16),
        jax.ShapeDtypeStruct((b, MLA_HEADS, nti, MLA_V, TILE), BF16),
    ]
    out_specs = [
        pl.BlockSpec((nb, TILE, SSD_INNER), lambda bi, i: (bi, i, 0)),
        pl.BlockSpec((nb, TILE, SSD_CONV_DIM), lambda bi, i: (bi, i, 0)),
        pl.BlockSpec((nb, TILE, LANES), lambda bi, i: (bi, i, 0)),
        pl.BlockSpec((nb, DIFF_MAPS, LANES, TILE), lambda bi, i: (bi, 0, 0, i)),
        pl.BlockSpec((nb, DIFF_MAPS // MAPS_PER_TILE, 1, TILE, LANES), lambda bi, i: (bi, 0, i, 0, 0)),
        pl.BlockSpec((nb, DIFF_HEADS, 1, DIFF_V, TILE), lambda bi, i: (bi, 0, i, 0, 0)),
        pl.BlockSpec((nb, MLA_HEADS, MLA_QK_PAD, TILE), lambda bi, i: (bi, 0, 0, i)),
        pl.BlockSpec((nb, MLA_HEADS, 1, TILE, MLA_QK_PAD), lambda bi, i: (bi, 0, i, 0, 0)),
        pl.BlockSpec((nb, MLA_HEADS, 1, MLA_V, TILE), lambda bi, i: (bi, 0, i, 0, 0)),
    ]
    in_specs = (stream_specs
                + [pl.BlockSpec((nb, 1, N_MOD, d), lambda bi, i: (bi, jnp.minimum(i, 1), 0, 0))]
                + [full(a) for a in ws]
                + [tok(DIFF_MAPS * DIFF_QK), tok(DIFF_MAPS * DIFF_QK), tokt(DIFF_QK), tokt(DIFF_QK)]
                + [full(a) for a in ws2])
    return pl.pallas_call(
        _inproj_kernel,
        grid=(b // nb, nti),
        in_specs=in_specs,
        out_specs=out_specs,
        out_shape=out_shape,
        compiler_params=_cparams(("arbitrary", "arbitrary"), VMEM_LIMIT),
        name="inproj",
    )(*streams, mod, *ws, tabs["ck"], tabs["sk"], tabs["ct"], tabs["st"], *ws2)


def _ssd_chunk_of(ph, i, nck, nctx):
    back = jnp.where(i < nctx, nctx - 1 - i, nck - 1 + nctx - i)
    return jnp.where(ph == 0, back, i)


def _ssd_kernel(xc_ref, xp_ref, xn_ref, dt_ref, z_ref, cw_ref, cb_ref, alog_ref, dtb_ref, dsk_ref, nw_ref,
                o_ref, s_ref, sb_ref, *cache, nck, nctx, nb):
    ph = pl.program_id(1)
    i = pl.program_id(2)
    c = _ssd_chunk_of(ph, i, nck, nctx)

    @pl.when(i == 0)
    def _():
        s_ref[...] = jnp.zeros_like(s_ref)

    one = lambda ref, bb: ref.at[pl.ds(bb, 1)]
    fns = [_ssd_sample(one(xc_ref, bb), one(xp_ref, bb), one(xn_ref, bb), one(dt_ref, bb), one(z_ref, bb),
                       cw_ref, cb_ref, alog_ref, dtb_ref, dsk_ref, nw_ref, one(o_ref, bb),
                       s_ref.at[bb], sb_ref.at[bb], [r.at[bb] for r in cache], c, nck=nck, nctx=nctx)
           for bb in range(nb)]

    @pl.when(ph == 0)
    def _():
        for backward_states, _ in fns:
            backward_states()

    @pl.when(ph == 1)
    def _():
        for _, forward_and_output in fns:
            forward_and_output()


def _ssd_sample(xc_ref, xp_ref, xn_ref, dt_ref, z_ref, cw_ref, cb_ref, alog_ref, dtb_ref, dsk_ref, nw_ref,
                o_ref, s_ref, sb_ref, cache, c, *, nck, nctx):
    q = SSD_CHUNK
    nh2 = 2 * SSD_HEADS
    cx_ref, cbc_ref, crow_ref, ccol_ref, cbt_ref = cache
    ri = lax.broadcasted_iota(I32, (q, q), 0)
    ci = lax.broadcasted_iota(I32, (q, q), 1)
    lower = ci <= ri
    upper = ci >= ri
    lane = ci
    first_half_s = ri < SSD_STATE
    first_half_l = lane < SSD_HEAD_DIM
    blockdiag = first_half_s == first_half_l

    def chunk_values():
        x = xc_ref[0]
        has_prev = jnp.logical_and(c != 0, c != nctx)
        has_next = jnp.logical_and(c != nctx - 1, c != nck - 1)
        prev_row = jnp.where(has_prev, xp_ref[0][7:8, :], 0.0)
        next_row = jnp.where(has_next, xn_ref[0][0:1, :], 0.0)
        row = lax.broadcasted_iota(I32, x.shape, 0)
        xm1 = jnp.where(row == 0, prev_row, pltpu.roll(x, 1, 0))
        xp1 = jnp.where(row == q - 1, next_row, pltpu.roll(x, q - 1, 0))
        cw = cw_ref[...]
        u = _silu(xm1 * cw[0:1] + x * cw[1:2] + xp1 * cw[2:3] + cb_ref[...])
        xs = u[:, :SSD_INNER].astype(BF16)
        bc = u[:, SSD_INNER:]
        xdt = dt_ref[0].T[:nh2] + dtb_ref[...]
        dtt = jnp.maximum(xdt, 0.0) + jnp.log1p(jnp.exp(-jnp.abs(xdt)))
        dat = dtt * (-jnp.exp(alog_ref[...]) * math.log2(math.e))
        tri_dims = (((1,), (0,)), ((), ()))
        acf = lax.dot_general(dat, upper.astype(F32), tri_dims, precision=HIGHEST, preferred_element_type=F32)
        acb = lax.dot_general(dat, lower.astype(F32), tri_dims, precision=HIGHEST, preferred_element_type=F32)
        act = jnp.where(ri[:nh2] < SSD_HEADS, acf, acb)
        rows = jnp.concatenate([dtt, act], axis=0)
        cols = jnp.concatenate([rows, jnp.zeros((q - 2 * nh2, q), F32)], axis=0).T
        bt = bc[:, :SSD_BC].T
        return xs, bc, rows, cols, bt

    def bodies(xs, bc, rows, cols, bt):
        bm = bc[:, :SSD_BC]
        cm = bc[:, SSD_BC:]
        dtt = rows[:nh2]
        act = rows[nh2:]
        acc = pltpu.roll(cols, LANES - nh2, 1)
        fwd_rows = lax.broadcasted_iota(I32, (nh2, 1), 0) < SSD_HEADS
        alast = jnp.where(fwd_rows, act[:, q - 1:q], act[:, 0:1])
        w_rows = jnp.exp2(alast - act) * dtt
        decay_all = jnp.exp2(alast)

        def pair_cols(arr_c, h0):
            return jnp.where(first_half_l, arr_c[:, h0:h0 + 1], arr_c[:, h0 + 1:h0 + 2])

        def pair_rows(arr_t, h0):
            return jnp.where(first_half_s[:, 0:arr_t.shape[1]], arr_t[h0:h0 + 1, :], arr_t[h0 + 1:h0 + 2, :])

        def state_update(p, d):
            g = (2 * p) // (SSD_HEADS // SSD_GROUPS)
            h0 = d * SSD_HEADS + 2 * p
            btg = bt[SSD_STATE * g:SSD_STATE * (g + 1)]
            lhs = (jnp.concatenate([btg, btg], axis=0) * pair_rows(w_rows, h0)).astype(BF16)
            xs2 = xs[:, 2 * SSD_HEAD_DIM * p:2 * SSD_HEAD_DIM * (p + 1)]
            upd = jnp.where(blockdiag, _dot(lhs, xs2), 0.0)
            return pair_rows(decay_all, h0) * s_ref[d, p] + upd

        def backward():
            for p in range(SSD_PAIRS):
                sb_ref[c, p] = s_ref[1, p].astype(BF16)
                s_ref[1, p] = state_update(p, 1)

        def forward():
            roll_c = pltpu.roll(cm, SSD_STATE, 1)
            dsk = dsk_ref[...]
            ys = []
            for p in range(SSD_PAIRS):
                g = (2 * p) // (SSD_HEADS // SSD_GROUPS)
                cg_only = jnp.where((lane < SSD_STATE) == (g == 0), cm, 0.0).astype(BF16)
                cb = lax.dot_general(cg_only, bm.astype(BF16), NT_DIMS, preferred_element_type=F32)
                ms = []
                for hh in range(2):
                    hf = 2 * p + hh
                    hb = SSD_HEADS + hf
                    lf = (jnp.exp2(jnp.where(lower, acc[:, hf:hf + 1] - act[hf:hf + 1, :], -jnp.inf))
                          * dtt[hf:hf + 1, :])
                    lb = (jnp.exp2(jnp.where(upper, acc[:, hb:hb + 1] - act[hb:hb + 1, :], -jnp.inf))
                          * dtt[hb:hb + 1, :])
                    ms.append((cb * (lf + lb) + jnp.where(ri == ci, dsk[:, hf:hf + 1], 0.0)).astype(BF16))
                xs2 = xs[:, 2 * SSD_HEAD_DIM * p:2 * SSD_HEAD_DIM * (p + 1)]
                zero = jnp.zeros_like(xs2)
                rhs = jnp.concatenate([jnp.where(first_half_l, xs2, zero), jnp.where(first_half_l, zero, xs2)],
                                      axis=0)
                y = _dot(jnp.concatenate(ms, axis=1), rhs)
                cdup = jnp.where(first_half_l == (g == 0), cm, roll_c)
                ef = jnp.exp2(pair_cols(acc, 2 * p))
                eb = jnp.exp2(pair_cols(acc, SSD_HEADS + 2 * p))
                lhs_off = jnp.concatenate([cdup * ef, cdup * eb], axis=1).astype(BF16)
                rhs_off = jnp.concatenate([s_ref[0, p].astype(BF16), sb_ref[c, p]], axis=0)
                ys.append(y + _dot(lhs_off, rhs_off))
                s_ref[0, p] = state_update(p, 0)
            y = jnp.concatenate(ys, axis=1)
            zf = z_ref[0].astype(F32)
            gt = y * _silu(zf)
            nw = nw_ref[...]
            gw = SSD_INNER // SSD_GROUPS
            outs = [_rms(gt[:, gw * g:gw * (g + 1)], nw[:, gw * g:gw * (g + 1)]) for g in range(SSD_GROUPS)]
            o_ref[0] = jnp.concatenate(outs, axis=1).astype(BF16)

        return backward, forward

    def backward_states():
        xs, bc, rows, cols, bt = chunk_values()
        cx_ref[c] = xs
        cbc_ref[c] = bc
        crow_ref[c] = rows
        ccol_ref[c] = cols
        cbt_ref[c] = bt
        bodies(xs, bc, rows, cols, bt)[0]()

    def forward_and_output():
        bodies(cx_ref[c], cbc_ref[c], crow_ref[c], ccol_ref[c], cbt_ref[c])[1]()

    return backward_states, forward_and_output


def _ssd(z, xbc, dt, lw):
    b, nt, _ = z.shape
    q = SSD_CHUNK
    nck = nt // q
    nctx = TILE // q
    rows8 = q // 8
    nb = 1
    chunk = functools.partial(_ssd_chunk_of, nck=nck, nctx=nctx)
    full = lambda a: pl.BlockSpec(a.shape, lambda bi, ph, i: (0,) * a.ndim)
    ws = [lw["conv_w"], lw["conv_b"], lw["alog"], lw["dtb"], lw["dsk"], lw["ssd_norm"]]
    kern = functools.partial(_ssd_kernel, nck=nck, nctx=nctx, nb=nb)
    return pl.pallas_call(
        kern,
        grid=(b // nb, 2, nck),
        in_specs=[pl.BlockSpec((nb, q, SSD_CONV_DIM), lambda bi, ph, i: (bi, chunk(ph, i), 0)),
                  pl.BlockSpec((nb, 8, SSD_CONV_DIM),
                               lambda bi, ph, i: (bi, jnp.maximum(chunk(ph, i) * rows8 - 1, 0), 0)),
                  pl.BlockSpec((nb, 8, SSD_CONV_DIM),
                               lambda bi, ph, i: (bi, jnp.minimum((chunk(ph, i) + 1) * rows8, nck * rows8 - 1), 0)),
                  pl.BlockSpec((nb, q, LANES), lambda bi, ph, i: (bi, chunk(ph, i), 0)),
                  pl.BlockSpec((nb, q, SSD_INNER), lambda bi, ph, i: (bi, chunk(ph, i), 0))]
                 + [full(a) for a in ws],
        out_specs=pl.BlockSpec((nb, q, SSD_INNER), lambda bi, ph, i: (bi, jnp.where(ph == 0, 0, i), 0)),
        out_shape=jax.ShapeDtypeStruct((b, nt, SSD_INNER), BF16),
        scratch_shapes=[pltpu.VMEM((nb, 2, SSD_PAIRS, 2 * SSD_STATE, 2 * SSD_HEAD_DIM), F32),
                        pltpu.VMEM((nb, nck, SSD_PAIRS, 2 * SSD_STATE, 2 * SSD_HEAD_DIM), BF16),
                        pltpu.VMEM((nb, nck, q, SSD_INNER), BF16),
                        pltpu.VMEM((nb, nck, q, 2 * SSD_BC), F32),
                        pltpu.VMEM((nb, nck, 4 * SSD_HEADS, q), F32),
                        pltpu.VMEM((nb, nck, q, LANES), F32),
                        pltpu.VMEM((nb, nck, SSD_BC, q), F32)],
        compiler_params=_cparams(("arbitrary", "arbitrary", "arbitrary"), VMEM_LIMIT),
        name="ssd",
    )(xbc, xbc, xbc, dt, z, *ws)


ACC_ROWS = 80
DIFF_C_EXP = (DIFF_QK ** -0.5) * math.log2(math.e)
MLA_C_EXP = ((MLA_NOPE + MLA_ROPE) ** -0.5) * math.log2(math.e)


def _ones_rows(tk):
    return (lax.broadcasted_iota(I32, (ACC_ROWS - DIFF_V, tk), 0) == 0).astype(BF16)


def _score_step(kqs, s_ref):
    for idx, (k, q) in enumerate(kqs):
        s_ref[idx, 0:k.shape[0], :] = _dot(k, q)


def _softmax_pv_step(n_keys, vaugs, s_ref, m_ref, acc_ref):
    for idx in range(len(vaugs)):
        s = s_ref[idx, 0:n_keys, :]
        m = m_ref[idx]
        mn = jnp.maximum(m, jnp.max(s, axis=0, keepdims=True))
        p = jnp.exp2(s - mn).astype(BF16)
        acc_ref[idx] = acc_ref[idx] * jnp.exp2(m - mn) + _dot(vaugs[idx], p)
        m_ref[idx] = mn


def _plain_pv_step(n_keys, vaugs, s_ref, acc_ref):
    for idx in range(len(vaugs)):
        p = jnp.exp2(s_ref[idx, 0:n_keys, :]).astype(BF16)
        acc_ref[idx] += _dot(vaugs[idx], p)


def _plain_keys(kq_fn, v_fn, acc_ref, nti, group):
    def run(steps):
        work = [(c0, n, idx) for c0, n in steps for idx in range(acc_ref.shape[0])]
        kqs = {}
        vaugs = {}

        def score(item):
            c0, n, idx = item
            if (c0, n) not in kqs:
                kqs[(c0, n)] = kq_fn(c0, n)
            k, q = kqs[(c0, n)][idx]
            return _dot(k, q)

        nxt = score(work[0])
        for pos, (c0, n, idx) in enumerate(work):
            s = nxt
            if pos + 1 < len(work):
                nxt = score(work[pos + 1])
            if (c0, n) not in vaugs:
                vaugs[(c0, n)] = v_fn(c0, n)
            acc_ref[idx] += _dot(vaugs[(c0, n)][idx], jnp.exp2(s).astype(BF16))

    latent = pl.program_id(1) > 0
    pl.when(jnp.logical_not(latent))(lambda: run([(0, 1)]))
    pl.when(latent)(lambda: run([(0, 1)] + [(1 + g * group, group) for g in range((nti - 1) // group)]))


def _attn_init(m_ref, acc_ref):
    m_ref[...] = jnp.full(m_ref.shape, -jnp.inf, F32)
    acc_ref[...] = jnp.zeros_like(acc_ref)


MAX_UNSHIFTED_SCORE = 96.0


def _key_abs_max(k_ref, kmax_ref):
    @pl.when(pl.program_id(1) == 0)
    def _():
        lead = k_ref.shape[1]
        nti = k_ref.shape[2]

        def body(t, best):
            for g in range(lead):
                best = jnp.maximum(best, jnp.max(jnp.abs(k_ref[0, g, t].astype(F32))))
            return best

        kmax_ref[0] = lax.fori_loop(0, nti, body, jnp.float32(0.0))


def _scores_are_bounded(q_ref, kmax_ref):
    q = jnp.abs(q_ref[0].astype(F32))
    return kmax_ref[0] * jnp.max(jnp.sum(q, axis=1)) <= MAX_UNSHIFTED_SCORE


DIFF_KEY_GROUP = 2
MLA_KEY_GROUP = 4


def _key_group(nti, want):
    n_lat = nti - 1
    assert n_lat % 2 == 0
    while n_lat % (2 * want):
        want //= 2
    return want


def _chunks_k(k_ref, lead, c0, n):
    return jnp.concatenate([k_ref[lead + (c0 + j,)] for j in range(n)], axis=0)


def _chunks_v(v_ref, lead, c0, n):
    v = jnp.concatenate([v_ref[lead + (c0 + j,)] for j in range(n)], axis=1)
    return jnp.concatenate([v, _ones_rows(n * TILE)], axis=0)


def _pipelined_keys(scores, consume, nti, group):
    steps = (nti - 1) // group
    first = lambda k: 1 + (k - 1) * group
    scores(0, 1, 0)
    latent = pl.program_id(1) > 0

    @pl.when(jnp.logical_not(latent))
    def _():
        consume(0, 1, 0)

    @pl.when(latent)
    def _():
        scores(first(1), group, 1)
        consume(0, 1, 0)
        scores(first(2), group, 0)
        consume(first(1), group, 1)

        def body(j, carry):
            k = 2 * j
            scores(first(k + 1), group, 1)
            consume(first(k), group, 0)
            scores(first(k + 2), group, 0)
            consume(first(k + 1), group, 1)
            return carry

        lax.fori_loop(1, steps // 2, body, 0)
        consume(first(steps), group, 0)


def _diff_attn_kernel(lq1_ref, lk1_ref, lq2_ref, lk2_ref, subw_ref, q_ref, k_ref, v_ref, o_ref,
                      m_ref, acc_ref, sa_ref, sb_ref, kmax_ref, *, nti, group, lambda_init):
    _attn_init(m_ref, acc_ref)
    _key_abs_max(k_ref, kmax_ref)
    slots = (sa_ref, sb_ref)

    def kq_pairs(c0, n):
        ks = [_chunks_k(k_ref, (0, g), c0, n) for g in range(DIFF_MAPS // MAPS_PER_TILE)]
        return [(ks[m // MAPS_PER_TILE], q_ref[0, m]) for m in range(DIFF_MAPS)]

    def values(c0, n):
        vaugs = []
        for h in range(DIFF_HEADS):
            vaugs += [_chunks_v(v_ref, (0, h), c0, n)] * 2
        return vaugs

    def scores(c0, n, slot):
        _score_step(kq_pairs(c0, n), slots[slot])

    def consume(c0, n, slot):
        _softmax_pv_step(n * TILE, values(c0, n), slots[slot], m_ref, acc_ref)

    bounded = _scores_are_bounded(q_ref, kmax_ref)
    pl.when(bounded)(lambda: _plain_keys(kq_pairs, values, acc_ref, nti, 4))
    pl.when(jnp.logical_not(bounded))(lambda: _pipelined_keys(scores, consume, nti, group))
    lam =(jnp.exp(jnp.sum(lq1_ref[...] * lk1_ref[...], keepdims=True))
           - jnp.exp(jnp.sum(lq2_ref[...] * lk2_ref[...], keepdims=True)) + lambda_init)
    for h in range(DIFF_HEADS):
        a1 = acc_ref[2 * h]
        a2 = acc_ref[2 * h + 1]
        o = a1[:DIFF_V] / a1[DIFF_V:DIFF_V + 1] - lam * (a2[:DIFF_V] / a2[DIFF_V:DIFF_V + 1])
        o = o * lax.rsqrt(jnp.mean(o * o, axis=0, keepdims=True) + EPS) * subw_ref[...]
        o_ref[0, h] = (o * (1.0 - lambda_init)).astype(BF16)


def _diff_attn(dq, dk, dv, lw, lambda_init):
    b, nmaps, _, nt = dq.shape
    nh = nmaps // 2
    nti = nt // TILE
    group = _key_group(nti, DIFF_KEY_GROUP)
    kern = functools.partial(_diff_attn_kernel, nti=nti, group=group, lambda_init=lambda_init)
    score_slot = pltpu.VMEM((2 * nh, group * TILE, TILE), F32)
    vec = pl.BlockSpec((1, DIFF_QK), lambda bi, i: (0, 0))
    return pl.pallas_call(
        kern,
        grid=(b, nti),
        in_specs=[vec, vec, vec, vec,
                  pl.BlockSpec((DIFF_V, 1), lambda bi, i: (0, 0)),
                  pl.BlockSpec((1, nmaps, LANES, TILE), lambda bi, i: (bi, 0, 0, i)),
                  pl.BlockSpec((1, nmaps // MAPS_PER_TILE, nti, TILE, LANES), lambda bi, i: (bi, 0, 0, 0, 0)),
                  pl.BlockSpec((1, nh, nti, DIFF_V, TILE), lambda bi, i: (bi, 0, 0, 0, 0))],
        out_specs=pl.BlockSpec((1, nh, DIFF_V, TILE), lambda bi, i: (bi, 0, 0, i)),
        out_shape=jax.ShapeDtypeStruct((b, nh, DIFF_V, nt), BF16),
        scratch_shapes=[pltpu.VMEM((2 * nh, 1, TILE), F32), pltpu.VMEM((2 * nh, ACC_ROWS, TILE), F32),
                        score_slot, score_slot, pltpu.SMEM((1,), F32)],
        compiler_params=_cparams(("arbitrary", "arbitrary"), VMEM_LIMIT),
        name="diff_attn",
    )(lw["lq1"], lw["lk1"], lw["lq2"], lw["lk2"], lw["subw"], dq, dk, dv)


def _mla_attn_kernel(q_ref, k_ref, v_ref, o_ref, m_ref, acc_ref, sa_ref, sb_ref, kmax_ref, *, nti, group):
    _attn_init(m_ref, acc_ref)
    _key_abs_max(k_ref, kmax_ref)
    slots = (sa_ref, sb_ref)

    def kq_pairs(c0, n):
        return [(_chunks_k(k_ref, (0, h), c0, n), q_ref[0, h]) for h in range(MLA_HEADS)]

    def values(c0, n):
        return [_chunks_v(v_ref, (0, h), c0, n) for h in range(MLA_HEADS)]

    def scores(c0, n, slot):
        _score_step(kq_pairs(c0, n), slots[slot])

    def consume(c0, n, slot):
        _softmax_pv_step(n * TILE, values(c0, n), slots[slot], m_ref, acc_ref)

    def consume_plain(c0, n, slot):
        _plain_pv_step(n * TILE, values(c0, n), slots[slot], acc_ref)

    bounded = _scores_are_bounded(q_ref, kmax_ref)
    pl.when(bounded)(lambda: _pipelined_keys(scores, consume_plain, nti, group))
    pl.when(jnp.logical_not(bounded))(lambda: _pipelined_keys(scores, consume, nti, group))
    for h in range(MLA_HEADS):
        a = acc_ref[h]
        o_ref[0, h] = (a[:MLA_V] / a[MLA_V:MLA_V + 1]).astype(BF16)


def _mla_attn(mq, mk, mv):
    b, nh, dpad, nt = mq.shape
    nti = nt // TILE
    group = _key_group(nti, MLA_KEY_GROUP)
    kern = functools.partial(_mla_attn_kernel, nti=nti, group=group)
    score_slot = pltpu.VMEM((nh, group * TILE, TILE), F32)
    return pl.pallas_call(
        kern,
        grid=(b, nti),
        in_specs=[pl.BlockSpec((1, nh, dpad, TILE), lambda bi, i: (bi, 0, 0, i)),
                  pl.BlockSpec((1, nh, nti, TILE, dpad), lambda bi, i: (bi, 0, 0, 0, 0)),
                  pl.BlockSpec((1, nh, nti, MLA_V, TILE), lambda bi, i: (bi, 0, 0, 0, 0))],
        out_specs=pl.BlockSpec((1, nh, MLA_V, TILE), lambda bi, i: (bi, 0, 0, i)),
        out_shape=jax.ShapeDtypeStruct((b, nh, MLA_V, nt), BF16),
        scratch_shapes=[pltpu.VMEM((nh, 1, TILE), F32), pltpu.VMEM((nh, ACC_ROWS, TILE), F32),
                        score_slot, score_slot, pltpu.SMEM((1,), F32)],
        compiler_params=_cparams(("arbitrary", "arbitrary"), VMEM_LIMIT),
        name="mla_attn",
    )(mq, mk, mv)


def _outproj_kernel(t_ref, c_ref, s_ref, d_ref, a_ref, mod_ref, npost_ref, nffn_ref, ws_ref, wd_ref, wa_ref,
                    rwt_ref, rb_ref, tn_ref, hf_ref, aff_ref):
    nb = s_ref.shape[0]
    first = pl.program_id(1) == 0
    rwt = rwt_ref[...]
    rw_hi = rwt.astype(BF16)
    rw_lo = (rwt - rw_hi.astype(F32)).astype(BF16)
    rw_both = jnp.concatenate([rw_hi, rw_lo], axis=0)
    m_all = (_dot(jnp.concatenate([s_ref[bb] for bb in range(nb)], axis=0), ws_ref[...])
             + lax.dot_general(jnp.concatenate([d_ref[bb] for bb in range(nb)], axis=1), wd_ref[...], TN_DIMS,
                               preferred_element_type=F32)
             + lax.dot_general(jnp.concatenate([a_ref[bb] for bb in range(nb)], axis=1), wa_ref[...], TN_DIMS,
                               preferred_element_type=F32))
    for bb in range(nb):
        mod = mod_ref[bb, 0]
        t_in = jnp.where(first, c_ref[bb], t_ref[bb])
        tn = t_in + mod[2:3] * _rms(m_all[TILE * bb:TILE * (bb + 1)], npost_ref[...])
        tn_ref[bb] = tn
        hf = _rms(tn, nffn_ref[...]) * (1.0 + mod[4:5]) + mod[3:4]
        hf_hi = hf.astype(BF16)
        hf_ref[bb] = hf_hi
        hf_lo = (hf - hf_hi.astype(F32)).astype(BF16)
        both = lax.dot_general(rw_both, hf_hi, NT_DIMS, preferred_element_type=F32)
        logits = (both[:N_EXPERTS] + both[N_EXPERTS:]
                  + lax.dot_general(rw_hi, hf_lo, NT_DIMS, preferred_element_type=F32) + rb_ref[...])
        e = jnp.exp(logits - jnp.max(logits, axis=0, keepdims=True))
        aff_ref[bb] = e / jnp.sum(e, axis=0, keepdims=True)


def _outproj(t, s, dt_, at_, mod, lw):
    nb = 2 if s.shape[0] % 2 == 0 else 1
    streams, stream_specs, (b, nt, d) = _token_stream(t, nb)
    nti = nt // TILE
    full = lambda a: pl.BlockSpec(a.shape, lambda bi, i: (0,) * a.ndim)
    ws = [lw["norm_mix_post"], lw["norm_ffn_pre"], lw["wo_s"], lw["wo_d"], lw["wo_a"], lw["rwt"], lw["rb"]]
    return pl.pallas_call(
        _outproj_kernel,
        grid=(b // nb, nti),
        in_specs=stream_specs + [
                  pl.BlockSpec((nb, TILE, SSD_INNER), lambda bi, i: (bi, i, 0)),
                  pl.BlockSpec((nb, DIFF_HEADS * DIFF_V, TILE), lambda bi, i: (bi, 0, i)),
                  pl.BlockSpec((nb, MLA_HEADS * MLA_V, TILE), lambda bi, i: (bi, 0, i)),
                  pl.BlockSpec((nb, 1, N_MOD, d), lambda bi, i: (bi, jnp.minimum(i, 1), 0, 0))]
                 + [full(a) for a in ws],
        out_specs=[pl.BlockSpec((nb, TILE, d), lambda bi, i: (bi, i, 0)),
                   pl.BlockSpec((nb, TILE, d), lambda bi, i: (bi, i, 0)),
                   pl.BlockSpec((nb, N_EXPERTS, TILE), lambda bi, i: (bi, 0, i))],
        out_shape=[jax.ShapeDtypeStruct((b, nt, d), F32),
                   jax.ShapeDtypeStruct((b, nt, d), BF16),
                   jax.ShapeDtypeStruct((b, N_EXPERTS, nt), F32)],
        compiler_params=_cparams(("arbitrary", "arbitrary"), VMEM_LIMIT),
        name="outproj",
    )(*streams, s, dt_, at_, mod, *ws)


def _route_kernel(aff_ref, pos_ref, gate_ref, cum_ref, *, nti, caps):
    ne = N_EXPERTS
    tri = (lax.broadcasted_iota(I32, (TILE, TILE), 0) < lax.broadcasted_iota(I32, (TILE, TILE), 1)).astype(BF16)
    lane = lax.broadcasted_iota(I32, (ne, LANES), 1)

    def excl_prefix(mask_f):
        return _dot(mask_f.astype(BF16), tri)

    cum_vec = jnp.zeros((ne, LANES), F32)
    total = jnp.zeros((ne, 1), F32)
    seg_bounds = ((0, 1, caps[0]), (1, nti, caps[1]))
    for t0, t1, cap in seg_bounds:
        xi = aff_ref[0, :, t0 * TILE:t1 * TILE]

        def bit_step(j, thr_bits, xi=xi, cap=cap):
            cand = thr_bits | (1 << (29 - j))
            cnt = jnp.sum((xi >= pltpu.bitcast(cand, F32)).astype(F32), axis=1, keepdims=True)
            return jnp.where(cnt >= cap, cand, thr_bits)

        thr = pltpu.bitcast(lax.fori_loop(0, 30, bit_step, jnp.zeros((ne, 1), I32)), F32)
        need = cap - jnp.sum((xi > thr).astype(F32), axis=1, keepdims=True)
        eq_seen = jnp.zeros((ne, 1), F32)
        for t in range(t0, t1):
            lo = (t - t0) * TILE
            xt = xi[:, lo:lo + TILE]
            eq = (xt == thr).astype(F32)
            eq_rank = eq_seen + excl_prefix(eq)
            sel = jnp.where(xt > thr, 1.0, eq * (eq_rank < need).astype(F32))
            eq_seen = eq_seen + jnp.sum(eq, axis=1, keepdims=True)
            rank = total + excl_prefix(sel)
            pos_ref[0, :, t * TILE:(t + 1) * TILE] = jnp.where(sel > 0.0, rank, -1.0).astype(I32)
            gate_ref[0, :, t * TILE:(t + 1) * TILE] = sel * aff_ref[0, :, t * TILE:(t + 1) * TILE]
            cum_vec = jnp.where(lane == t, total, cum_vec)
            total = total + jnp.sum(sel, axis=1, keepdims=True)
    cum_vec = jnp.where(lane == nti, total, cum_vec)
    cum_ref[0] = cum_vec.astype(I32)


def _route(aff, caps):
    b, ne, nt = aff.shape
    nti = nt // TILE
    kern = functools.partial(_route_kernel, nti=nti, caps=caps)
    return pl.pallas_call(
        kern,
        grid=(b,),
        in_specs=[pl.BlockSpec((1, ne, nt), lambda bi: (bi, 0, 0))],
        out_specs=[pl.BlockSpec((1, ne, nt), lambda bi: (bi, 0, 0)),
                   pl.BlockSpec((1, ne, nt), lambda bi: (bi, 0, 0)),
                   pl.BlockSpec((1, ne, LANES), lambda bi: (bi, 0, 0))],
        out_shape=[jax.ShapeDtypeStruct((b, ne, nt), I32),
                   jax.ShapeDtypeStruct((b, ne, nt), F32),
                   jax.ShapeDtypeStruct((b, ne, LANES), I32)],
        compiler_params=_cparams(("arbitrary",)),
        name="route",
    )(aff)


WIN = 64
GROUP = 4


def _tile_windows(cum_ref, b, t, rows):
    los = []
    rounds = jnp.int32(1)
    for e in range(N_EXPERTS):
        base = (b * N_EXPERTS + e) * LANES
        lo = (cum_ref[base + t] // 16) * 16
        los.append(lo)
        rounds = jnp.maximum(rounds, (cum_ref[base + t + 1] - lo + WIN - 1) // WIN)
    return los, rounds


def _window_onehot(pos_row, lo, r, rows):
    want = lo + WIN * r
    w0 = pl.multiple_of(jnp.minimum(want, rows - WIN), 16)
    rowid = w0 + lax.broadcasted_iota(I32, (WIN, TILE), 0)
    return w0, jnp.logical_and(pos_row == rowid, rowid >= want).astype(F32)


def _gather_kernel(cum_ref, hf_ref, pos_ref, gate_ref, xg_ref, gc_ref, *, rows):
    b = pl.program_id(0)
    t = pl.program_id(1)

    @pl.when(t == 0)
    def _():
        xg_ref[...] = jnp.zeros_like(xg_ref)
        gc_ref[...] = jnp.zeros_like(gc_ref)

    los, rounds = _tile_windows(cum_ref, b, t, rows)

    def round_step(r, carry):
        w0s, hots = [], []
        for e in range(N_EXPERTS):
            w0, hot = _window_onehot(pos_ref[0, e:e + 1, :], los[e], r, rows)
            w0s.append(w0)
            hots.append(hot)
            gc_ref[0, e, pl.ds(w0, WIN), :] += jnp.sum(hot * gate_ref[0, e:e + 1, :], axis=1, keepdims=True)
        res = _dot(jnp.concatenate(hots, axis=0).astype(BF16), hf_ref[0])
        for e in range(N_EXPERTS):
            xg_ref[0, e, pl.ds(w0s[e], WIN), :] += res[WIN * e:WIN * (e + 1)].astype(BF16)
        return carry

    lax.fori_loop(0, rounds, round_step, 0)


def _gather(cum_flat, hf, pos, gate, rows):
    b, nt, d = hf.shape
    nti = nt // TILE
    ne = pos.shape[1]
    kern = functools.partial(_gather_kernel, rows=rows)
    grid_spec = pltpu.PrefetchScalarGridSpec(
        num_scalar_prefetch=1,
        grid=(b, nti),
        in_specs=[pl.BlockSpec((1, TILE, d), lambda bi, i, cum: (bi, i, 0)),
                  pl.BlockSpec((1, ne, TILE), lambda bi, i, cum: (bi, 0, i)),
                  pl.BlockSpec((1, ne, TILE), lambda bi, i, cum: (bi, 0, i))],
        out_specs=[pl.BlockSpec((1, ne, rows, d), lambda bi, i, cum: (bi, 0, 0, 0)),
                   pl.BlockSpec((1, ne, rows, 1), lambda bi, i, cum: (bi, 0, 0, 0))],
    )
    return pl.pallas_call(
        kern,
        grid_spec=grid_spec,
        out_shape=[jax.ShapeDtypeStruct((b, ne, rows, d), BF16), jax.ShapeDtypeStruct((b, ne, rows, 1), F32)],
        compiler_params=_cparams(("arbitrary", "arbitrary"), VMEM_LIMIT),
        name="gather",
    )(cum_flat, hf, pos, gate)


def _experts_kernel(xg_ref, gc_ref, wg_ref, wu_ref, wd_ref, y_ref, wgb_ref, wub_ref, wdb_ref):
    @pl.when(pl.program_id(1) == 0)
    def _():
        wgb_ref[...] = wg_ref[0, 0].astype(BF16)
        wub_ref[...] = wu_ref[0, 0].astype(BF16)
        wdb_ref[...] = wd_ref[0, 0].astype(BF16)

    nb, _, rows, _ = xg_ref.shape
    half = rows // 2
    for bb in range(nb):
        for r in range(2):
            sl = slice(half * r, half * (r + 1))
            xg = xg_ref[bb, 0, sl, :]
            hid = (_silu(_dot(xg, wgb_ref[...])) * _dot(xg, wub_ref[...])).astype(BF16)
            y_ref[bb, 0, sl, :] = (_dot(hid, wdb_ref[...]) * gc_ref[bb, 0, sl, :]).astype(BF16)


def _experts(xg, gc, lw, l):
    b, ne, rows, d = xg.shape
    ff = lw["w_gate"].shape[3]
    nb = 2 if b % 2 == 0 else 1
    return pl.pallas_call(
        _experts_kernel,
        grid=(ne, b // nb),
        in_specs=[pl.BlockSpec((nb, 1, rows, d), lambda e, bi: (bi, e, 0, 0)),
                  pl.BlockSpec((nb, 1, rows, 1), lambda e, bi: (bi, e, 0, 0)),
                  pl.BlockSpec((1, 1, d, ff), lambda e, bi: (l, e, 0, 0)),
                  pl.BlockSpec((1, 1, d, ff), lambda e, bi: (l, e, 0, 0)),
                  pl.BlockSpec((1, 1, ff, d), lambda e, bi: (l, e, 0, 0))],
        out_specs=pl.BlockSpec((nb, 1, rows, d), lambda e, bi: (bi, e, 0, 0)),
        out_shape=jax.ShapeDtypeStruct((b, ne, rows, d), BF16),
        scratch_shapes=[pltpu.VMEM((d, ff), BF16), pltpu.VMEM((d, ff), BF16), pltpu.VMEM((ff, d), BF16)],
        compiler_params=_cparams(("arbitrary", "arbitrary"), VMEM_LIMIT),
        name="experts",
    )(xg, gc, lw["w_gate"], lw["w_up"], lw["w_down"])


def _combine_kernel(cum_ref, t_ref, y_ref, pos_ref, mod_ref, npost_ref, o_ref, f_ref, *, rows, latent_only):
    b = pl.program_id(0)
    t = pl.program_id(1)

    def run():
        los, rounds = _tile_windows(cum_ref, b, t, rows)

        def scatter_round(r):
            total = None
            for g in range(N_EXPERTS // GROUP):
                hots, wins = [], []
                for e in range(GROUP * g, GROUP * (g + 1)):
                    w0, hot = _window_onehot(pos_ref[0, e:e + 1, :], los[e], r, rows)
                    hots.append(hot)
                    wins.append(y_ref[0, e, pl.ds(w0, WIN), :])
                hot = jnp.concatenate(hots, axis=0).astype(BF16)
                part = lax.dot_general(hot, jnp.concatenate(wins, axis=0), TN_DIMS, preferred_element_type=F32)
                total = part if total is None else total + part
            return total

        f_ref[...] = scatter_round(0)

        def round_step(r, carry):
            f_ref[...] += scatter_round(r)
            return carry

        lax.fori_loop(1, rounds, round_step, 0)
        mod = mod_ref[0, 0]
        o_ref[0] = t_ref[0] + mod[5:6] * _rms(f_ref[...], npost_ref[...])

    if latent_only:
        pl.when(t > 0)(run)
    else:
        run()


def _combine(cum_flat, t, y, pos, mod, lw, latent_only):
    b, nt, d = t.shape
    nti = nt // TILE
    ne, rows = y.shape[1], y.shape[2]
    kern = functools.partial(_combine_kernel, rows=rows, latent_only=latent_only)
    if latent_only:
        out_rows, out_map = nt - TILE, lambda bi, i, cum: (bi, jnp.maximum(i - 1, 0), 0)
    else:
        out_rows, out_map = nt, lambda bi, i, cum: (bi, i, 0)
    grid_spec = pltpu.PrefetchScalarGridSpec(
        num_scalar_prefetch=1,
        grid=(b, nti),
        in_specs=[pl.BlockSpec((1, TILE, d), lambda bi, i, cum: (bi, i, 0)),
                  pl.BlockSpec((1, ne, rows, d), lambda bi, i, cum: (bi, 0, 0, 0)),
                  pl.BlockSpec((1, ne, TILE), lambda bi, i, cum: (bi, 0, i)),
                  pl.BlockSpec((1, 1, N_MOD, d), lambda bi, i, cum: (bi, jnp.minimum(i, 1), 0, 0)),
                  pl.BlockSpec((1, d), lambda bi, i, cum: (0, 0))],
        out_specs=pl.BlockSpec((1, TILE, d), out_map),
        scratch_shapes=[pltpu.VMEM((TILE, d), F32)],
    )
    return pl.pallas_call(
        kern,
        grid_spec=grid_spec,
        out_shape=jax.ShapeDtypeStruct((b, out_rows, d), F32),
        compiler_params=_cparams(("arbitrary", "arbitrary"), VMEM_LIMIT),
        name="combine",
    )(cum_flat, t, y, pos, mod, lw["norm_ffn_post"])


def _rope_tables(seq, ctx):
    quarter = MLA_ROPE // 4
    inv = ROPE_BASE ** (-jnp.arange(quarter, dtype=F32) / quarter)
    n_rows = seq // GRID_W
    rows = jnp.repeat(jnp.arange(n_rows, dtype=F32), GRID_W)
    cols = jnp.tile(jnp.arange(GRID_W, dtype=F32), n_rows)
    ar = rows[:, None] * inv[None, :]
    ac = cols[:, None] * inv[None, :]
    cos = jnp.concatenate([jnp.cos(ar), jnp.cos(ar), jnp.cos(ac), jnp.cos(ac)], axis=1)
    sin = jnp.concatenate([-jnp.sin(ar), jnp.sin(ar), -jnp.sin(ac), jnp.sin(ac)], axis=1)
    cos = jnp.concatenate([jnp.ones((ctx, MLA_ROPE), F32), cos], axis=0)
    sin = jnp.concatenate([jnp.zeros((ctx, MLA_ROPE), F32), sin], axis=0)
    return {"ck": jnp.tile(cos, (1, DIFF_MAPS)), "sk": jnp.tile(sin, (1, DIFF_MAPS)), "ct": cos.T, "st": sin.T}


def _partner_perm(width):
    idx = jnp.arange(width)
    r = idx % 16
    return jnp.where(r < 8, idx + 8, idx - 8)


def _layer_weights(l, p):
    d = p["w_in"].shape[1]
    w_in = p["w_in"][l]
    o_diff = 2 * SSD_INNER + 2 * SSD_BC + 2 * SSD_HEADS
    o_mla = o_diff + 3 * DIFF_HEADS * DIFF_V
    nk = DIFF_MAPS * DIFF_QK
    w_ssd = w_in[:, :o_diff]
    wa = jnp.concatenate([w_ssd, jnp.zeros((d, LANES - 2 * SSD_HEADS), F32)], axis=1)
    wq = w_in[:, o_diff:o_diff + nk]
    wk = w_in[:, o_diff + nk:o_diff + 2 * nk]
    wv = w_in[:, o_diff + 2 * nk:o_mla]
    wcq =w_in[:, o_mla:o_mla + MLA_Q_LORA]
    wckv = w_in[:, o_mla + MLA_Q_LORA:o_mla + MLA_Q_LORA + MLA_KV_LORA]
    wkr = w_in[:, o_mla + MLA_Q_LORA + MLA_KV_LORA:]
    zeros = lambda n: jnp.zeros((d, n), F32)
    wm = jnp.concatenate([wcq, zeros(256 - MLA_Q_LORA), wckv, wkr, wkr[:, _partner_perm(MLA_ROPE)],
                          zeros(512 - 448)], axis=1)

    wqu = p["mla_w_q_up"][l].reshape(MLA_Q_LORA, MLA_HEADS, MLA_NOPE + MLA_ROPE)
    pad = jnp.zeros((MLA_Q_LORA, MLA_HEADS, MLA_QK_PAD - MLA_NOPE - MLA_ROPE), F32)
    wqu_plain = jnp.concatenate([wqu, pad], axis=2).reshape(MLA_Q_LORA, -1)
    wkvu = p["mla_w_kv_up"][l].reshape(MLA_KV_LORA, MLA_HEADS, MLA_NOPE + MLA_V)
    wk2 = jnp.concatenate([wkvu[:, :, :MLA_NOPE],
                           jnp.zeros((MLA_KV_LORA, MLA_HEADS, MLA_QK_PAD - MLA_NOPE), F32)],
                          axis=2).reshape(MLA_KV_LORA, -1)
    eye = jnp.eye(MLA_ROPE, dtype=F32)
    ek_h = jnp.concatenate([jnp.zeros((MLA_ROPE, MLA_NOPE), F32), eye,
                            jnp.zeros((MLA_ROPE, MLA_QK_PAD - MLA_NOPE - MLA_ROPE), F32)], axis=1)
    ek = jnp.tile(ek_h, (1, MLA_HEADS))
    wv2 = wkvu[:, :, MLA_NOPE:].reshape(MLA_KV_LORA, -1)

    w_out = p["w_out"][l]
    row = lambda a: a.reshape(1, -1)
    col = lambda a: a.reshape(-1, 1)
    return {
        "norm_mix_pre": row(p["norm_mix_pre"][l]), "norm_mix_post": row(p["norm_mix_post"][l]),
        "norm_ffn_pre": row(p["norm_ffn_pre"][l]), "norm_ffn_post": row(p["norm_ffn_post"][l]),
        "wa": wa.astype(BF16),
        "wdk": wk.astype(BF16),
        "wm": wm.astype(BF16),
        "wqt": wq.T.astype(BF16),
        "wvt": wv.T.astype(BF16),
        "qnw": row(p["mla_q_norm"][l]), "kvnw": row(p["mla_kv_norm"][l]),
        "wqut": wqu_plain.T.astype(BF16),
        "wk2": wk2.astype(BF16), "ek": ek.astype(BF16), "wvt2": wv2.T.astype(BF16),
        "conv_w": p["ssd_conv_w"][l], "conv_b": row(p["ssd_conv_b"][l]),
        "alog": col(p["ssd_a_log"][l]), "dtb": col(p["ssd_dt_bias"][l]),
        "dsk": row(p["ssd_d"][l]), "ssd_norm": row(p["ssd_norm"][l]),
        "lq1": row(p["diff_lam_q1"][l]), "lk1": row(p["diff_lam_k1"][l]),
        "lq2": row(p["diff_lam_q2"][l]), "lk2": row(p["diff_lam_k2"][l]),
        "subw": p["diff_subln"][l].reshape(-1, 1),
        "wo_s": w_out[:SSD_INNER].astype(BF16),
        "wo_d": w_out[SSD_INNER:SSD_INNER + DIFF_HEADS * DIFF_V].astype(BF16),
        "wo_a": w_out[SSD_INNER + DIFF_HEADS * DIFF_V:].astype(BF16),
        "rwt": p["router_w"][l].T, "rb": p["router_b"][l].reshape(-1, 1),
        "w_gate": p["w_gate"], "w_up": p["w_up"], "w_down": p["w_down"],
    }


def kernel(x, c, ctx, c_ctx, ada_w, ada_b, norm_mix_pre, norm_mix_post, norm_ffn_pre, norm_ffn_post, w_in, ssd_conv_w, ssd_conv_b, ssd_a_log, ssd_dt_bias, ssd_d, ssd_norm, diff_lam_q1, diff_lam_k1, diff_lam_q2, diff_lam_k2, diff_subln, mla_q_norm, mla_w_q_up, mla_kv_norm, mla_w_kv_up, w_out, router_w, router_b, w_gate, w_up, w_down):
    p = dict(norm_mix_pre=norm_mix_pre, norm_mix_post=norm_mix_post, norm_ffn_pre=norm_ffn_pre,
             norm_ffn_post=norm_ffn_post, w_in=w_in, ssd_conv_w=ssd_conv_w, ssd_conv_b=ssd_conv_b,
             ssd_a_log=ssd_a_log, ssd_dt_bias=ssd_dt_bias, ssd_d=ssd_d, ssd_norm=ssd_norm,
             diff_lam_q1=diff_lam_q1, diff_lam_k1=diff_lam_k1, diff_lam_q2=diff_lam_q2, diff_lam_k2=diff_lam_k2,
             diff_subln=diff_subln, mla_q_norm=mla_q_norm, mla_w_q_up=mla_w_q_up, mla_kv_norm=mla_kv_norm,
             mla_w_kv_up=mla_w_kv_up, w_out=w_out, router_w=router_w, router_b=router_b,
             w_gate=w_gate, w_up=w_up, w_down=w_down)
    b, seq, d = x.shape
    nctx = ctx.shape[1]
    depth = ada_w.shape[0]
    assert nctx == TILE and seq % TILE == 0 and seq % GRID_W == 0
    nt = nctx + seq
    caps = (EC_CAPACITY * nctx // N_EXPERTS, EC_CAPACITY * seq // N_EXPERTS)
    assert caps[0] % 16 == 0 and caps[1] % 16 == 0 and caps[0] + caps[1] >= WIN

    cvec = jnp.concatenate([c, c_ctx[None, :], jnp.zeros((8 - b - 1, d), F32)], axis=0)
    mods = _adaln(cvec, ada_w, ada_b).reshape(depth, 8, N_MOD, d)
    tabs = _rope_tables(seq, nctx)
    t = (x, ctx)
    for l in range(depth):
        lw = _layer_weights(l, p)
        lambda_init = 0.8 - 0.6 * math.exp(-0.3 * l)
        mod = jnp.stack([jnp.broadcast_to(mods[l, b], (b, N_MOD, d)), mods[l, :b]], axis=1)
        z, xbc, dt, dq, dk, dv, mq, mk, mv = _inproj(t, mod, lw, tabs)
        s = _ssd(z, xbc, dt, lw)
        da = _diff_attn(dq, dk, dv, lw, lambda_init).reshape(b, DIFF_HEADS * DIFF_V, nt)
        aa = _mla_attn(mq, mk, mv).reshape(b, MLA_HEADS * MLA_V, nt)
        t, hf, aff = _outproj(t, s, da, aa, mod, lw)
        pos, gate, cum = _route(aff, caps)
        cum_flat = cum.reshape(-1)
        xg, gc = _gather(cum_flat, hf, pos, gate, caps[0] + caps[1])
        y = _experts(xg, gc, lw, l)
        t = _combine(cum_flat, t, y, pos, mod, lw, latent_only=(l == depth - 1))
    return t
```

```python
import functools
import math

import jax
import jax.numpy as jnp
from jax import lax
from jax.experimental import pallas as pl
from jax.experimental.pallas import tpu as pltpu

F32 = jnp.float32
BF16 = jnp.bfloat16
I32 = jnp.int32
HIGHEST = lax.Precision.HIGHEST

EPS = 1e-6
GRID_W = 64
ROPE_BASE = 10000.0
N_MOD = 6

SSD_HEADS = 8
SSD_HEAD_DIM = 64
SSD_INNER = SSD_HEADS * SSD_HEAD_DIM
SSD_GROUPS = 2
SSD_STATE = 64
SSD_CHUNK = 128
SSD_BC = SSD_GROUPS * SSD_STATE
SSD_CONV_DIM = SSD_INNER + 2 * SSD_BC
SSD_PAIRS = SSD_HEADS // 2

DIFF_HEADS = 4
DIFF_QK = 32
DIFF_V = 64
DIFF_MAPS = 2 * DIFF_HEADS

MLA_HEADS = 4
MLA_Q_LORA = 192
MLA_KV_LORA = 128
MLA_NOPE = 64
MLA_ROPE = 32
MLA_V = 64
MLA_QK_PAD = 128

N_EXPERTS = 16
EC_CAPACITY = 2

TILE = 256
LANES = 128
MAPS_PER_TILE = LANES // DIFF_QK
VMEM_LIMIT = 56 * 1024 * 1024

NT_DIMS = (((1,), (1,)), ((), ()))
TN_DIMS = (((0,), (0,)), ((), ()))


def _cparams(sem, vmem=None):
    return pltpu.CompilerParams(dimension_semantics=sem, vmem_limit_bytes=vmem)


def _rms(x, w):
    return x * lax.rsqrt(jnp.mean(x * x, axis=-1, keepdims=True) + EPS) * w


def _silu(x):
    return x * jax.nn.sigmoid(x)


def _dot(a, b):
    return jnp.dot(a, b, preferred_element_type=F32)


def _adaln_kernel(c_ref, w_ref, b_ref, o_ref):
    s = _silu(c_ref[...])
    w = w_ref[0]
    rows = s.shape[0]
    s_hi = s.astype(BF16).astype(F32)
    w_hi = w.astype(BF16)
    w_lo = (w - w_hi.astype(F32)).astype(BF16)
    both = _dot(jnp.concatenate([s_hi, s - s_hi], axis=0).astype(BF16), w_hi)
    o_ref[0] = both[:rows] + both[rows:] + _dot(s_hi.astype(BF16), w_lo) + b_ref[0]


def _adaln(cvec, ada_w, ada_b):
    depth, d, nd = ada_w.shape
    rows = cvec.shape[0]
    return pl.pallas_call(
        _adaln_kernel,
        grid=(depth, nd // d),
        in_specs=[pl.BlockSpec((rows, d), lambda l, j: (0, 0)),
                  pl.BlockSpec((1, d, d), lambda l, j: (l, 0, j)),
                  pl.BlockSpec((1, 1, d), lambda l, j: (l, 0, j))],
        out_specs=pl.BlockSpec((1, rows, d), lambda l, j: (l, 0, j)),
        out_shape=jax.ShapeDtypeStruct((depth, rows, nd), F32),
        compiler_params=_cparams(("arbitrary", "arbitrary")),
        name="adaln",
    )(cvec, ada_w, ada_b.reshape(depth, 1, nd))


def _token_stream(t, nb=1):
    if isinstance(t, tuple):
        x, ctx = t
        b, seq, d = x.shape
        specs = [pl.BlockSpec((nb, TILE, d), lambda bi, i: (bi, jnp.maximum(i - 1, 0), 0)),
                 pl.BlockSpec((nb, TILE, d), lambda bi, i: (bi, 0, 0))]
        return (x, ctx), specs, (b, seq + ctx.shape[1], d)
    b, nt, d = t.shape
    specs = [pl.BlockSpec((nb, TILE, d), lambda bi, i: (bi, i, 0)),
             pl.BlockSpec((nb, TILE, d), lambda bi, i: (bi, 0, 0))]
    return (t, t), specs, (b, nt, d)


def _partner_rows(x):
    parts = []
    for g in range(0, x.shape[0], 16):
        parts += [x[g + 8:g + 16], x[g:g + 8]]
    return jnp.concatenate(parts, axis=0)


def _partner_lanes(x):
    width = x.shape[1]
    lane = lax.broadcasted_iota(I32, x.shape, 1)
    return jnp.where((lane & 8) == 0, pltpu.roll(x, width - 8, 1), pltpu.roll(x, 8, 1))


def _inproj_kernel(x_ref, c_ref, mod_ref, nw_ref, wa_ref, wdk_ref, wm_ref, wqt_ref, wvt_ref,
                   ck_ref, sk_ref, ct_ref, st_ref, qnw_ref, kvnw_ref, wqut_ref, wk2_ref, ek_ref, wvt2_ref,
                   z_ref, xbc_ref, dt_ref, dq_ref, dk_ref, dv_ref, mq_ref, mk_ref, mv_ref):
    nb = x_ref.shape[0]
    first = pl.program_id(1) == 0
    hs = []
    for bb in range(nb):
        mod = mod_ref[bb, 0]
        x = jnp.where(first, c_ref[bb], x_ref[bb])
        hs.append((_rms(x, nw_ref[...]) * (1.0 + mod[1:2]) + mod[0:1]).astype(BF16))
    h = jnp.concatenate(hs, axis=0)
    tile = lambda bb: slice(TILE * bb, TILE * (bb + 1))
    tile_rows = lambda a: jnp.concatenate([a] * nb, axis=0)
    tile_lanes = lambda a: jnp.concatenate([a] * nb, axis=1)
    ck = tile_rows(ck_ref[...])
    sk = tile_rows(sk_ref[...])
    ct = tile_lanes(ct_ref[...])
    st = tile_lanes(st_ref[...])

    ra = _dot(h, wa_ref[...])
    for bb in range(nb):
        z_ref[bb] = ra[tile(bb), :SSD_INNER].astype(BF16)
        xbc_ref[bb] = ra[tile(bb), SSD_INNER:SSD_INNER + SSD_CONV_DIM]
        dt_ref[bb] = ra[tile(bb), SSD_INNER + SSD_CONV_DIM:]

    rk = _dot(h, wdk_ref[...])
    k = (rk * ck + _partner_lanes(rk) * sk).astype(BF16)
    for bb in range(nb):
        for g in range(DIFF_MAPS // MAPS_PER_TILE):
            dk_ref[bb, g, 0] = k[tile(bb), LANES * g:LANES * (g + 1)]

    rq = lax.dot_general(wqt_ref[...], h, NT_DIMS, preferred_element_type=F32)
    rq_partner = _partner_rows(rq)
    for m in range(DIFF_MAPS):
        lo = DIFF_QK * m
        qm = ((rq[lo:lo + DIFF_QK] * ct + rq_partner[lo:lo + DIFF_QK] * st) * DIFF_C_EXP).astype(BF16)
        above = DIFF_QK * (m % MAPS_PER_TILE)
        below = LANES - above - DIFF_QK
        parts = ([jnp.zeros((above, qm.shape[1]), BF16)] if above else []) + [qm]
        parts += [jnp.zeros((below, qm.shape[1]), BF16)] if below else []
        padded = jnp.concatenate(parts, axis=0)
        for bb in range(nb):
            dq_ref[bb, m] = padded[:, tile(bb)]

    rv = lax.dot_general(wvt_ref[...], h, NT_DIMS, preferred_element_type=F32).astype(BF16)
    for bb in range(nb):
        for hd in range(DIFF_HEADS):
            dv_ref[bb, hd, 0] = rv[DIFF_V * hd:DIFF_V * (hd + 1), tile(bb)]

    rm = _dot(h, wm_ref[...])
    cq = _rms(rm[:, :MLA_Q_LORA], qnw_ref[...]).astype(BF16)
    ckv = _rms(rm[:, 256:256 + MLA_KV_LORA], kvnw_ref[...]).astype(BF16)
    kr = rm[:, 384:384 + MLA_ROPE] * ck[:, :MLA_ROPE] + rm[:, 416:416 + MLA_ROPE] * sk[:, :MLA_ROPE]

    rq2 = lax.dot_general(wqut_ref[...], cq, NT_DIMS, preferred_element_type=F32)
    rq2_partner = _partner_rows(rq2)
    ones = jnp.ones((MLA_NOPE, ct.shape[1]), F32)
    pad1 = jnp.ones((MLA_QK_PAD - MLA_NOPE - MLA_ROPE, ct.shape[1]), F32)
    ct_h = jnp.concatenate([ones, ct, pad1], axis=0)
    st_h = jnp.concatenate([0.0 * ones, st, 0.0 * pad1], axis=0)
    for hd in range(MLA_HEADS):
        lo = MLA_QK_PAD * hd
        qh = ((rq2[lo:lo + MLA_QK_PAD] * ct_h + rq2_partner[lo:lo + MLA_QK_PAD] * st_h) * MLA_C_EXP).astype(BF16)
        for bb in range(nb):
            mq_ref[bb, hd] = qh[:, tile(bb)]

    k2 = (_dot(ckv, wk2_ref[...]) + _dot(kr.astype(BF16), ek_ref[...])).astype(BF16)
    rv2 = lax.dot_general(wvt2_ref[...], ckv, NT_DIMS, preferred_element_type=F32).astype(BF16)
    for bb in range(nb):
        for hd in range(MLA_HEADS):
            mk_ref[bb, hd, 0] = k2[tile(bb), MLA_QK_PAD * hd:MLA_QK_PAD * (hd + 1)]
            mv_ref[bb, hd, 0] = rv2[MLA_V * hd:MLA_V * (hd + 1), tile(bb)]


def _inproj(t, mod, lw, tabs):
    nb = 2 if mod.shape[0] % 2 == 0 else 1
    streams, stream_specs, (b, nt, d) = _token_stream(t, nb)
    nti = nt // TILE
    full = lambda a: pl.BlockSpec(a.shape, lambda bi, i: (0,) * a.ndim)
    tok = lambda w: pl.BlockSpec((TILE, w), lambda bi, i: (i, 0))
    tokt = lambda w: pl.BlockSpec((w, TILE), lambda bi, i: (0, i))
    ws = [lw["norm_mix_pre"], lw["wa"], lw["wdk"], lw["wm"], lw["wqt"], lw["wvt"]]
    ws2 = [lw["qnw"], lw["kvnw"], lw["wqut"], lw["wk2"], lw["ek"], lw["wvt2"]]
    out_shape = [
        jax.ShapeDtypeStruct((b, nt, SSD_INNER), BF16),
        jax.ShapeDtypeStruct((b, nt, SSD_CONV_DIM), F32),
        jax.ShapeDtypeStruct((b, nt, LANES), F32),
        jax.ShapeDtypeStruct((b, DIFF_MAPS, LANES, nt), BF16),
        jax.ShapeDtypeStruct((b, DIFF_MAPS // MAPS_PER_TILE, nti, TILE, LANES), BF16),
        jax.ShapeDtypeStruct((b, DIFF_HEADS, nti, DIFF_V, TILE), BF16),
        jax.ShapeDtypeStruct((b, MLA_HEADS, MLA_QK_PAD, nt), BF16),
        jax.ShapeDtypeStruct((b, MLA_HEADS, nti, TILE, MLA_QK_PAD), BF16),
        jax.ShapeDtypeStruct((b, MLA_HEADS, nti, MLA_V, TILE), BF16),
    ]
    out_specs = [
        pl.BlockSpec((nb, TILE, SSD_INNER), lambda bi, i: (bi, i, 0)),
        pl.BlockSpec((nb, TILE, SSD_CONV_DIM), lambda bi, i: (bi, i, 0)),
        pl.BlockSpec((nb, TILE, LANES), lambda bi, i: (bi, i, 0)),
        pl.BlockSpec((nb, DIFF_MAPS, LANES, TILE), lambda bi, i: (bi, 0, 0, i)),
        pl.BlockSpec((nb, DIFF_MAPS // MAPS_PER_TILE, 1, TILE, LANES), lambda bi, i: (bi, 0, i, 0, 0)),
        pl.BlockSpec((nb, DIFF_HEADS, 1, DIFF_V, TILE), lambda bi, i: (bi, 0, i, 0, 0)),
        pl.BlockSpec((nb, MLA_HEADS, MLA_QK_PAD, TILE), lambda bi, i: (bi, 0, 0, i)),
        pl.BlockSpec((nb, MLA_HEADS, 1, TILE, MLA_QK_PAD), lambda bi, i: (bi, 0, i, 0, 0)),
        pl.BlockSpec((nb, MLA_HEADS, 1, MLA_V, TILE), lambda bi, i: (bi, 0, i, 0, 0)),
    ]
    in_specs = (stream_specs
                + [pl.BlockSpec((nb, 1, N_MOD, d), lambda bi, i: (bi, jnp.minimum(i, 1), 0, 0))]
                + [full(a) for a in ws]
                + [tok(DIFF_MAPS * DIFF_QK), tok(DIFF_MAPS * DIFF_QK), tokt(DIFF_QK), tokt(DIFF_QK)]
                + [full(a) for a in ws2])
    return pl.pallas_call(
        _inproj_kernel,
        grid=(b // nb, nti),
        in_specs=in_specs,
        out_specs=out_specs,
        out_shape=out_shape,
        compiler_params=_cparams(("arbitrary", "arbitrary"), VMEM_LIMIT),
        name="inproj",
    )(*streams, mod, *ws, tabs["ck"], tabs["sk"], tabs["ct"], tabs["st"], *ws2)


def _ssd_chunk_of(ph, i, nck, nctx):
    back = jnp.where(i < nctx, nctx - 1 - i, nck - 1 + nctx - i)
    return jnp.where(ph == 0, back, i)


def _ssd_kernel(xc_ref, xp_ref, xn_ref, dt_ref, z_ref, cw_ref, cb_ref, alog_ref, dtb_ref, dsk_ref, nw_ref,
                o_ref, s_ref, sb_ref, *cache, nck, nctx, nb):
    ph = pl.program_id(1)
    i = pl.program_id(2)
    c = _ssd_chunk_of(ph, i, nck, nctx)

    @pl.when(i == 0)
    def _():
        s_ref[...] = jnp.zeros_like(s_ref)

    one = lambda ref, bb: ref.at[pl.ds(bb, 1)]
    fns = [_ssd_sample(one(xc_ref, bb), one(xp_ref, bb), one(xn_ref, bb), one(dt_ref, bb), one(z_ref, bb),
                       cw_ref, cb_ref, alog_ref, dtb_ref, dsk_ref, nw_ref, one(o_ref, bb),
                       s_ref.at[bb], sb_ref.at[bb], [r.at[bb] for r in cache], c, nck=nck, nctx=nctx)
           for bb in range(nb)]

    @pl.when(ph == 0)
    def _():
        for backward_states, _ in fns:
            backward_states()

    @pl.when(ph == 1)
    def _():
        for _, forward_and_output in fns:
            forward_and_output()


def _ssd_sample(xc_ref, xp_ref, xn_ref, dt_ref, z_ref, cw_ref, cb_ref, alog_ref, dtb_ref, dsk_ref, nw_ref,
                o_ref, s_ref, sb_ref, cache, c, *, nck, nctx):
    q = SSD_CHUNK
    nh2 = 2 * SSD_HEADS
    cx_ref, cbc_ref, crow_ref, ccol_ref, cbt_ref = cache
    ri = lax.broadcasted_iota(I32, (q, q), 0)
    ci = lax.broadcasted_iota(I32, (q, q), 1)
    lower = ci <= ri
    upper = ci >= ri
    lane = ci
    first_half_s = ri < SSD_STATE
    first_half_l = lane < SSD_HEAD_DIM
    blockdiag = first_half_s == first_half_l

    def chunk_values():
        x = xc_ref[0]
        has_prev = jnp.logical_and(c != 0, c != nctx)
        has_next = jnp.logical_and(c != nctx - 1, c != nck - 1)
        prev_row = jnp.where(has_prev, xp_ref[0][7:8, :], 0.0)
        next_row = jnp.where(has_next, xn_ref[0][0:1, :], 0.0)
        row = lax.broadcasted_iota(I32, x.shape, 0)
        xm1 = jnp.where(row == 0, prev_row, pltpu.roll(x, 1, 0))
        xp1 = jnp.where(row == q - 1, next_row, pltpu.roll(x, q - 1, 0))
        cw = cw_ref[...]
        u = _silu(xm1 * cw[0:1] + x * cw[1:2] + xp1 * cw[2:3] + cb_ref[...])
        xs = u[:, :SSD_INNER].astype(BF16)
        bc = u[:, SSD_INNER:]
        xdt = dt_ref[0].T[:nh2] + dtb_ref[...]
        dtt = jnp.maximum(xdt, 0.0) + jnp.log1p(jnp.exp(-jnp.abs(xdt)))
        dat = dtt * (-jnp.exp(alog_ref[...]) * math.log2(math.e))
        tri_dims = (((1,), (0,)), ((), ()))
        acf = lax.dot_general(dat, upper.astype(F32), tri_dims, precision=HIGHEST, preferred_element_type=F32)
        acb = lax.dot_general(dat, lower.astype(F32), tri_dims, precision=HIGHEST, preferred_element_type=F32)
        act = jnp.where(ri[:nh2] < SSD_HEADS, acf, acb)
        rows = jnp.concatenate([dtt, act], axis=0)
        cols = jnp.concatenate([rows, jnp.zeros((q - 2 * nh2, q), F32)], axis=0).T
        bt = bc[:, :SSD_BC].T
        return xs, bc, rows, cols, bt

    def bodies(xs, bc, rows, cols, bt):
        bm = bc[:, :SSD_BC]
        cm = bc[:, SSD_BC:]
        dtt = rows[:nh2]
        act = rows[nh2:]
        acc = pltpu.roll(cols, LANES - nh2, 1)
        fwd_rows = lax.broadcasted_iota(I32, (nh2, 1), 0) < SSD_HEADS
        alast = jnp.where(fwd_rows, act[:, q - 1:q], act[:, 0:1])
        w_rows = jnp.exp2(alast - act) * dtt
        decay_all = jnp.exp2(alast)

        def pair_cols(arr_c, h0):
            return jnp.where(first_half_l, arr_c[:, h0:h0 + 1], arr_c[:, h0 + 1:h0 + 2])

        def pair_rows(arr_t, h0):
            return jnp.where(first_half_s[:, 0:arr_t.shape[1]], arr_t[h0:h0 + 1, :], arr_t[h0 + 1:h0 + 2, :])

        def state_update(p, d):
            g = (2 * p) // (SSD_HEADS // SSD_GROUPS)
            h0 = d * SSD_HEADS + 2 * p
            btg = bt[SSD_STATE * g:SSD_STATE * (g + 1)]
            lhs = (jnp.concatenate([btg, btg], axis=0) * pair_rows(w_rows, h0)).astype(BF16)
            xs2 = xs[:, 2 * SSD_HEAD_DIM * p:2 * SSD_HEAD_DIM * (p + 1)]
            upd = jnp.where(blockdiag, _dot(lhs, xs2), 0.0)
            return pair_rows(decay_all, h0) * s_ref[d, p] + upd

        def backward():
            for p in range(SSD_PAIRS):
                sb_ref[c, p] = s_ref[1, p].astype(BF16)
                s_ref[1, p] = state_update(p, 1)

        def forward():
            roll_c = pltpu.roll(cm, SSD_STATE, 1)
            dsk = dsk_ref[...]
            ys = []
            for p in range(SSD_PAIRS):
                g = (2 * p) // (SSD_HEADS // SSD_GROUPS)
                cg_only = jnp.where((lane < SSD_STATE) == (g == 0), cm, 0.0).astype(BF16)
                cb = lax.dot_general(cg_only, bm.astype(BF16), NT_DIMS, preferred_element_type=F32)
                ms = []
                for hh in range(2):
                    hf = 2 * p + hh
                    hb = SSD_HEADS + hf
                    lf = (jnp.exp2(jnp.where(lower, acc[:, hf:hf + 1] - act[hf:hf + 1, :], -jnp.inf))
                          * dtt[hf:hf + 1, :])
                    lb = (jnp.exp2(jnp.where(upper, acc[:, hb:hb + 1] - act[hb:hb + 1, :], -jnp.inf))
                          * dtt[hb:hb + 1, :])
                    ms.append((cb * (lf + lb) + jnp.where(ri == ci, dsk[:, hf:hf + 1], 0.0)).astype(BF16))
                xs2 = xs[:, 2 * SSD_HEAD_DIM * p:2 * SSD_HEAD_DIM * (p + 1)]
                zero = jnp.zeros_like(xs2)
                rhs = jnp.concatenate([jnp.where(first_half_l, xs2, zero), jnp.where(first_half_l, zero, xs2)],
                                      axis=0)
                y = _dot(jnp.concatenate(ms, axis=1), rhs)
                cdup = jnp.where(first_half_l == (g == 0), cm, roll_c)
                ef = jnp.exp2(pair_cols(acc, 2 * p))
                eb = jnp.exp2(pair_cols(acc, SSD_HEADS + 2 * p))
                lhs_off = jnp.concatenate([cdup * ef, cdup * eb], axis=1).astype(BF16)
                rhs_off = jnp.concatenate([s_ref[0, p].astype(BF16), sb_ref[c, p]], axis=0)
                ys.append(y + _dot(lhs_off, rhs_off))
                s_ref[0, p] = state_update(p, 0)
            y = jnp.concatenate(ys, axis=1)
            zf = z_ref[0].astype(F32)
            gt = y * _silu(zf)
            nw = nw_ref[...]
            gw = SSD_INNER // SSD_GROUPS
            outs = [_rms(gt[:, gw * g:gw * (g + 1)], nw[:, gw * g:gw * (g + 1)]) for g in range(SSD_GROUPS)]
            o_ref[0] = jnp.concatenate(outs, axis=1).astype(BF16)

        return backward, forward

    def backward_states():
        xs, bc, rows, cols, bt = chunk_values()
        cx_ref[c] = xs
        cbc_ref[c] = bc
        crow_ref[c] = rows
        ccol_ref[c] = cols
        cbt_ref[c] = bt
        bodies(xs, bc, rows, cols, bt)[0]()

    def forward_and_output():
        bodies(cx_ref[c], cbc_ref[c], crow_ref[c], ccol_ref[c], cbt_ref[c])[1]()

    return backward_states, forward_and_output


def _ssd(z, xbc, dt, lw):
    b, nt, _ = z.shape
    q = SSD_CHUNK
    nck = nt // q
    nctx = TILE // q
    rows8 = q // 8
    nb = 1
    chunk = functools.partial(_ssd_chunk_of, nck=nck, nctx=nctx)
    full = lambda a: pl.BlockSpec(a.shape, lambda bi, ph, i: (0,) * a.ndim)
    ws = [lw["conv_w"], lw["conv_b"], lw["alog"], lw["dtb"], lw["dsk"], lw["ssd_norm"]]
    kern = functools.partial(_ssd_kernel, nck=nck, nctx=nctx, nb=nb)
    return pl.pallas_call(
        kern,
        grid=(b // nb, 2, nck),
        in_specs=[pl.BlockSpec((nb, q, SSD_CONV_DIM), lambda bi, ph, i: (bi, chunk(ph, i), 0)),
                  pl.BlockSpec((nb, 8, SSD_CONV_DIM),
                               lambda bi, ph, i: (bi, jnp.maximum(chunk(ph, i) * rows8 - 1, 0), 0)),
                  pl.BlockSpec((nb, 8, SSD_CONV_DIM),
                               lambda bi, ph, i: (bi, jnp.minimum((chunk(ph, i) + 1) * rows8, nck * rows8 - 1), 0)),
                  pl.BlockSpec((nb, q, LANES), lambda bi, ph, i: (bi, chunk(ph, i), 0)),
                  pl.BlockSpec((nb, q, SSD_INNER), lambda bi, ph, i: (bi, chunk(ph, i), 0))]
                 + [full(a) for a in ws],
        out_specs=pl.BlockSpec((nb, q, SSD_INNER), lambda bi, ph, i: (bi, jnp.where(ph == 0, 0, i), 0)),
        out_shape=jax.ShapeDtypeStruct((b, nt, SSD_INNER), BF16),
        scratch_shapes=[pltpu.VMEM((nb, 2, SSD_PAIRS, 2 * SSD_STATE, 2 * SSD_HEAD_DIM), F32),
                        pltpu.VMEM((nb, nck, SSD_PAIRS, 2 * SSD_STATE, 2 * SSD_HEAD_DIM), BF16),
                        pltpu.VMEM((nb, nck, q, SSD_INNER), BF16),
                        pltpu.VMEM((nb, nck, q, 2 * SSD_BC), F32),
                        pltpu.VMEM((nb, nck, 4 * SSD_HEADS, q), F32),
                        pltpu.VMEM((nb, nck, q, LANES), F32),
                        pltpu.VMEM((nb, nck, SSD_BC, q), F32)],
        compiler_params=_cparams(("arbitrary", "arbitrary", "arbitrary"), VMEM_LIMIT),
        name="ssd",
    )(xbc, xbc, xbc, dt, z, *ws)


ACC_ROWS = 80
DIFF_C_EXP = (DIFF_QK ** -0.5) * math.log2(math.e)
MLA_C_EXP = ((MLA_NOPE + MLA_ROPE) ** -0.5) * math.log2(math.e)


def _ones_rows(tk):
    return (lax.broadcasted_iota(I32, (ACC_ROWS - DIFF_V, tk), 0) == 0).astype(BF16)


def _score_step(kqs, s_ref):
    for idx, (k, q) in enumerate(kqs):
        s_ref[idx, 0:k.shape[0], :] = _dot(k, q)


def _softmax_pv_step(n_keys, vaugs, s_ref, m_ref, acc_ref):
    for idx in range(len(vaugs)):
        s = s_ref[idx, 0:n_keys, :]
        m = m_ref[idx]
        mn = jnp.maximum(m, jnp.max(s, axis=0, keepdims=True))
        p = jnp.exp2(s - mn).astype(BF16)
        acc_ref[idx] = acc_ref[idx] * jnp.exp2(m - mn) + _dot(vaugs[idx], p)
        m_ref[idx] = mn


def _plain_pv_step(n_keys, vaugs, s_ref, acc_ref):
    for idx in range(len(vaugs)):
        p = jnp.exp2(s_ref[idx, 0:n_keys, :]).astype(BF16)
        acc_ref[idx] += _dot(vaugs[idx], p)


def _plain_keys(kq_fn, v_fn, acc_ref, nti, group):
    def run(steps):
        work = [(c0, n, idx) for c0, n in steps for idx in range(acc_ref.shape[0])]
        kqs = {}
        vaugs = {}

        def score(item):
            c0, n, idx = item
            if (c0, n) not in kqs:
                kqs[(c0, n)] = kq_fn(c0, n)
            k, q = kqs[(c0, n)][idx]
            return _dot(k, q)

        nxt = score(work[0])
        for pos, (c0, n, idx) in enumerate(work):
            s = nxt
            if pos + 1 < len(work):
                nxt = score(work[pos + 1])
            if (c0, n) not in vaugs:
                vaugs[(c0, n)] = v_fn(c0, n)
            acc_ref[idx] += _dot(vaugs[(c0, n)][idx], jnp.exp2(s).astype(BF16))

    latent = pl.program_id(1) > 0
    pl.when(jnp.logical_not(latent))(lambda: run([(0, 1)]))
    pl.when(latent)(lambda: run([(0, 1)] + [(1 + g * group, group) for g in range((nti - 1) // group)]))


def _attn_init(m_ref, acc_ref):
    m_ref[...] = jnp.full(m_ref.shape, -jnp.inf, F32)
    acc_ref[...] = jnp.zeros_like(acc_ref)


MAX_UNSHIFTED_SCORE = 96.0
MAX_UNSHIFTED_VALUE = 65536.0


def _key_abs_max(k_ref, v_ref, kmax_ref):
    @pl.when(pl.program_id(1) == 0)
    def _():
        nti = k_ref.shape[2]

        def slab_max(ref, t, best):
            for g in range(ref.shape[1]):
                a = jnp.abs(ref[0, g, t].astype(F32))
                best = jnp.maximum(best, jnp.max(a.reshape(-1, 8, a.shape[1]), axis=0))
            return best

        def body(t, best):
            return slab_max(k_ref, t, best[0]), slab_max(v_ref, t, best[1])

        zeros = lambda ref: jnp.zeros((8, ref.shape[4]), F32)
        kbest, vbest = lax.fori_loop(0, nti, body, (zeros(k_ref), zeros(v_ref)))
        kmax_ref[0] = jnp.max(kbest)
        kmax_ref[1] = jnp.max(vbest)


def _scores_are_bounded(q_ref, kmax_ref):
    q = jnp.abs(q_ref[0].astype(F32))
    return jnp.logical_and(kmax_ref[0] * jnp.max(jnp.sum(q, axis=1)) <= MAX_UNSHIFTED_SCORE,
                           kmax_ref[1] <= MAX_UNSHIFTED_VALUE)


DIFF_KEY_GROUP = 2
MLA_KEY_GROUP = 4


def _key_group(nti, want):
    n_lat = nti - 1
    assert n_lat % 2 == 0
    while n_lat % (2 * want):
        want //= 2
    return want


def _chunks_k(k_ref, lead, c0, n):
    return jnp.concatenate([k_ref[lead + (c0 + j,)] for j in range(n)], axis=0)


def _chunks_v(v_ref, lead, c0, n):
    v = jnp.concatenate([v_ref[lead + (c0 + j,)] for j in range(n)], axis=1)
    return jnp.concatenate([v, _ones_rows(n * TILE)], axis=0)


def _pipelined_keys(scores, consume, nti, group):
    steps = (nti - 1) // group
    first = lambda k: 1 + (k - 1) * group
    scores(0, 1, 0)
    latent = pl.program_id(1) > 0

    @pl.when(jnp.logical_not(latent))
    def _():
        consume(0, 1, 0)

    @pl.when(latent)
    def _():
        scores(first(1), group, 1)
        consume(0, 1, 0)
        scores(first(2), group, 0)
        consume(first(1), group, 1)

        def body(j, carry):
            k = 2 * j
            scores(first(k + 1), group, 1)
            consume(first(k), group, 0)
            scores(first(k + 2), group, 0)
            consume(first(k + 1), group, 1)
            return carry

        lax.fori_loop(1, steps // 2, body, 0)
        consume(first(steps), group, 0)


def _diff_attn_kernel(lq1_ref, lk1_ref, lq2_ref, lk2_ref, subw_ref, q_ref, k_ref, v_ref, o_ref,
                      m_ref, acc_ref, sa_ref, sb_ref, kmax_ref, *, nti, group, lambda_init):
    _attn_init(m_ref, acc_ref)
    _key_abs_max(k_ref, v_ref, kmax_ref)
    slots = (sa_ref, sb_ref)

    def kq_pairs(c0, n):
        ks = [_chunks_k(k_ref, (0, g), c0, n) for g in range(DIFF_MAPS // MAPS_PER_TILE)]
        return [(ks[m // MAPS_PER_TILE], q_ref[0, m]) for m in range(DIFF_MAPS)]

    def values(c0, n):
        vaugs = []
        for h in range(DIFF_HEADS):
            vaugs += [_chunks_v(v_ref, (0, h), c0, n)] * 2
        return vaugs

    def scores(c0, n, slot):
        _score_step(kq_pairs(c0, n), slots[slot])

    def consume(c0, n, slot):
        _softmax_pv_step(n * TILE, values(c0, n), slots[slot], m_ref, acc_ref)

    bounded = _scores_are_bounded(q_ref, kmax_ref)
    pl.when(bounded)(lambda: _plain_keys(kq_pairs, values, acc_ref, nti, 4))
    pl.when(jnp.logical_not(bounded))(lambda: _pipelined_keys(scores, consume, nti, group))
    lam =(jnp.exp(jnp.sum(lq1_ref[...] * lk1_ref[...], keepdims=True))
           - jnp.exp(jnp.sum(lq2_ref[...] * lk2_ref[...], keepdims=True)) + lambda_init)
    for h in range(DIFF_HEADS):
        a1 = acc_ref[2 * h]
        a2 = acc_ref[2 * h + 1]
        o = a1[:DIFF_V] / a1[DIFF_V:DIFF_V + 1] - lam * (a2[:DIFF_V] / a2[DIFF_V:DIFF_V + 1])
        o = o * lax.rsqrt(jnp.mean(o * o, axis=0, keepdims=True) + EPS) * subw_ref[...]
        o_ref[0, h] = (o * (1.0 - lambda_init)).astype(BF16)


def _diff_attn(dq, dk, dv, lw, lambda_init):
    b, nmaps, _, nt = dq.shape
    nh = nmaps // 2
    nti = nt // TILE
    group = _key_group(nti, DIFF_KEY_GROUP)
    kern = functools.partial(_diff_attn_kernel, nti=nti, group=group, lambda_init=lambda_init)
    score_slot = pltpu.VMEM((2 * nh, group * TILE, TILE), F32)
    vec = pl.BlockSpec((1, DIFF_QK), lambda bi, i: (0, 0))
    return pl.pallas_call(
        kern,
        grid=(b, nti),
        in_specs=[vec, vec, vec, vec,
                  pl.BlockSpec((DIFF_V, 1), lambda bi, i: (0, 0)),
                  pl.BlockSpec((1, nmaps, LANES, TILE), lambda bi, i: (bi, 0, 0, i)),
                  pl.BlockSpec((1, nmaps // MAPS_PER_TILE, nti, TILE, LANES), lambda bi, i: (bi, 0, 0, 0, 0)),
                  pl.BlockSpec((1, nh, nti, DIFF_V, TILE), lambda bi, i: (bi, 0, 0, 0, 0))],
        out_specs=pl.BlockSpec((1, nh, DIFF_V, TILE), lambda bi, i: (bi, 0, 0, i)),
        out_shape=jax.ShapeDtypeStruct((b, nh, DIFF_V, nt), BF16),
        scratch_shapes=[pltpu.VMEM((2 * nh, 1, TILE), F32), pltpu.VMEM((2 * nh, ACC_ROWS, TILE), F32),
                        score_slot, score_slot, pltpu.SMEM((2,), F32)],
        compiler_params=_cparams(("arbitrary", "arbitrary"), VMEM_LIMIT),
        name="diff_attn",
    )(lw["lq1"], lw["lk1"], lw["lq2"], lw["lk2"], lw["subw"], dq, dk, dv)


def _mla_attn_kernel(q_ref, k_ref, v_ref, o_ref, m_ref, acc_ref, sa_ref, sb_ref, kmax_ref, *, nti, group):
    _attn_init(m_ref, acc_ref)
    _key_abs_max(k_ref, v_ref, kmax_ref)
    slots = (sa_ref, sb_ref)

    def kq_pairs(c0, n):
        return [(_chunks_k(k_ref, (0, h), c0, n), q_ref[0, h]) for h in range(MLA_HEADS)]

    def values(c0, n):
        return [_chunks_v(v_ref, (0, h), c0, n) for h in range(MLA_HEADS)]

    def scores(c0, n, slot):
        _score_step(kq_pairs(c0, n), slots[slot])

    def consume(c0, n, slot):
        _softmax_pv_step(n * TILE, values(c0, n), slots[slot], m_ref, acc_ref)

    def consume_plain(c0, n, slot):
        _plain_pv_step(n * TILE, values(c0, n), slots[slot], acc_ref)

    bounded = _scores_are_bounded(q_ref, kmax_ref)
    pl.when(bounded)(lambda: _pipelined_keys(scores, consume_plain, nti, group))
    pl.when(jnp.logical_not(bounded))(lambda: _pipelined_keys(scores, consume, nti, group))
    for h in range(MLA_HEADS):
        a = acc_ref[h]
        o_ref[0, h] = (a[:MLA_V] / a[MLA_V:MLA_V + 1]).astype(BF16)


def _mla_attn(mq, mk, mv):
    b, nh, dpad, nt = mq.shape
    nti = nt // TILE
    group = _key_group(nti, MLA_KEY_GROUP)
    kern = functools.partial(_mla_attn_kernel, nti=nti, group=group)
    score_slot = pltpu.VMEM((nh, group * TILE, TILE), F32)
    return pl.pallas_call(
        kern,
        grid=(b, nti),
        in_specs=[pl.BlockSpec((1, nh, dpad, TILE), lambda bi, i: (bi, 0, 0, i)),
                  pl.BlockSpec((1, nh, nti, TILE, dpad), lambda bi, i: (bi, 0, 0, 0, 0)),
                  pl.BlockSpec((1, nh, nti, MLA_V, TILE), lambda bi, i: (bi, 0, 0, 0, 0))],
        out_specs=pl.BlockSpec((1, nh, MLA_V, TILE), lambda bi, i: (bi, 0, 0, i)),
        out_shape=jax.ShapeDtypeStruct((b, nh, MLA_V, nt), BF16),
        scratch_shapes=[pltpu.VMEM((nh, 1, TILE), F32), pltpu.VMEM((nh, ACC_ROWS, TILE), F32),
                        score_slot, score_slot, pltpu.SMEM((2,), F32)],
        compiler_params=_cparams(("arbitrary", "arbitrary"), VMEM_LIMIT),
        name="mla_attn",
    )(mq, mk, mv)


def _outproj_kernel(t_ref, c_ref, s_ref, d_ref, a_ref, mod_ref, npost_ref, nffn_ref, ws_ref, wd_ref, wa_ref,
                    rwt_ref, rb_ref, tn_ref, hf_ref, aff_ref):
    nb = s_ref.shape[0]
    first = pl.program_id(1) == 0
    rwt = rwt_ref[...]
    rw_hi = rwt.astype(BF16)
    rw_lo = (rwt - rw_hi.astype(F32)).astype(BF16)
    rw_both = jnp.concatenate([rw_hi, rw_lo], axis=0)
    m_all = (_dot(jnp.concatenate([s_ref[bb] for bb in range(nb)], axis=0), ws_ref[...])
             + lax.dot_general(jnp.concatenate([d_ref[bb] for bb in range(nb)], axis=1), wd_ref[...], TN_DIMS,
                               preferred_element_type=F32)
             + lax.dot_general(jnp.concatenate([a_ref[bb] for bb in range(nb)], axis=1), wa_ref[...], TN_DIMS,
                               preferred_element_type=F32))
    for bb in range(nb):
        mod = mod_ref[bb, 0]
        t_in = jnp.where(first, c_ref[bb], t_ref[bb])
        tn = t_in + mod[2:3] * _rms(m_all[TILE * bb:TILE * (bb + 1)], npost_ref[...])
        tn_ref[bb] = tn
        hf = _rms(tn, nffn_ref[...]) * (1.0 + mod[4:5]) + mod[3:4]
        hf_hi = hf.astype(BF16)
        hf_ref[bb] = hf_hi
        hf_lo = (hf - hf_hi.astype(F32)).astype(BF16)
        both = lax.dot_general(rw_both, hf_hi, NT_DIMS, preferred_element_type=F32)
        logits = (both[:N_EXPERTS] + both[N_EXPERTS:]
                  + lax.dot_general(rw_hi, hf_lo, NT_DIMS, preferred_element_type=F32) + rb_ref[...])
        e = jnp.exp(logits - jnp.max(logits, axis=0, keepdims=True))
        aff_ref[bb] = e / jnp.sum(e, axis=0, keepdims=True)


def _outproj(t, s, dt_, at_, mod, lw):
    nb = 2 if s.shape[0] % 2 == 0 else 1
    streams, stream_specs, (b, nt, d) = _token_stream(t, nb)
    nti = nt // TILE
    full = lambda a: pl.BlockSpec(a.shape, lambda bi, i: (0,) * a.ndim)
    ws = [lw["norm_mix_post"], lw["norm_ffn_pre"], lw["wo_s"], lw["wo_d"], lw["wo_a"], lw["rwt"], lw["rb"]]
    return pl.pallas_call(
        _outproj_kernel,
        grid=(b // nb, nti),
        in_specs=stream_specs + [
                  pl.BlockSpec((nb, TILE, SSD_INNER), lambda bi, i: (bi, i, 0)),
                  pl.BlockSpec((nb, DIFF_HEADS * DIFF_V, TILE), lambda bi, i: (bi, 0, i)),
                  pl.BlockSpec((nb, MLA_HEADS * MLA_V, TILE), lambda bi, i: (bi, 0, i)),
                  pl.BlockSpec((nb, 1, N_MOD, d), lambda bi, i: (bi, jnp.minimum(i, 1), 0, 0))]
                 + [full(a) for a in ws],
        out_specs=[pl.BlockSpec((nb, TILE, d), lambda bi, i: (bi, i, 0)),
                   pl.BlockSpec((nb, TILE, d), lambda bi, i: (bi, i, 0)),
                   pl.BlockSpec((nb, N_EXPERTS, TILE), lambda bi, i: (bi, 0, i))],
        out_shape=[jax.ShapeDtypeStruct((b, nt, d), F32),
                   jax.ShapeDtypeStruct((b, nt, d), BF16),
                   jax.ShapeDtypeStruct((b, N_EXPERTS, nt), F32)],
        compiler_params=_cparams(("arbitrary", "arbitrary"), VMEM_LIMIT),
        name="outproj",
    )(*streams, s, dt_, at_, mod, *ws)


def _route_kernel(aff_ref, pos_ref, gate_ref, cum_ref, *, nti, caps):
    ne = N_EXPERTS
    tri = (lax.broadcasted_iota(I32, (TILE, TILE), 0) < lax.broadcasted_iota(I32, (TILE, TILE), 1)).astype(BF16)
    lane = lax.broadcasted_iota(I32, (ne, LANES), 1)

    def excl_prefix(mask_f):
        return _dot(mask_f.astype(BF16), tri)

    cum_vec = jnp.zeros((ne, LANES), F32)
    total = jnp.zeros((ne, 1), F32)
    seg_bounds = ((0, 1, caps[0]), (1, nti, caps[1]))
    for t0, t1, cap in seg_bounds:
        xi = aff_ref[0, :, t0 * TILE:t1 * TILE]

        def bit_step(j, thr_bits, xi=xi, cap=cap):
            hi = 1 << (29 - 2 * j)
            lo = 1 << (28 - 2 * j)
            best = thr_bits
            for cand in (thr_bits | lo, thr_bits | hi, thr_bits | hi | lo):
                cnt = jnp.sum((xi >= pltpu.bitcast(cand, F32)).astype(F32), axis=1, keepdims=True)
                best = jnp.where(cnt >= cap, cand, best)
            return best

        thr = pltpu.bitcast(lax.fori_loop(0, 15, bit_step, jnp.zeros((ne, 1), I32)), F32)
        need = cap - jnp.sum((xi > thr).astype(F32), axis=1, keepdims=True)
        eq_seen = jnp.zeros((ne, 1), F32)
        for t in range(t0, t1):
            lo = (t - t0) * TILE
            xt = xi[:, lo:lo + TILE]
            eq = (xt == thr).astype(F32)
            eq_rank = eq_seen + excl_prefix(eq)
            sel = jnp.where(xt > thr, 1.0, eq * (eq_rank < need).astype(F32))
            eq_seen = eq_seen + jnp.sum(eq, axis=1, keepdims=True)
            rank = total + excl_prefix(sel)
            pos_ref[0, :, t * TILE:(t + 1) * TILE] = jnp.where(sel > 0.0, rank, -1.0).astype(I32)
            gate_ref[0, :, t * TILE:(t + 1) * TILE] = sel * aff_ref[0, :, t * TILE:(t + 1) * TILE]
            cum_vec = jnp.where(lane == t, total, cum_vec)
            total = total + jnp.sum(sel, axis=1, keepdims=True)
    cum_vec = jnp.where(lane == nti, total, cum_vec)
    cum_ref[0] = cum_vec.astype(I32)


def _route(aff, caps):
    b, ne, nt = aff.shape
    nti = nt // TILE
    kern = functools.partial(_route_kernel, nti=nti, caps=caps)
    return pl.pallas_call(
        kern,
        grid=(b,),
        in_specs=[pl.BlockSpec((1, ne, nt), lambda bi: (bi, 0, 0))],
        out_specs=[pl.BlockSpec((1, ne, nt), lambda bi: (bi, 0, 0)),
                   pl.BlockSpec((1, ne, nt), lambda bi: (bi, 0, 0)),
                   pl.BlockSpec((1, ne, LANES), lambda bi: (bi, 0, 0))],
        out_shape=[jax.ShapeDtypeStruct((b, ne, nt), I32),
                   jax.ShapeDtypeStruct((b, ne, nt), F32),
                   jax.ShapeDtypeStruct((b, ne, LANES), I32)],
        compiler_params=_cparams(("arbitrary",)),
        name="route",
    )(aff)


WIN = 64
GROUP = 4


def _tile_windows(cum_ref, b, t, rows):
    los = []
    rounds = jnp.int32(1)
    for e in range(N_EXPERTS):
        base = (b * N_EXPERTS + e) * LANES
        lo = (cum_ref[base + t] // 16) * 16
        los.append(lo)
        rounds = jnp.maximum(rounds, (cum_ref[base + t + 1] - lo + WIN - 1) // WIN)
    return los, rounds


def _window_onehot(pos_row, lo, r, rows):
    want = lo + WIN * r
    w0 = pl.multiple_of(jnp.minimum(want, rows - WIN), 16)
    rowid = w0 + lax.broadcasted_iota(I32, (WIN, TILE), 0)
    return w0, jnp.logical_and(pos_row == rowid, rowid >= want).astype(F32)


def _gather_kernel(cum_ref, hf_ref, pos_ref, gate_ref, xg_ref, gc_ref, *, rows):
    b = pl.program_id(0)
    t = pl.program_id(1)

    @pl.when(t == 0)
    def _():
        xg_ref[...] = jnp.zeros_like(xg_ref)
        gc_ref[...] = jnp.zeros_like(gc_ref)

    los, rounds = _tile_windows(cum_ref, b, t, rows)

    def round_step(r, carry):
        w0s, hots = [], []
        for e in range(N_EXPERTS):
            w0, hot = _window_onehot(pos_ref[0, e:e + 1, :], los[e], r, rows)
            w0s.append(w0)
            hots.append(hot)
            gc_ref[0, e, pl.ds(w0, WIN), :] += jnp.sum(hot * gate_ref[0, e:e + 1, :], axis=1, keepdims=True)
        res = _dot(jnp.concatenate(hots, axis=0).astype(BF16), hf_ref[0])
        for e in range(N_EXPERTS):
            xg_ref[0, e, pl.ds(w0s[e], WIN), :] += res[WIN * e:WIN * (e + 1)].astype(BF16)
        return carry

    lax.fori_loop(0, rounds, round_step, 0)


def _gather(cum_flat, hf, pos, gate, rows):
    b, nt, d = hf.shape
    nti = nt // TILE
    ne = pos.shape[1]
    kern = functools.partial(_gather_kernel, rows=rows)
    grid_spec = pltpu.PrefetchScalarGridSpec(
        num_scalar_prefetch=1,
        grid=(b, nti),
        in_specs=[pl.BlockSpec((1, TILE, d), lambda bi, i, cum: (bi, i, 0)),
                  pl.BlockSpec((1, ne, TILE), lambda bi, i, cum: (bi, 0, i)),
                  pl.BlockSpec((1, ne, TILE), lambda bi, i, cum: (bi, 0, i))],
        out_specs=[pl.BlockSpec((1, ne, rows, d), lambda bi, i, cum: (bi, 0, 0, 0)),
                   pl.BlockSpec((1, ne, rows, 1), lambda bi, i, cum: (bi, 0, 0, 0))],
    )
    return pl.pallas_call(
        kern,
        grid_spec=grid_spec,
        out_shape=[jax.ShapeDtypeStruct((b, ne, rows, d), BF16), jax.ShapeDtypeStruct((b, ne, rows, 1), F32)],
        compiler_params=_cparams(("arbitrary", "arbitrary"), VMEM_LIMIT),
        name="gather",
    )(cum_flat, hf, pos, gate)


def _experts_kernel(xg_ref, gc_ref, wg_ref, wu_ref, wd_ref, y_ref, wgb_ref, wub_ref, wdb_ref):
    @pl.when(pl.program_id(1) == 0)
    def _():
        wgb_ref[...] = wg_ref[0, 0].astype(BF16)
        wub_ref[...] = wu_ref[0, 0].astype(BF16)
        wdb_ref[...] = wd_ref[0, 0].astype(BF16)

    nb, _, rows, _ = xg_ref.shape
    half = rows // 2
    for bb in range(nb):
        for r in range(2):
            sl = slice(half * r, half * (r + 1))
            xg = xg_ref[bb, 0, sl, :]
            hid = (_silu(_dot(xg, wgb_ref[...])) * _dot(xg, wub_ref[...])).astype(BF16)
            y_ref[bb, 0, sl, :] = (_dot(hid, wdb_ref[...]) * gc_ref[bb, 0, sl, :]).astype(BF16)


def _experts(xg, gc, lw, l):
    b, ne, rows, d = xg.shape
    ff = lw["w_gate"].shape[3]
    nb = 2 if b % 2 == 0 else 1
    return pl.pallas_call(
        _experts_kernel,
        grid=(ne, b // nb),
        in_specs=[pl.BlockSpec((nb, 1, rows, d), lambda e, bi: (bi, e, 0, 0)),
                  pl.BlockSpec((nb, 1, rows, 1), lambda e, bi: (bi, e, 0, 0)),
                  pl.BlockSpec((1, 1, d, ff), lambda e, bi: (l, e, 0, 0)),
                  pl.BlockSpec((1, 1, d, ff), lambda e, bi: (l, e, 0, 0)),
                  pl.BlockSpec((1, 1, ff, d), lambda e, bi: (l, e, 0, 0))],
        out_specs=pl.BlockSpec((nb, 1, rows, d), lambda e, bi: (bi, e, 0, 0)),
        out_shape=jax.ShapeDtypeStruct((b, ne, rows, d), BF16),
        scratch_shapes=[pltpu.VMEM((d, ff), BF16), pltpu.VMEM((d, ff), BF16), pltpu.VMEM((ff, d), BF16)],
        compiler_params=_cparams(("arbitrary", "arbitrary"), VMEM_LIMIT),
        name="experts",
    )(xg, gc, lw["w_gate"], lw["w_up"], lw["w_down"])


def _combine_kernel(cum_ref, t_ref, y_ref, pos_ref, mod_ref, npost_ref, o_ref, f_ref, *, rows, latent_only):
    b = pl.program_id(0)
    t = pl.program_id(1)

    def run():
        los, rounds = _tile_windows(cum_ref, b, t, rows)

        def scatter_round(r):
            total = None
            for g in range(N_EXPERTS // GROUP):
                hots, wins = [], []
                for e in range(GROUP * g, GROUP * (g + 1)):
                    w0, hot = _window_onehot(pos_ref[0, e:e + 1, :], los[e], r, rows)
                    hots.append(hot)
                    wins.append(y_ref[0, e, pl.ds(w0, WIN), :])
                hot = jnp.concatenate(hots, axis=0).astype(BF16)
                part = lax.dot_general(hot, jnp.concatenate(wins, axis=0), TN_DIMS, preferred_element_type=F32)
                total = part if total is None else total + part
            return total

        f_ref[...] = scatter_round(0)

        def round_step(r, carry):
            f_ref[...] += scatter_round(r)
            return carry

        lax.fori_loop(1, rounds, round_step, 0)
        mod = mod_ref[0, 0]
        o_ref[0] = t_ref[0] + mod[5:6] * _rms(f_ref[...], npost_ref[...])

    if latent_only:
        pl.when(t > 0)(run)
    else:
        run()


def _combine(cum_flat, t, y, pos, mod, lw, latent_only):
    b, nt, d = t.shape
    nti = nt // TILE
    ne, rows = y.shape[1], y.shape[2]
    kern = functools.partial(_combine_kernel, rows=rows, latent_only=latent_only)
    if latent_only:
        out_rows, out_map = nt - TILE, lambda bi, i, cum: (bi, jnp.maximum(i - 1, 0), 0)
    else:
        out_rows, out_map = nt, lambda bi, i, cum: (bi, i, 0)
    grid_spec = pltpu.PrefetchScalarGridSpec(
        num_scalar_prefetch=1,
        grid=(b, nti),
        in_specs=[pl.BlockSpec((1, TILE, d), lambda bi, i, cum: (bi, i, 0)),
                  pl.BlockSpec((1, ne, rows, d), lambda bi, i, cum: (bi, 0, 0, 0)),
                  pl.BlockSpec((1, ne, TILE), lambda bi, i, cum: (bi, 0, i)),
                  pl.BlockSpec((1, 1, N_MOD, d), lambda bi, i, cum: (bi, jnp.minimum(i, 1), 0, 0)),
                  pl.BlockSpec((1, d), lambda bi, i, cum: (0, 0))],
        out_specs=pl.BlockSpec((1, TILE, d), out_map),
        scratch_shapes=[pltpu.VMEM((TILE, d), F32)],
    )
    return pl.pallas_call(
        kern,
        grid_spec=grid_spec,
        out_shape=jax.ShapeDtypeStruct((b, out_rows, d), F32),
        compiler_params=_cparams(("arbitrary", "arbitrary"), VMEM_LIMIT),
        name="combine",
    )(cum_flat, t, y, pos, mod, lw["norm_ffn_post"])


def _rope_tables(seq, ctx):
    quarter = MLA_ROPE // 4
    inv = ROPE_BASE ** (-jnp.arange(quarter, dtype=F32) / quarter)
    n_rows = seq // GRID_W
    rows = jnp.repeat(jnp.arange(n_rows, dtype=F32), GRID_W)
    cols = jnp.tile(jnp.arange(GRID_W, dtype=F32), n_rows)
    ar = rows[:, None] * inv[None, :]
    ac = cols[:, None] * inv[None, :]
    cos = jnp.concatenate([jnp.cos(ar), jnp.cos(ar), jnp.cos(ac), jnp.cos(ac)], axis=1)
    sin = jnp.concatenate([-jnp.sin(ar), jnp.sin(ar), -jnp.sin(ac), jnp.sin(ac)], axis=1)
    cos = jnp.concatenate([jnp.ones((ctx, MLA_ROPE), F32), cos], axis=0)
    sin = jnp.concatenate([jnp.zeros((ctx, MLA_ROPE), F32), sin], axis=0)
    return {"ck": jnp.tile(cos, (1, DIFF_MAPS)), "sk": jnp.tile(sin, (1, DIFF_MAPS)), "ct": cos.T, "st": sin.T}


def _partner_perm(width):
    idx = jnp.arange(width)
    r = idx % 16
    return jnp.where(r < 8, idx + 8, idx - 8)


def _layer_weights(l, p):
    d = p["w_in"].shape[1]
    w_in = p["w_in"][l]
    o_diff = 2 * SSD_INNER + 2 * SSD_BC + 2 * SSD_HEADS
    o_mla = o_diff + 3 * DIFF_HEADS * DIFF_V
    nk = DIFF_MAPS * DIFF_QK
    w_ssd = w_in[:, :o_diff]
    wa = jnp.concatenate([w_ssd, jnp.zeros((d, LANES - 2 * SSD_HEADS), F32)], axis=1)
    wq = w_in[:, o_diff:o_diff + nk]
    wk = w_in[:, o_diff + nk:o_diff + 2 * nk]
    wv = w_in[:, o_diff + 2 * nk:o_mla]
    wcq =w_in[:, o_mla:o_mla + MLA_Q_LORA]
    wckv = w_in[:, o_mla + MLA_Q_LORA:o_mla + MLA_Q_LORA + MLA_KV_LORA]
    wkr = w_in[:, o_mla + MLA_Q_LORA + MLA_KV_LORA:]
    zeros = lambda n: jnp.zeros((d, n), F32)
    wm = jnp.concatenate([wcq, zeros(256 - MLA_Q_LORA), wckv, wkr, wkr[:, _partner_perm(MLA_ROPE)],
                          zeros(512 - 448)], axis=1)

    wqu = p["mla_w_q_up"][l].reshape(MLA_Q_LORA, MLA_HEADS, MLA_NOPE + MLA_ROPE)
    pad = jnp.zeros((MLA_Q_LORA, MLA_HEADS, MLA_QK_PAD - MLA_NOPE - MLA_ROPE), F32)
    wqu_plain = jnp.concatenate([wqu, pad], axis=2).reshape(MLA_Q_LORA, -1)
    wkvu = p["mla_w_kv_up"][l].reshape(MLA_KV_LORA, MLA_HEADS, MLA_NOPE + MLA_V)
    wk2 = jnp.concatenate([wkvu[:, :, :MLA_NOPE],
                           jnp.zeros((MLA_KV_LORA, MLA_HEADS, MLA_QK_PAD - MLA_NOPE), F32)],
                          axis=2).reshape(MLA_KV_LORA, -1)
    eye = jnp.eye(MLA_ROPE, dtype=F32)
    ek_h = jnp.concatenate([jnp.zeros((MLA_ROPE, MLA_NOPE), F32), eye,
                            jnp.zeros((MLA_ROPE, MLA_QK_PAD - MLA_NOPE - MLA_ROPE), F32)], axis=1)
    ek = jnp.tile(ek_h, (1, MLA_HEADS))
    wv2 = wkvu[:, :, MLA_NOPE:].reshape(MLA_KV_LORA, -1)

    w_out = p["w_out"][l]
    row = lambda a: a.reshape(1, -1)
    col = lambda a: a.reshape(-1, 1)
    return {
        "norm_mix_pre": row(p["norm_mix_pre"][l]), "norm_mix_post": row(p["norm_mix_post"][l]),
        "norm_ffn_pre": row(p["norm_ffn_pre"][l]), "norm_ffn_post": row(p["norm_ffn_post"][l]),
        "wa": wa.astype(BF16),
        "wdk": wk.astype(BF16),
        "wm": wm.astype(BF16),
        "wqt": wq.T.astype(BF16),
        "wvt": wv.T.astype(BF16),
        "qnw": row(p["mla_q_norm"][l]), "kvnw": row(p["mla_kv_norm"][l]),
        "wqut": wqu_plain.T.astype(BF16),
        "wk2": wk2.astype(BF16), "ek": ek.astype(BF16), "wvt2": wv2.T.astype(BF16),
        "conv_w": p["ssd_conv_w"][l], "conv_b": row(p["ssd_conv_b"][l]),
        "alog": col(p["ssd_a_log"][l]), "dtb": col(p["ssd_dt_bias"][l]),
        "dsk": row(p["ssd_d"][l]), "ssd_norm": row(p["ssd_norm"][l]),
        "lq1": row(p["diff_lam_q1"][l]), "lk1": row(p["diff_lam_k1"][l]),
        "lq2": row(p["diff_lam_q2"][l]), "lk2": row(p["diff_lam_k2"][l]),
        "subw": p["diff_subln"][l].reshape(-1, 1),
        "wo_s": w_out[:SSD_INNER].astype(BF16),
        "wo_d": w_out[SSD_INNER:SSD_INNER + DIFF_HEADS * DIFF_V].astype(BF16),
        "wo_a": w_out[SSD_INNER + DIFF_HEADS * DIFF_V:].astype(BF16),
        "rwt": p["router_w"][l].T, "rb": p["router_b"][l].reshape(-1, 1),
        "w_gate": p["w_gate"], "w_up": p["w_up"], "w_down": p["w_down"],
    }


def kernel(x, c, ctx, c_ctx, ada_w, ada_b, norm_mix_pre, norm_mix_post, norm_ffn_pre, norm_ffn_post, w_in, ssd_conv_w, ssd_conv_b, ssd_a_log, ssd_dt_bias, ssd_d, ssd_norm, diff_lam_q1, diff_lam_k1, diff_lam_q2, diff_lam_k2, diff_subln, mla_q_norm, mla_w_q_up, mla_kv_norm, mla_w_kv_up, w_out, router_w, router_b, w_gate, w_up, w_down):
    p = dict(norm_mix_pre=norm_mix_pre, norm_mix_post=norm_mix_post, norm_ffn_pre=norm_ffn_pre,
             norm_ffn_post=norm_ffn_post, w_in=w_in, ssd_conv_w=ssd_conv_w, ssd_conv_b=ssd_conv_b,
             ssd_a_log=ssd_a_log, ssd_dt_bias=ssd_dt_bias, ssd_d=ssd_d, ssd_norm=ssd_norm,
             diff_lam_q1=diff_lam_q1, diff_lam_k1=diff_lam_k1, diff_lam_q2=diff_lam_q2, diff_lam_k2=diff_lam_k2,
             diff_subln=diff_subln, mla_q_norm=mla_q_norm, mla_w_q_up=mla_w_q_up, mla_kv_norm=mla_kv_norm,
             mla_w_kv_up=mla_w_kv_up, w_out=w_out, router_w=router_w, router_b=router_b,
             w_gate=w_gate, w_up=w_up, w_down=w_down)
    b, seq, d = x.shape
    nctx = ctx.shape[1]
    depth = ada_w.shape[0]
    assert nctx == TILE and seq % TILE == 0 and seq % GRID_W == 0
    nt = nctx + seq
    caps = (EC_CAPACITY * nctx // N_EXPERTS, EC_CAPACITY * seq // N_EXPERTS)
    assert caps[0] % 16 == 0 and caps[1] % 16 == 0 and caps[0] + caps[1] >= WIN

    cvec = jnp.concatenate([c, c_ctx[None, :], jnp.zeros((8 - b - 1, d), F32)], axis=0)
    mods = _adaln(cvec, ada_w, ada_b).reshape(depth, 8, N_MOD, d)
    tabs = _rope_tables(seq, nctx)
    t = (x, ctx)
    for l in range(depth):
        lw = _layer_weights(l, p)
        lambda_init = 0.8 - 0.6 * math.exp(-0.3 * l)
        mod = jnp.stack([jnp.broadcast_to(mods[l, b], (b, N_MOD, d)), mods[l, :b]], axis=1)
        z, xbc, dt, dq, dk, dv, mq, mk, mv = _inproj(t, mod, lw, tabs)
        s = _ssd(z, xbc, dt, lw)
        da = _diff_attn(dq, dk, dv, lw, lambda_init).reshape(b, DIFF_HEADS * DIFF_V, nt)
        aa = _mla_attn(mq, mk, mv).reshape(b, MLA_HEADS * MLA_V, nt)
        t, hf, aff = _outproj(t, s, da, aa, mod, lw)
        pos, gate, cum = _route(aff, caps)
        cum_flat = cum.reshape(-1)
        xg, gc = _gather(cum_flat, hf, pos, gate, caps[0] + caps[1])
        y = _experts(xg, gc, lw, l)
        t = _combine(cum_flat, t, y, pos, mod, lw, latent_only=(l == depth - 1))
    return t
```
